```python
import jax, jax.numpy as jnp
from jax import lax
import numpy as np

D_MODEL = 1024
BATCH = 8
SEQ = 4096
DEPTH = 1

N_F_GROUPS = 4
F_GROUP_DIM = 128
F_WIDTH = N_F_GROUPS * F_GROUP_DIM
N_S_GROUPS = 4
S_GROUP_DIM = 128
S_WIDTH = N_S_GROUPS * S_GROUP_DIM
CHUNK = 128
IN_WIDTH = F_WIDTH + 2 * S_WIDTH + 2 * D_MODEL
N_EXPERTS = 16
CAPACITY_FACTOR = 2
D_FF_EXPERT = 2048
EPS = 1e-6

kernel_name = "hybrid_fnet_gmlp_ec_moe_block"


def rms_norm(x, g):
    xf = x.astype(jnp.float32)
    y = xf * lax.rsqrt(jnp.mean(xf * xf, axis=-1, keepdims=True) + EPS)
    return (y * g.astype(jnp.float32)).astype(x.dtype)


def fourier_mixer(z):
    b, s, _ = z.shape
    zg = z.reshape(b, s, N_F_GROUPS, F_GROUP_DIM).astype(jnp.float32)
    zf = jnp.fft.fft2(zg, axes=(1, 3), norm="ortho").real
    return zf.reshape(b, s, F_WIDTH).astype(z.dtype)


def spatial_gating_mixer(u, v, ln_g, ln_b, w_s, b_s):
    b, s, _ = u.shape
    u = jax.nn.gelu(u)
    v = jax.nn.gelu(v)
    vg = v.reshape(b, s, N_S_GROUPS, S_GROUP_DIM).astype(jnp.float32)
    mu = jnp.mean(vg, axis=-1, keepdims=True)
    var = jnp.mean(jnp.square(vg - mu), axis=-1, keepdims=True)
    vn = (vg - mu) * lax.rsqrt(var + EPS)
    vn = vn * ln_g.reshape(N_S_GROUPS, S_GROUP_DIM).astype(jnp.float32) + ln_b.reshape(N_S_GROUPS, S_GROUP_DIM).astype(jnp.float32)
    vc = vn.astype(u.dtype).reshape(b, s // CHUNK, CHUNK, N_S_GROUPS, S_GROUP_DIM)
    mixed = jnp.einsum("hpq,bnqhc->bnphc", w_s, vc) + b_s.T[None, None, :, :, None]
    return u * mixed.reshape(b, s, S_WIDTH)


def expert_choice_moe(h, w_router, w_gate_e, w_up_e, w_down_e):
    b, s, _ = h.shape
    cap = CAPACITY_FACTOR * s // N_EXPERTS
    logits = jnp.einsum("bsd,de->bse", h, w_router).astype(jnp.float32)
    affinity = jax.nn.softmax(logits, axis=-1)
    gate_val, tok_idx = lax.top_k(jnp.transpose(affinity, (0, 2, 1)), cap)
    bidx = jnp.arange(b)[:, None, None]
    xe = h[bidx, tok_idx]
    a = jnp.einsum("becd,edf->becf", xe, w_gate_e)
    g = jnp.einsum("becd,edf->becf", xe, w_up_e)
    ye = jnp.einsum("becf,efd->becd", jax.nn.silu(a) * g, w_down_e)
    ye = ye * gate_val[..., None].astype(ye.dtype)
    return jnp.zeros_like(h).at[bidx, tok_idx].add(ye)


def setup_inputs(seed: int = 0) -> dict:
    key = jax.random.key(seed)
    ks = jax.random.split(key, 20)
    nrm = lambda k, shape, scale: jax.random.normal(k, shape, jnp.float32) * scale
    L, D = DEPTH, D_MODEL
    return {
        "x": nrm(ks[0], (BATCH, SEQ, D), 1.0),
        "norm1_g": 1.0 + nrm(ks[1], (L, D), 0.1),
        "w_in": nrm(ks[2], (L, D, IN_WIDTH), D ** -0.5),
        "sgu_ln_g": 1.0 + nrm(ks[3], (L, S_WIDTH), 0.1),
        "sgu_ln_b": nrm(ks[4], (L, S_WIDTH), 0.1),
        "w_spatial": nrm(ks[5], (L, N_S_GROUPS, CHUNK, CHUNK), CHUNK ** -0.5),
        "b_spatial": 1.0 + nrm(ks[6], (L, N_S_GROUPS, CHUNK), 0.1),
        "w_fourier_out": nrm(ks[7], (L, F_WIDTH, D), F_WIDTH ** -0.5),
        "w_sgu_out": nrm(ks[8], (L, S_WIDTH, D), S_WIDTH ** -0.5),
        "w_out": nrm(ks[9], (L, D, D), D ** -0.5),
        "norm2_g": 1.0 + nrm(ks[10], (L, D), 0.1),
        "w_router": nrm(ks[11], (L, D, N_EXPERTS), D ** -0.5),
        "w_gate_e": nrm(ks[12], (L, N_EXPERTS, D, D_FF_EXPERT), D ** -0.5),
        "w_up_e": nrm(ks[13], (L, N_EXPERTS, D, D_FF_EXPERT), D ** -0.5),
        "w_down_e": nrm(ks[14], (L, N_EXPERTS, D_FF_EXPERT, D), D_FF_EXPERT ** -0.5),
        "final_g": 1.0 + nrm(ks[15], (D,), 0.1),
    }


def reference(x, norm1_g, w_in, sgu_ln_g, sgu_ln_b, w_spatial, b_spatial, w_fourier_out, w_sgu_out, w_out,
              norm2_g, w_router, w_gate_e, w_up_e, w_down_e, final_g):
    cuts = [F_WIDTH, F_WIDTH + S_WIDTH, F_WIDTH + 2 * S_WIDTH, F_WIDTH + 2 * S_WIDTH + D_MODEL]
    for l in range(DEPTH):
        h = rms_norm(x, norm1_g[l])
        p = jnp.einsum("bsd,dk->bsk", h, w_in[l])
        z_f, u, v, g_f, g_s = jnp.split(p, cuts, axis=-1)
        y_f = jnp.einsum("bsk,kd->bsd", fourier_mixer(z_f), w_fourier_out[l])
        y_s = jnp.einsum("bsk,kd->bsd",
                         spatial_gating_mixer(u, v, sgu_ln_g[l], sgu_ln_b[l], w_spatial[l], b_spatial[l]),
                         w_sgu_out[l])
        merged = jax.nn.sigmoid(g_f) * y_f + jax.nn.sigmoid(g_s) * y_s
        x = x + jnp.einsum("bsd,de->bse", merged, w_out[l])
        h2 = rms_norm(x, norm2_g[l])
        x = x + expert_choice_moe(h2, w_router[l], w_gate_e[l], w_up_e[l], w_down_e[l])
    return rms_norm(x, final_g)
```

```python
import functools
import math

import jax
import jax.numpy as jnp
from jax import lax
from jax.experimental import pallas as pl
from jax.experimental.pallas import tpu as pltpu

EPS = 1e-6
GROUP = 128
N_GROUPS = 4
F_WIDTH = N_GROUPS * GROUP
S_WIDTH = N_GROUPS * GROUP
CAPACITY_FACTOR = 2
VMEM_LIMIT_V7X = 56 * 1024 * 1024

f32 = jnp.float32
bf16 = jnp.bfloat16


def _dot(a, b):
    return jnp.dot(a, b, preferred_element_type=f32)


def _params(sem, vmem=VMEM_LIMIT_V7X):
    return pltpu.CompilerParams(dimension_semantics=sem, vmem_limit_bytes=vmem)


def _front_kernel(x_ref, g1_ref, win_ref, cs_ref, lng_ref, lnb_ref, ws_ref, bs_ref, wso_ref,
                  a_ref, b_ref, ysg_ref, sgf_ref, mix_ref):
    tm, d = x_ref.shape
    x = x_ref[...]
    ms = jnp.mean(x * x, axis=-1, keepdims=True)
    h = (x * lax.rsqrt(ms + EPS) * g1_ref[...]).astype(bf16)

    def proj(lo, hi):
        return _dot(h, win_ref[:, lo:hi])

    zf = proj(0, F_WIDTH)
    for g in range(N_GROUPS):
        zg = zf[:, g * GROUP:(g + 1) * GROUP].astype(bf16)
        ab = _dot(zg, cs_ref[...])
        a_ref[:, g * GROUP:(g + 1) * GROUP] = ab[:, :GROUP].astype(bf16)
        b_ref[:, g * GROUP:(g + 1) * GROUP] = ab[:, GROUP:].astype(bf16)

    u = jax.nn.gelu(proj(F_WIDTH, F_WIDTH + S_WIDTH))
    v = jax.nn.gelu(proj(F_WIDTH + S_WIDTH, F_WIDTH + 2 * S_WIDTH))
    for g in range(N_GROUPS):
        vg = v[:, g * GROUP:(g + 1) * GROUP]
        mu = jnp.mean(vg, axis=-1, keepdims=True)
        dv = vg - mu
        var = jnp.mean(dv * dv, axis=-1, keepdims=True)
        vn = dv * lax.rsqrt(var + EPS) * lng_ref[:, g * GROUP:(g + 1) * GROUP] \
            + lnb_ref[:, g * GROUP:(g + 1) * GROUP]
        vnb = vn.astype(bf16)
        for c in range(tm // GROUP):
            m = _dot(ws_ref[g], vnb[c * GROUP:(c + 1) * GROUP, :]) + bs_ref[:, g:g + 1]
            mix_ref[c * GROUP:(c + 1) * GROUP, g * GROUP:(g + 1) * GROUP] = m
    sgu = (u * mix_ref[...]).astype(bf16)
    ys = _dot(sgu, wso_ref[...])
    c0 = F_WIDTH + 2 * S_WIDTH
    ysg_ref[...] = (jax.nn.sigmoid(proj(c0 + d, c0 + 2 * d)) * ys).astype(bf16)
    sgf_ref[...] = jax.nn.sigmoid(proj(c0, c0 + d)).astype(bf16)


def _front(x2, g1, win_b, cs_b, lng, lnb, ws_b, bs_t, wso_b, tm):
    t, d = x2.shape
    kin = win_b.shape[1]
    const = lambda *shape: pl.BlockSpec(shape, lambda i: (0,) * len(shape))
    return pl.pallas_call(
        _front_kernel,
        grid=(t // tm,),
        in_specs=[
            pl.BlockSpec((tm, d), lambda i: (i, 0)),
            const(1, d),
            const(d, kin),
            const(GROUP, 2 * GROUP),
            const(1, S_WIDTH),
            const(1, S_WIDTH),
            const(N_GROUPS, GROUP, GROUP),
            const(GROUP, N_GROUPS),
            const(S_WIDTH, d),
        ],
        out_specs=[
            pl.BlockSpec((tm, F_WIDTH), lambda i: (i, 0)),
            pl.BlockSpec((tm, F_WIDTH), lambda i: (i, 0)),
            pl.BlockSpec((tm, d), lambda i: (i, 0)),
            pl.BlockSpec((tm, d), lambda i: (i, 0)),
        ],
        out_shape=[
            jax.ShapeDtypeStruct((t, F_WIDTH), bf16),
            jax.ShapeDtypeStruct((t, F_WIDTH), bf16),
            jax.ShapeDtypeStruct((t, d), bf16),
            jax.ShapeDtypeStruct((t, d), bf16),
        ],
        scratch_shapes=[pltpu.VMEM((tm, S_WIDTH), f32)],
        compiler_params=_params(("parallel",)),
        name="front",
    )(x2, g1, win_b, cs_b, lng, lnb, ws_b, bs_t, wso_b)


def _mix_kernel(dc_ref, ds_ref, a_ref, b_ref, sgf_ref, ysg_ref, x_ref, wfo_ref, wout_ref, g2_ref, wr_ref,
                x1_ref, h2_ref, aff_ref):
    fm = _dot(dc_ref[...], a_ref[...]) + _dot(ds_ref[...], b_ref[...])
    yf = _dot(fm.astype(bf16), wfo_ref[...])
    merged = sgf_ref[...].astype(f32) * yf + ysg_ref[...].astype(f32)
    x1 = x_ref[...] + _dot(merged.astype(bf16), wout_ref[...])
    x1_ref[...] = x1
    ms = jnp.mean(x1 * x1, axis=-1, keepdims=True)
    h2 = x1 * lax.rsqrt(ms + EPS) * g2_ref[...]
    h2_ref[...] = h2
    logits = jnp.dot(h2, wr_ref[...], preferred_element_type=f32, precision=lax.Precision.HIGHEST)
    mx = jnp.max(logits, axis=-1, keepdims=True)
    ex = jnp.exp(logits - mx)
    aff_ref[...] = ex / jnp.sum(ex, axis=-1, keepdims=True)


def _mix(dc, ds, a, b, sgf, ysg, x2, wfo_b, wout_b, g2, wr, nb, s, ts):
    t, d = x2.shape
    ne = wr.shape[1]
    nt = s // ts
    const = lambda *shape: pl.BlockSpec(shape, lambda bi, i: (0,) * len(shape))
    tile = lambda w: pl.BlockSpec((ts, w), lambda bi, i: (bi * nt + i, 0))
    return pl.pallas_call(
        _mix_kernel,
        grid=(nb, nt),
        in_specs=[
            pl.BlockSpec((ts, s), lambda bi, i: (i, 0)),
            pl.BlockSpec((ts, s), lambda bi, i: (i, 0)),
            pl.BlockSpec((s, F_WIDTH), lambda bi, i: (bi, 0)),
            pl.BlockSpec((s, F_WIDTH), lambda bi, i: (bi, 0)),
            tile(d), tile(d), tile(d),
            const(F_WIDTH, d),
            const(d, d),
            const(1, d),
            const(d, ne),
        ],
        out_specs=[tile(d), tile(d), tile(ne)],
        out_shape=[
            jax.ShapeDtypeStruct((t, d), f32),
            jax.ShapeDtypeStruct((t, d), f32),
            jax.ShapeDtypeStruct((t, ne), f32),
        ],
        compiler_params=_params(("parallel", "parallel")),
        name="mix",
    )(dc, ds, a, b, sgf, ysg, x2, wfo_b, wout_b, g2, wr)


def _select_kernel(aff_ref, tri_ref, idx_ref, gate_ref, sel_ref, cs_ref, *, cap):
    s, ne = aff_ref.shape
    nchunk = s // GROUP
    aff = aff_ref[...]

    def bit_step(i, thr):
        cand = thr | jnp.left_shift(jnp.int32(1), 30 - i)
        cnt = jnp.sum((aff >= pltpu.bitcast(cand, f32)).astype(jnp.int32), axis=0, keepdims=True)
        return jnp.where(cnt >= cap, cand, thr)

    thr = lax.fori_loop(0, 31, bit_step, jnp.zeros((1, ne), jnp.int32))
    gt = aff >= pltpu.bitcast(thr + 1, f32)
    eq = jnp.logical_and(aff >= pltpu.bitcast(thr, f32), jnp.logical_not(gt))
    need = (cap - jnp.sum(gt.astype(jnp.int32), axis=0, keepdims=True)).astype(f32)

    def cumsum_rows(src_ref, dst_ref):
        def body(c, carry):
            r0 = pl.multiple_of(c * GROUP, GROUP)
            loc = _dot(tri_ref[...], src_ref[pl.ds(r0, GROUP), :].astype(bf16)) + carry
            dst_ref[pl.ds(r0, GROUP), :] = loc
            return loc[GROUP - 1:GROUP, :]
        lax.fori_loop(0, nchunk, body, jnp.zeros((1, ne), f32))

    sel_ref[...] = eq.astype(f32)
    cumsum_rows(sel_ref, cs_ref)
    sel = jnp.where(gt, 1.0, jnp.where(eq & (cs_ref[...] <= need), 1.0, 0.0))
    sel_ref[...] = sel
    cumsum_rows(sel_ref, cs_ref)
    cs_ref[...] = cs_ref[...] * sel_ref[...]

    slot = (lax.broadcasted_iota(jnp.int32, (1, cap), 1) + 1).astype(f32)
    for e in range(ne):
        def body(c, carry):
            acc_i, acc_g = carry
            r0 = pl.multiple_of(c * GROUP, GROUP)
            hit = cs_ref[pl.ds(r0, GROUP), e:e + 1] == slot
            tok = (lax.broadcasted_iota(jnp.int32, (GROUP, 1), 0) + r0).astype(f32)
            gv = aff_ref[pl.ds(r0, GROUP), e:e + 1]
            acc_i = acc_i + jnp.sum(jnp.where(hit, tok, 0.0), axis=0, keepdims=True)
            acc_g = acc_g + jnp.sum(jnp.where(hit, gv, 0.0), axis=0, keepdims=True)
            return acc_i, acc_g
        acc_i, acc_g = lax.fori_loop(0, nchunk, body, (jnp.zeros((1, cap), f32), jnp.zeros((1, cap), f32)))
        idx_ref[0, e:e + 1, :] = acc_i.astype(jnp.int32)
        gate_ref[0, e:e + 1, :] = acc_g


def _select(aff, tri, nb, s, cap):
    ne = aff.shape[1]
    return pl.pallas_call(
        functools.partial(_select_kernel, cap=cap),
        grid=(nb,),
        in_specs=[
            pl.BlockSpec((s, ne), lambda bi: (bi, 0)),
            pl.BlockSpec((GROUP, GROUP), lambda bi: (0, 0)),
        ],
        out_specs=[
            pl.BlockSpec((1, ne, cap), lambda bi: (bi, 0, 0)),
            pl.BlockSpec((1, ne, cap), lambda bi: (bi, 0, 0)),
        ],
        out_shape=[
            jax.ShapeDtypeStruct((nb, ne, cap), jnp.int32),
            jax.ShapeDtypeStruct((nb, ne, cap), f32),
        ],
        scratch_shapes=[pltpu.VMEM((s, ne), f32), pltpu.VMEM((s, ne), f32)],
        compiler_params=_params(("parallel",)),
        name="select",
    )(aff, tri)


ROWS_PER_STEP = 16


def _gather_kernel(idx_ref, h2_ref, xe_ref, *, ne, cap):
    base = (pl.program_id(0) * ne + pl.program_id(1)) * cap

    def body(i, _):
        r0 = pl.multiple_of(i * ROWS_PER_STEP, ROWS_PER_STEP)
        rows = [h2_ref[pl.ds(idx_ref[base + r0 + j], 1), :] for j in range(ROWS_PER_STEP)]
        xe_ref[0, pl.ds(r0, ROWS_PER_STEP), :] = jnp.concatenate(rows, axis=0).astype(bf16)
        return 0

    lax.fori_loop(0, cap // ROWS_PER_STEP, body, 0)


def _gather(idx_flat, h2, nb, s, ne, cap):
    d = h2.shape[1]
    return pl.pallas_call(
        functools.partial(_gather_kernel, ne=ne, cap=cap),
        grid_spec=pltpu.PrefetchScalarGridSpec(
            num_scalar_prefetch=1,
            grid=(nb, ne),
            in_specs=[pl.BlockSpec((s, d), lambda bi, e, idx: (bi, 0))],
            out_specs=pl.BlockSpec((1, cap, d), lambda bi, e, idx: (e, bi, 0)),
        ),
        out_shape=jax.ShapeDtypeStruct((ne, nb * cap, d), bf16),
        compiler_params=_params(("arbitrary", "arbitrary")),
        name="gather",
    )(idx_flat, h2)


def _expert_kernel(x_ref, wg_ref, wu_ref, wd_ref, gate_ref, y_ref, acc_ref, wgb_ref, wub_ref, wdb_ref, *, sub):
    f = pl.program_id(2)
    nf = pl.num_programs(2)
    tmo = x_ref.shape[1]
    wgb_ref[...] = wg_ref[0].astype(bf16)
    wub_ref[...] = wu_ref[0].astype(bf16)
    wdb_ref[...] = wd_ref[0].astype(bf16)

    @pl.when(f == 0)
    def _():
        acc_ref[...] = jnp.zeros_like(acc_ref)

    def body(si, _):
        r0 = pl.multiple_of(si * sub, sub)
        xs = x_ref[0, pl.ds(r0, sub), :]
        a = _dot(xs, wgb_ref[...])
        g = _dot(xs, wub_ref[...])
        hm = (a * jax.nn.sigmoid(a) * g).astype(bf16)
        acc_ref[pl.ds(r0, sub), :] += _dot(hm, wdb_ref[...])
        return 0

    lax.fori_loop(0, tmo // sub, body, 0)

    @pl.when(f == nf - 1)
    def _():
        y_ref[0] = (acc_ref[...] * gate_ref[0]).astype(bf16)


def _experts(xe, wg, wu, wd, gate_col, tmo, tf, sub):
    ne, m, d = xe.shape
    ff = wg.shape[2]
    return pl.pallas_call(
        functools.partial(_expert_kernel, sub=sub),
        grid=(ne, m // tmo, ff // tf),
        in_specs=[
            pl.BlockSpec((1, tmo, d), lambda e, mi, fi: (e, mi, 0)),
            pl.BlockSpec((1, d, tf), lambda e, mi, fi: (e, 0, fi)),
            pl.BlockSpec((1, d, tf), lambda e, mi, fi: (e, 0, fi)),
            pl.BlockSpec((1, tf, d), lambda e, mi, fi: (e, fi, 0)),
            pl.BlockSpec((1, tmo, 1), lambda e, mi, fi: (e, mi, 0)),
        ],
        out_specs=pl.BlockSpec((1, tmo, d), lambda e, mi, fi: (e, mi, 0)),
        out_shape=jax.ShapeDtypeStruct((ne, m, d), bf16),
        scratch_shapes=[
            pltpu.VMEM((tmo, d), f32),
            pltpu.VMEM((d, tf), bf16),
            pltpu.VMEM((d, tf), bf16),
            pltpu.VMEM((tf, d), bf16),
        ],
        compiler_params=_params(("parallel", "parallel", "arbitrary")),
        name="experts",
    )(xe, wg, wu, wd, gate_col)


def _combine_kernel(idx_ref, ye_ref, moe_ref, *, ne, cap):
    e = pl.program_id(1)
    base = (pl.program_id(0) * ne + e) * cap

    @pl.when(e == 0)
    def _():
        moe_ref[...] = jnp.zeros_like(moe_ref)

    def body(i, _):
        r0 = pl.multiple_of(i * ROWS_PER_STEP, ROWS_PER_STEP)
        blk = ye_ref[0, pl.ds(r0, ROWS_PER_STEP), :].astype(f32)
        for j in range(ROWS_PER_STEP):
            tok = idx_ref[base + r0 + j]
            moe_ref[pl.ds(tok, 1), :] += blk[j:j + 1, :]
        return 0

    lax.fori_loop(0, cap // ROWS_PER_STEP, body, 0)


def _combine(idx_flat, ye, nb, s, ne, cap):
    d = ye.shape[2]
    return pl.pallas_call(
        functools.partial(_combine_kernel, ne=ne, cap=cap),
        grid_spec=pltpu.PrefetchScalarGridSpec(
            num_scalar_prefetch=1,
            grid=(nb, ne),
            in_specs=[pl.BlockSpec((1, cap, d), lambda bi, e, idx: (e, bi, 0))],
            out_specs=pl.BlockSpec((s, d), lambda bi, e, idx: (bi, 0)),
        ),
        out_shape=jax.ShapeDtypeStruct((nb * s, d), f32),
        compiler_params=_params(("arbitrary", "arbitrary")),
        name="combine",
    )(idx_flat, ye)


def _final_kernel(x1_ref, moe_ref, g_ref, o_ref, *, do_norm):
    y = x1_ref[...] + moe_ref[...]
    if do_norm:
        ms = jnp.mean(y * y, axis=-1, keepdims=True)
        y = y * lax.rsqrt(ms + EPS) * g_ref[...]
    o_ref[...] = y


def _final(x1, moe, g, tm, do_norm):
    t, d = x1.shape
    return pl.pallas_call(
        functools.partial(_final_kernel, do_norm=do_norm),
        grid=(t // tm,),
        in_specs=[
            pl.BlockSpec((tm, d), lambda i: (i, 0)),
            pl.BlockSpec((tm, d), lambda i: (i, 0)),
            pl.BlockSpec((1, d), lambda i: (0, 0)),
        ],
        out_specs=pl.BlockSpec((tm, d), lambda i: (i, 0)),
        out_shape=jax.ShapeDtypeStruct((t, d), f32),
        compiler_params=_params(("parallel",)),
        name="final",
    )(x1, moe, g)


def _channel_dft_table():
    k = jnp.arange(GROUP, dtype=jnp.int32)
    ang = ((k[:, None] * k[None, :]) % GROUP).astype(f32) * (2.0 * math.pi / GROUP)
    return jnp.concatenate([jnp.cos(ang), jnp.sin(ang)], axis=1).astype(bf16)


def _sequence_dft_tables(s):
    scale = 1.0 / math.sqrt(s * GROUP)
    inner = 64
    k = jnp.arange(s, dtype=jnp.int32)[None, :]
    n1 = jnp.arange(s // inner, dtype=jnp.int32)[:, None] * inner
    n2 = jnp.arange(inner, dtype=jnp.int32)[:, None]
    w = 2.0 * math.pi / s
    a1 = ((n1 * k) % s).astype(f32) * w
    a2 = ((n2 * k) % s).astype(f32) * w
    c1, s1 = jnp.cos(a1)[:, None, :], jnp.sin(a1)[:, None, :]
    c2, s2 = jnp.cos(a2)[None, :, :] * scale, jnp.sin(a2)[None, :, :] * scale
    dc = (c1 * c2 - s1 * s2).reshape(s, s).astype(bf16)
    dsn = (-(s1 * c2 + c1 * s2)).reshape(s, s).astype(bf16)
    return dc, dsn


def _pick(n, pref):
    return pref if n % pref == 0 else n


def kernel(x, norm1_g, w_in, sgu_ln_g, sgu_ln_b, w_spatial, b_spatial, w_fourier_out, w_sgu_out, w_out,
           norm2_g, w_router, w_gate_e, w_up_e, w_down_e, final_g):
    nb, s, d = x.shape
    depth = norm1_g.shape[0]
    ne = w_router.shape[2]
    ff = w_gate_e.shape[3]
    cap = CAPACITY_FACTOR * s // ne
    t = nb * s
    assert s % GROUP == 0 and cap % ROWS_PER_STEP == 0 and d % GROUP == 0

    cs_tab = _channel_dft_table()
    dc, dsn = _sequence_dft_tables(s)
    tri = (jnp.arange(GROUP)[:, None] >= jnp.arange(GROUP)[None, :]).astype(bf16)

    tm_front = _pick(s, 512)
    ts_mix = _pick(s, 256)
    tmo = _pick(nb * cap, 2048)
    tf = _pick(ff, 512)
    sub = _pick(tmo, 256)

    x2 = x.reshape(t, d)
    for l in range(depth):
        a, b, ysg, sgf = _front(
            x2, norm1_g[l][None, :], w_in[l].astype(bf16), cs_tab,
            sgu_ln_g[l][None, :], sgu_ln_b[l][None, :], w_spatial[l].astype(bf16),
            b_spatial[l].T, w_sgu_out[l].astype(bf16), tm_front)
        x1, h2, aff = _mix(dc, dsn, a, b, sgf, ysg, x2, w_fourier_out[l].astype(bf16),
                           w_out[l].astype(bf16), norm2_g[l][None, :], w_router[l], nb, s, ts_mix)
        idx, gate = _select(aff, tri, nb, s, cap)
        idx_flat = idx.reshape(-1)
        xe = _gather(idx_flat, h2, nb, s, ne, cap)
        gate_col = jnp.transpose(gate, (1, 0, 2)).reshape(ne, nb * cap, 1)
        ye = _experts(xe, w_gate_e[l], w_up_e[l], w_down_e[l], gate_col, tmo, tf, sub)
        moe = _combine(idx_flat, ye, nb, s, ne, cap)
        last = l == depth - 1
        x2 = _final(x1, moe, final_g[None, :], tm_front, last)
    return x2.reshape(nb, s, d)
```

```python
import functools
import math

import jax
import jax.numpy as jnp
from jax import lax
from jax.experimental import pallas as pl
from jax.experimental.pallas import tpu as pltpu

EPS = 1e-6
GROUP = 128
N_GROUPS = 4
F_WIDTH = N_GROUPS * GROUP
S_WIDTH = N_GROUPS * GROUP
CAPACITY_FACTOR = 2
VMEM_LIMIT_V7X = 56 * 1024 * 1024

f32 = jnp.float32
bf16 = jnp.bfloat16


def _dot(a, b):
    return jnp.dot(a, b, preferred_element_type=f32)


def _params(sem, vmem=VMEM_LIMIT_V7X):
    return pltpu.CompilerParams(dimension_semantics=sem, vmem_limit_bytes=vmem)


def _front_kernel(x_ref, g1_ref, win_ref, cs_ref, lng_ref, lnb_ref, ws_ref, bs_ref, wso_ref,
                  a_ref, b_ref, ysg_ref, sgf_ref, mix_ref):
    tm, d = x_ref.shape
    x = x_ref[...]
    ms = jnp.mean(x * x, axis=-1, keepdims=True)
    h = (x * lax.rsqrt(ms + EPS) * g1_ref[...]).astype(bf16)

    def proj(lo, hi):
        return _dot(h, win_ref[:, lo:hi])

    zf = proj(0, F_WIDTH)
    for g in range(N_GROUPS):
        zg = zf[:, g * GROUP:(g + 1) * GROUP].astype(bf16)
        ab = _dot(zg, cs_ref[...])
        a_ref[:, g * GROUP:(g + 1) * GROUP] = ab[:, :GROUP].astype(bf16)
        b_ref[:, g * GROUP:(g + 1) * GROUP] = ab[:, GROUP:].astype(bf16)

    u = jax.nn.gelu(proj(F_WIDTH, F_WIDTH + S_WIDTH))
    v = jax.nn.gelu(proj(F_WIDTH + S_WIDTH, F_WIDTH + 2 * S_WIDTH))
    for g in range(N_GROUPS):
        vg = v[:, g * GROUP:(g + 1) * GROUP]
        mu = jnp.mean(vg, axis=-1, keepdims=True)
        dv = vg - mu
        var = jnp.mean(dv * dv, axis=-1, keepdims=True)
        vn = dv * lax.rsqrt(var + EPS) * lng_ref[:, g * GROUP:(g + 1) * GROUP] \
            + lnb_ref[:, g * GROUP:(g + 1) * GROUP]
        vnb = vn.astype(bf16)
        for c in range(tm // GROUP):
            m = _dot(ws_ref[g], vnb[c * GROUP:(c + 1) * GROUP, :]) + bs_ref[:, g:g + 1]
            mix_ref[c * GROUP:(c + 1) * GROUP, g * GROUP:(g + 1) * GROUP] = m
    sgu = (u * mix_ref[...]).astype(bf16)
    ys = _dot(sgu, wso_ref[...])
    c0 = F_WIDTH + 2 * S_WIDTH
    ysg_ref[...] = (jax.nn.sigmoid(proj(c0 + d, c0 + 2 * d)) * ys).astype(bf16)
    sgf_ref[...] = jax.nn.sigmoid(proj(c0, c0 + d)).astype(bf16)


def _front(x2, g1, win_b, cs_b, lng, lnb, ws_b, bs_t, wso_b, tm):
    t, d = x2.shape
    kin = win_b.shape[1]
    const = lambda *shape: pl.BlockSpec(shape, lambda i: (0,) * len(shape))
    return pl.pallas_call(
        _front_kernel,
        grid=(t // tm,),
        in_specs=[
            pl.BlockSpec((tm, d), lambda i: (i, 0)),
            const(1, d),
            const(d, kin),
            const(GROUP, 2 * GROUP),
            const(1, S_WIDTH),
            const(1, S_WIDTH),
            const(N_GROUPS, GROUP, GROUP),
            const(GROUP, N_GROUPS),
            const(S_WIDTH, d),
        ],
        out_specs=[
            pl.BlockSpec((tm, F_WIDTH), lambda i: (i, 0)),
            pl.BlockSpec((tm, F_WIDTH), lambda i: (i, 0)),
            pl.BlockSpec((tm, d), lambda i: (i, 0)),
            pl.BlockSpec((tm, d), lambda i: (i, 0)),
        ],
        out_shape=[
            jax.ShapeDtypeStruct((t, F_WIDTH), bf16),
            jax.ShapeDtypeStruct((t, F_WIDTH), bf16),
            jax.ShapeDtypeStruct((t, d), bf16),
            jax.ShapeDtypeStruct((t, d), bf16),
        ],
        scratch_shapes=[pltpu.VMEM((tm, S_WIDTH), f32)],
        compiler_params=_params(("parallel",)),
        name="front",
    )(x2, g1, win_b, cs_b, lng, lnb, ws_b, bs_t, wso_b)


def _mix_kernel(dc_ref, ds_ref, a_ref, b_ref, sgf_ref, ysg_ref, x_ref, wfo_ref, wout_ref, g2_ref, wr_ref,
                x1_ref, h2_ref, aff_ref):
    fm = _dot(dc_ref[...], a_ref[...]) + _dot(ds_ref[...], b_ref[...])
    yf = _dot(fm.astype(bf16), wfo_ref[...])
    merged = sgf_ref[...].astype(f32) * yf + ysg_ref[...].astype(f32)
    x1 = x_ref[...] + _dot(merged.astype(bf16), wout_ref[...])
    x1_ref[...] = x1
    ms = jnp.mean(x1 * x1, axis=-1, keepdims=True)
    h2 = x1 * lax.rsqrt(ms + EPS) * g2_ref[...]
    h2_ref[...] = h2
    logits = lax.dot_general(wr_ref[...], h2, (((1,), (1,)), ((), ())),
                             preferred_element_type=f32, precision=lax.Precision.HIGHEST)
    mx = jnp.max(logits, axis=0, keepdims=True)
    ex = jnp.exp(logits - mx)
    aff_ref[...] = ex / jnp.sum(ex, axis=0, keepdims=True)


def _mix(dc, ds, a, b, sgf, ysg, x2, wfo_b, wout_b, g2, wr_t, nb, s, ts):
    t, d = x2.shape
    ne = wr_t.shape[0]
    nt = s // ts
    const = lambda *shape: pl.BlockSpec(shape, lambda bi, i: (0,) * len(shape))
    tile = lambda w: pl.BlockSpec((ts, w), lambda bi, i: (bi * nt + i, 0))
    return pl.pallas_call(
        _mix_kernel,
        grid=(nb, nt),
        in_specs=[
            pl.BlockSpec((ts, s), lambda bi, i: (i, 0)),
            pl.BlockSpec((ts, s), lambda bi, i: (i, 0)),
            pl.BlockSpec((s, F_WIDTH), lambda bi, i: (bi, 0)),
            pl.BlockSpec((s, F_WIDTH), lambda bi, i: (bi, 0)),
            tile(d), tile(d), tile(d),
            const(F_WIDTH, d),
            const(d, d),
            const(1, d),
            const(ne, d),
        ],
        out_specs=[tile(d), tile(d), pl.BlockSpec((ne, ts), lambda bi, i: (0, bi * nt + i))],
        out_shape=[
            jax.ShapeDtypeStruct((t, d), f32),
            jax.ShapeDtypeStruct((t, d), f32),
            jax.ShapeDtypeStruct((ne, t), f32),
        ],
        compiler_params=_params(("parallel", "parallel")),
        name="mix",
    )(dc, ds, a, b, sgf, ysg, x2, wfo_b, wout_b, g2, wr_t)


def _select_kernel(aff_ref, idx_ref, gate_ref, *, cap):
    ne, s = aff_ref.shape
    nbits = s.bit_length() - 1
    aff = aff_ref[...]
    lane = lax.broadcasted_iota(jnp.int32, (ne, s), 1)

    def bit_step(i, thr):
        cand = thr | jnp.left_shift(jnp.int32(1), 30 - i)
        cnt = jnp.sum((aff >= pltpu.bitcast(cand, f32)).astype(jnp.int32), axis=1, keepdims=True)
        return jnp.where(cnt >= cap, cand, thr)

    thr = lax.fori_loop(0, 31, bit_step, jnp.zeros((ne, 1), jnp.int32))
    gt = aff >= pltpu.bitcast(thr + 1, f32)
    eq = jnp.logical_and(aff >= pltpu.bitcast(thr, f32), jnp.logical_not(gt))
    need = cap - jnp.sum(gt.astype(jnp.int32), axis=1, keepdims=True)

    def cumsum_tokens(v):
        k = 1
        while k < s:
            v = v + jnp.where(lane >= k, pltpu.roll(v, k, axis=1), 0)
            k *= 2
        return v

    sel = jnp.logical_or(gt, jnp.logical_and(eq, cumsum_tokens(eq.astype(jnp.int32)) <= need))
    slot = cumsum_tokens(sel.astype(jnp.int32)) - 1
    word = jnp.where(sel, lane | jnp.left_shift(lane - slot, nbits) | (1 << (2 * nbits)), 0)
    gate = aff
    for k in range(nbits):
        step = 1 << k
        inc_word = pltpu.roll(word, s - step, axis=1)
        inc_gate = pltpu.roll(gate, s - step, axis=1)
        take = (jnp.right_shift(inc_word, nbits + k) & 1) == 1
        leave = (jnp.right_shift(word, nbits + k) & 1) == 1
        word = jnp.where(take, inc_word, jnp.where(leave, 0, word))
        gate = jnp.where(take, inc_gate, gate)
    idx_ref[0] = word[:, :cap] & (s - 1)
    gate_ref[0] = gate[:, :cap]


def _select(aff_t, nb, s, cap):
    ne = aff_t.shape[0]
    return pl.pallas_call(
        functools.partial(_select_kernel, cap=cap),
        grid=(nb,),
        in_specs=[pl.BlockSpec((ne, s), lambda bi: (0, bi))],
        out_specs=[
            pl.BlockSpec((1, ne, cap), lambda bi: (bi, 0, 0)),
            pl.BlockSpec((1, ne, cap), lambda bi: (bi, 0, 0)),
        ],
        out_shape=[
            jax.ShapeDtypeStruct((nb, ne, cap), jnp.int32),
            jax.ShapeDtypeStruct((nb, ne, cap), f32),
        ],
        compiler_params=_params(("parallel",)),
        name="select",
    )(aff_t)


ROWS_PER_STEP = 16


def _gather_kernel(idx_ref, h2_ref, xe_ref, *, ne, cap):
    base = (pl.program_id(0) * ne + pl.program_id(1)) * cap

    def body(i, _):
        r0 = pl.multiple_of(i * ROWS_PER_STEP, ROWS_PER_STEP)
        rows = [h2_ref[pl.ds(idx_ref[base + r0 + j], 1), :] for j in range(ROWS_PER_STEP)]
        xe_ref[0, pl.ds(r0, ROWS_PER_STEP), :] = jnp.concatenate(rows, axis=0).astype(bf16)
        return 0

    lax.fori_loop(0, cap // ROWS_PER_STEP, body, 0)


def _gather(idx_flat, h2, nb, s, ne, cap):
    d = h2.shape[1]
    return pl.pallas_call(
        functools.partial(_gather_kernel, ne=ne, cap=cap),
        grid_spec=pltpu.PrefetchScalarGridSpec(
            num_scalar_prefetch=1,
            grid=(nb, ne),
            in_specs=[pl.BlockSpec((s, d), lambda bi, e, idx: (bi, 0))],
            out_specs=pl.BlockSpec((1, cap, d), lambda bi, e, idx: (e, bi, 0)),
        ),
        out_shape=jax.ShapeDtypeStruct((ne, nb * cap, d), bf16),
        compiler_params=_params(("arbitrary", "arbitrary")),
        name="gather",
    )(idx_flat, h2)


def _expert_kernel(x_ref, wg_ref, wu_ref, wd_ref, gate_ref, y_ref, acc_ref, wgb_ref, wub_ref, wdb_ref, *, sub):
    f = pl.program_id(2)
    nf = pl.num_programs(2)
    tmo = x_ref.shape[1]
    wgb_ref[...] = wg_ref[0].astype(bf16)
    wub_ref[...] = wu_ref[0].astype(bf16)
    wdb_ref[...] = wd_ref[0].astype(bf16)

    @pl.when(f == 0)
    def _():
        acc_ref[...] = jnp.zeros_like(acc_ref)

    def body(si, _):
        r0 = pl.multiple_of(si * sub, sub)
        xs = x_ref[0, pl.ds(r0, sub), :]
        a = _dot(xs, wgb_ref[...])
        g = _dot(xs, wub_ref[...])
        hm = (a * jax.nn.sigmoid(a) * g).astype(bf16)
        acc_ref[pl.ds(r0, sub), :] += _dot(hm, wdb_ref[...])
        return 0

    lax.fori_loop(0, tmo // sub, body, 0)

    @pl.when(f == nf - 1)
    def _():
        y_ref[0] = (acc_ref[...] * gate_ref[0]).astype(bf16)


def _experts(xe, wg, wu, wd, gate_col, tmo, tf, sub):
    ne, m, d = xe.shape
    ff = wg.shape[2]
    return pl.pallas_call(
        functools.partial(_expert_kernel, sub=sub),
        grid=(ne, m // tmo, ff // tf),
        in_specs=[
            pl.BlockSpec((1, tmo, d), lambda e, mi, fi: (e, mi, 0)),
            pl.BlockSpec((1, d, tf), lambda e, mi, fi: (e, 0, fi)),
            pl.BlockSpec((1, d, tf), lambda e, mi, fi: (e, 0, fi)),
            pl.BlockSpec((1, tf, d), lambda e, mi, fi: (e, fi, 0)),
            pl.BlockSpec((1, tmo, 1), lambda e, mi, fi: (e, mi, 0)),
        ],
        out_specs=pl.BlockSpec((1, tmo, d), lambda e, mi, fi: (e, mi, 0)),
        out_shape=jax.ShapeDtypeStruct((ne, m, d), bf16),
        scratch_shapes=[
            pltpu.VMEM((tmo, d), f32),
            pltpu.VMEM((d, tf), bf16),
            pltpu.VMEM((d, tf), bf16),
            pltpu.VMEM((tf, d), bf16),
        ],
        compiler_params=_params(("parallel", "parallel", "arbitrary")),
        name="experts",
    )(xe, wg, wu, wd, gate_col)


def _combine_kernel(idx_ref, ye_ref, moe_ref, *, ne, cap):
    e = pl.program_id(1)
    base = (pl.program_id(0) * ne + e) * cap

    @pl.when(e == 0)
    def _():
        moe_ref[...] = jnp.zeros_like(moe_ref)

    def body(i, _):
        r0 = pl.multiple_of(i * ROWS_PER_STEP, ROWS_PER_STEP)
        blk = ye_ref[0, pl.ds(r0, ROWS_PER_STEP), :].astype(f32)
        toks = [idx_ref[base + r0 + j] for j in range(ROWS_PER_STEP)]
        rows = [moe_ref[pl.ds(tok, 1), :] for tok in toks]
        for j in range(ROWS_PER_STEP):
            moe_ref[pl.ds(toks[j], 1), :] = rows[j] + blk[j:j + 1, :]
        return 0

    lax.fori_loop(0, cap // ROWS_PER_STEP, body, 0)


def _combine(idx_flat, ye, nb, s, ne, cap):
    d = ye.shape[2]
    return pl.pallas_call(
        functools.partial(_combine_kernel, ne=ne, cap=cap),
        grid_spec=pltpu.PrefetchScalarGridSpec(
            num_scalar_prefetch=1,
            grid=(nb, ne),
            in_specs=[pl.BlockSpec((1, cap, d), lambda bi, e, idx: (e, bi, 0))],
            out_specs=pl.BlockSpec((s, d), lambda bi, e, idx: (bi, 0)),
        ),
        out_shape=jax.ShapeDtypeStruct((nb * s, d), f32),
        compiler_params=_params(("arbitrary", "arbitrary")),
        name="combine",
    )(idx_flat, ye)


def _final_kernel(x1_ref, moe_ref, g_ref, o_ref, *, do_norm):
    y = x1_ref[...] + moe_ref[...]
    if do_norm:
        ms = jnp.mean(y * y, axis=-1, keepdims=True)
        y = y * lax.rsqrt(ms + EPS) * g_ref[...]
    o_ref[...] = y


def _final(x1, moe, g, tm, do_norm):
    t, d = x1.shape
    return pl.pallas_call(
        functools.partial(_final_kernel, do_norm=do_norm),
        grid=(t // tm,),
        in_specs=[
            pl.BlockSpec((tm, d), lambda i: (i, 0)),
            pl.BlockSpec((tm, d), lambda i: (i, 0)),
            pl.BlockSpec((1, d), lambda i: (0, 0)),
        ],
        out_specs=pl.BlockSpec((tm, d), lambda i: (i, 0)),
        out_shape=jax.ShapeDtypeStruct((t, d), f32),
        compiler_params=_params(("parallel",)),
        name="final",
    )(x1, moe, g)


def _channel_dft_table():
    k = jnp.arange(GROUP, dtype=jnp.int32)
    ang = ((k[:, None] * k[None, :]) % GROUP).astype(f32) * (2.0 * math.pi / GROUP)
    return jnp.concatenate([jnp.cos(ang), jnp.sin(ang)], axis=1).astype(bf16)


def _sequence_dft_tables(s):
    scale = 1.0 / math.sqrt(s * GROUP)
    inner = 64
    k = jnp.arange(s, dtype=jnp.int32)[None, :]
    n1 = jnp.arange(s // inner, dtype=jnp.int32)[:, None] * inner
    n2 = jnp.arange(inner, dtype=jnp.int32)[:, None]
    w = 2.0 * math.pi / s
    a1 = ((n1 * k) % s).astype(f32) * w
    a2 = ((n2 * k) % s).astype(f32) * w
    c1, s1 = jnp.cos(a1)[:, None, :], jnp.sin(a1)[:, None, :]
    c2, s2 = jnp.cos(a2)[None, :, :] * scale, jnp.sin(a2)[None, :, :] * scale
    dc = (c1 * c2 - s1 * s2).reshape(s, s).astype(bf16)
    dsn = (-(s1 * c2 + c1 * s2)).reshape(s, s).astype(bf16)
    return dc, dsn


def _pick(n, pref):
    return pref if n % pref == 0 else n


def kernel(x, norm1_g, w_in, sgu_ln_g, sgu_ln_b, w_spatial, b_spatial, w_fourier_out, w_sgu_out, w_out,
           norm2_g, w_router, w_gate_e, w_up_e, w_down_e, final_g):
    nb, s, d = x.shape
    depth = norm1_g.shape[0]
    ne = w_router.shape[2]
    ff = w_gate_e.shape[3]
    cap = CAPACITY_FACTOR * s // ne
    t = nb * s
    assert s % GROUP == 0 and s & (s - 1) == 0 and cap % ROWS_PER_STEP == 0 and d % GROUP == 0

    cs_tab = _channel_dft_table()
    dc, dsn = _sequence_dft_tables(s)

    tm_front = _pick(s, 512)
    ts_mix = _pick(s, 256)
    tmo = _pick(nb * cap, 2048)
    tf = _pick(ff, 512)
    sub = _pick(tmo, 512)

    x2 = x.reshape(t, d)
    for l in range(depth):
        a, b, ysg, sgf = _front(
            x2, norm1_g[l][None, :], w_in[l].astype(bf16), cs_tab,
            sgu_ln_g[l][None, :], sgu_ln_b[l][None, :], w_spatial[l].astype(bf16),
            b_spatial[l].T, w_sgu_out[l].astype(bf16), tm_front)
        x1, h2, aff = _mix(dc, dsn, a, b, sgf, ysg, x2, w_fourier_out[l].astype(bf16),
                           w_out[l].astype(bf16), norm2_g[l][None, :], w_router[l].T, nb, s, ts_mix)
        idx, gate = _select(aff, nb, s, cap)
        idx_flat = idx.reshape(-1)
        xe = _gather(idx_flat, h2, nb, s, ne, cap)
        gate_col = jnp.transpose(gate, (1, 0, 2)).reshape(ne, nb * cap, 1)
        ye = _experts(xe, w_gate_e[l], w_up_e[l], w_down_e[l], gate_col, tmo, tf, sub)
        moe = _combine(idx_flat, ye, nb, s, ne, cap)
        last = l == depth - 1
        x2 = _final(x1, moe, final_g[None, :], tm_front, last)
    return x2.reshape(nb, s, d)
```

```python
import functools
import math

import jax
import jax.numpy as jnp
from jax import lax
from jax.experimental import pallas as pl
from jax.experimental.pallas import tpu as pltpu

EPS = 1e-6
GROUP = 128
N_GROUPS = 4
F_WIDTH = N_GROUPS * GROUP
S_WIDTH = N_GROUPS * GROUP
CAPACITY_FACTOR = 2
VMEM_LIMIT_V7X = 56 * 1024 * 1024

f32 = jnp.float32
bf16 = jnp.bfloat16


def _dot(a, b):
    return jnp.dot(a, b, preferred_element_type=f32)


def _params(sem, vmem=VMEM_LIMIT_V7X):
    return pltpu.CompilerParams(dimension_semantics=sem, vmem_limit_bytes=vmem)


def _front_kernel(x_ref, g1_ref, win_ref, cs_ref, lng_ref, lnb_ref, ws_ref, bs_ref, wso_ref,
                  a_ref, b_ref, ysg_ref, sgf_ref, mix_ref):
    tm, d = x_ref.shape
    x = x_ref[...]
    ms = jnp.mean(x * x, axis=-1, keepdims=True)
    h = (x * lax.rsqrt(ms + EPS) * g1_ref[...]).astype(bf16)

    def proj(lo, hi):
        return _dot(h, win_ref[:, lo:hi])

    zf = proj(0, F_WIDTH)
    for g in range(N_GROUPS):
        zg = zf[:, g * GROUP:(g + 1) * GROUP].astype(bf16)
        ab = _dot(zg, cs_ref[...])
        a_ref[:, g * GROUP:(g + 1) * GROUP] = ab[:, :GROUP].astype(bf16)
        b_ref[:, g * GROUP:(g + 1) * GROUP] = ab[:, GROUP:].astype(bf16)

    u = jax.nn.gelu(proj(F_WIDTH, F_WIDTH + S_WIDTH))
    v = jax.nn.gelu(proj(F_WIDTH + S_WIDTH, F_WIDTH + 2 * S_WIDTH))
    for g in range(N_GROUPS):
        vg = v[:, g * GROUP:(g + 1) * GROUP]
        mu = jnp.mean(vg, axis=-1, keepdims=True)
        dv = vg - mu
        var = jnp.mean(dv * dv, axis=-1, keepdims=True)
        vn = dv * lax.rsqrt(var + EPS) * lng_ref[:, g * GROUP:(g + 1) * GROUP] \
            + lnb_ref[:, g * GROUP:(g + 1) * GROUP]
        vnb = vn.astype(bf16)
        for c in range(tm // GROUP):
            m = _dot(ws_ref[g], vnb[c * GROUP:(c + 1) * GROUP, :]) + bs_ref[:, g:g + 1]
            mix_ref[c * GROUP:(c + 1) * GROUP, g * GROUP:(g + 1) * GROUP] = m
    sgu = (u * mix_ref[...]).astype(bf16)
    ys = _dot(sgu, wso_ref[...])
    c0 = F_WIDTH + 2 * S_WIDTH
    ysg_ref[...] = (jax.nn.sigmoid(proj(c0 + d, c0 + 2 * d)) * ys).astype(bf16)
    sgf_ref[...] = jax.nn.sigmoid(proj(c0, c0 + d)).astype(bf16)


def _front(x2, g1, win_b, cs_b, lng, lnb, ws_b, bs_t, wso_b, tm):
    t, d = x2.shape
    kin = win_b.shape[1]
    const = lambda *shape: pl.BlockSpec(shape, lambda i: (0,) * len(shape))
    return pl.pallas_call(
        _front_kernel,
        grid=(t // tm,),
        in_specs=[
            pl.BlockSpec((tm, d), lambda i: (i, 0)),
            const(1, d),
            const(d, kin),
            const(GROUP, 2 * GROUP),
            const(1, S_WIDTH),
            const(1, S_WIDTH),
            const(N_GROUPS, GROUP, GROUP),
            const(GROUP, N_GROUPS),
            const(S_WIDTH, d),
        ],
        out_specs=[
            pl.BlockSpec((tm, F_WIDTH), lambda i: (i, 0)),
            pl.BlockSpec((tm, F_WIDTH), lambda i: (i, 0)),
            pl.BlockSpec((tm, d), lambda i: (i, 0)),
            pl.BlockSpec((tm, d), lambda i: (i, 0)),
        ],
        out_shape=[
            jax.ShapeDtypeStruct((t, F_WIDTH), bf16),
            jax.ShapeDtypeStruct((t, F_WIDTH), bf16),
            jax.ShapeDtypeStruct((t, d), bf16),
            jax.ShapeDtypeStruct((t, d), bf16),
        ],
        scratch_shapes=[pltpu.VMEM((tm, S_WIDTH), f32)],
        compiler_params=_params(("parallel",)),
        name="front",
    )(x2, g1, win_b, cs_b, lng, lnb, ws_b, bs_t, wso_b)


MIRROR_PAD = 128


def _seqdft_kernel(cp_ref, sm_ref, flip_ref, altrow_ref, j1_ref, e0_ref, a_ref, b_ref, fm_ref,
                   ap_ref, bm_ref, gs_ref, *, scale):
    i = pl.program_id(1)
    ts = fm_ref.shape[0]
    s = a_ref.shape[0]
    half = s // 2
    a_mid = scale * a_ref[half:half + 1, :].astype(f32)

    @pl.when(i == 0)
    def _fold_inputs():
        for j in range(half // GROUP):
            lo = slice(j * GROUP, (j + 1) * GROUP)
            mir = slice(s - (j + 1) * GROUP, s - j * GROUP)
            a_m = _dot(j1_ref[...], a_ref[mir, :])
            b_m = _dot(j1_ref[...], b_ref[mir, :])
            if j > 0:
                nxt = slice(s - j * GROUP, s - (j - 1) * GROUP)
                a_m = a_m + _dot(e0_ref[...], a_ref[nxt, :])
                b_m = b_m + _dot(e0_ref[...], b_ref[nxt, :])
            ap_ref[lo, :] = (a_ref[lo, :].astype(f32) + a_m).astype(bf16)
            bm_ref[lo, :] = (b_ref[lo, :].astype(f32) - b_m).astype(bf16)
        first = lax.broadcasted_iota(jnp.int32, (8, 1), 0) == 0
        gs_ref[half:half + 8, :] = _dot(altrow_ref[...], ap_ref[...]) + jnp.where(first, a_mid, 0.0)
        gs_ref[half + 8:half + MIRROR_PAD, :] = jnp.zeros((MIRROR_PAD - 8, gs_ref.shape[1]), f32)

    @pl.when(i < half // ts)
    def _direct_tile():
        p = _dot(cp_ref[...], ap_ref[...])
        q = _dot(sm_ref[...], bm_ref[...])
        sign = (1 - 2 * (lax.broadcasted_iota(jnp.int32, (ts, 1), 0) & 1)).astype(f32)
        alt = sign * a_mid
        fm_ref[...] = (p + q + alt).astype(bf16)
        gs_ref[pl.ds(pl.multiple_of(i * ts, ts), ts), :] = p - q + alt

    @pl.when(i >= half // ts)
    def _mirror_tile():
        start = pl.multiple_of(s - (i + 1) * ts, ts)
        win = gs_ref[pl.ds(start, ts + MIRROR_PAD), :].astype(bf16)
        fm_ref[...] = _dot(flip_ref[...], win).astype(bf16)


def _seqdft(tables, a, b, nb, s, ts, scale):
    nt = s // ts
    half = s // 2
    nh = half // ts
    cp, sm, flip, altrow, j1, e0 = tables
    const = lambda *shape: pl.BlockSpec(shape, lambda bi, i: (0,) * len(shape))
    table_tile = pl.BlockSpec((ts, half), lambda bi, i: (jnp.minimum(i, nh - 1), 0))
    return pl.pallas_call(
        functools.partial(_seqdft_kernel, scale=scale),
        grid=(nb, nt),
        in_specs=[
            table_tile,
            table_tile,
            const(ts, ts + MIRROR_PAD),
            const(8, half),
            const(GROUP, GROUP),
            const(GROUP, GROUP),
            pl.BlockSpec((s, F_WIDTH), lambda bi, i: (bi, 0)),
            pl.BlockSpec((s, F_WIDTH), lambda bi, i: (bi, 0)),
        ],
        out_specs=pl.BlockSpec((ts, F_WIDTH), lambda bi, i: (bi * nt + i, 0)),
        out_shape=jax.ShapeDtypeStruct((nb * s, F_WIDTH), bf16),
        scratch_shapes=[
            pltpu.VMEM((half, F_WIDTH), bf16),
            pltpu.VMEM((half, F_WIDTH), bf16),
            pltpu.VMEM((half + MIRROR_PAD, F_WIDTH), f32),
        ],
        compiler_params=_params(("parallel", "arbitrary")),
        name="seqdft",
    )(cp, sm, flip, altrow, j1, e0, a, b)


ROUTER_LANES = 128


def _post_kernel(fm_ref, sgf_ref, ysg_ref, x_ref, wfo_ref, wout_ref, g2_ref, wr_ref,
                 x1_ref, h2_ref, aff_ref):
    ne = aff_ref.shape[0]
    yf = _dot(fm_ref[...], wfo_ref[...])
    merged = sgf_ref[...].astype(f32) * yf + ysg_ref[...].astype(f32)
    x1 = x_ref[...] + _dot(merged.astype(bf16), wout_ref[...])
    x1_ref[...] = x1
    ms = jnp.mean(x1 * x1, axis=-1, keepdims=True)
    h2 = x1 * lax.rsqrt(ms + EPS) * g2_ref[...]
    h2_ref[...] = h2
    h_hi = h2.astype(bf16)
    h_lo = (h2 - h_hi.astype(f32)).astype(bf16)
    both = _dot(jnp.concatenate([h_hi, h_lo], axis=1), wr_ref[...])
    logits = both[:, :ROUTER_LANES] + both[:, ROUTER_LANES:]
    logits = logits.T[:ne, :]
    mx = jnp.max(logits, axis=0, keepdims=True)
    ex = jnp.exp(logits - mx)
    aff_ref[...] = ex / jnp.sum(ex, axis=0, keepdims=True)


def _post(fm, sgf, ysg, x2, wfo_b, wout_b, g2, wr_split, ne, tm):
    t, d = x2.shape
    const = lambda *shape: pl.BlockSpec(shape, lambda i: (0,) * len(shape))
    tile = lambda w: pl.BlockSpec((tm, w), lambda i: (i, 0))
    return pl.pallas_call(
        _post_kernel,
        grid=(t // tm,),
        in_specs=[
            tile(F_WIDTH), tile(d), tile(d), tile(d),
            const(F_WIDTH, d),
            const(d, d),
            const(1, d),
            const(2 * d, 2 * ROUTER_LANES),
        ],
        out_specs=[tile(d), tile(d), pl.BlockSpec((ne, tm), lambda i: (0, i))],
        out_shape=[
            jax.ShapeDtypeStruct((t, d), f32),
            jax.ShapeDtypeStruct((t, d), f32),
            jax.ShapeDtypeStruct((ne, t), f32),
        ],
        compiler_params=_params(("parallel",)),
        name="post",
    )(fm, sgf, ysg, x2, wfo_b, wout_b, g2, wr_split)


def _select_kernel(aff_ref, idx_ref, gate_ref, *, cap):
    ne, s = aff_ref.shape
    nbits = s.bit_length() - 1
    aff = aff_ref[...]
    lane = lax.broadcasted_iota(jnp.int32, (ne, s), 1)

    def bit_step(i, thr):
        cand = thr | jnp.left_shift(jnp.int32(1), 30 - i)
        cnt = jnp.sum((aff >= pltpu.bitcast(cand, f32)).astype(jnp.int32), axis=1, keepdims=True)
        return jnp.where(cnt >= cap, cand, thr)

    thr = lax.fori_loop(0, 31, bit_step, jnp.zeros((ne, 1), jnp.int32))
    gt = aff >= pltpu.bitcast(thr + 1, f32)
    eq = jnp.logical_and(aff >= pltpu.bitcast(thr, f32), jnp.logical_not(gt))
    need = cap - jnp.sum(gt.astype(jnp.int32), axis=1, keepdims=True)

    def cumsum_tokens(v):
        k = 1
        while k < s:
            v = v + jnp.where(lane >= k, pltpu.roll(v, k, axis=1), 0)
            k *= 2
        return v

    sel = jnp.logical_or(gt, jnp.logical_and(eq, cumsum_tokens(eq.astype(jnp.int32)) <= need))
    slot = cumsum_tokens(sel.astype(jnp.int32)) - 1
    word = jnp.where(sel, lane | jnp.left_shift(lane - slot, nbits) | (1 << (2 * nbits)), 0)
    gate = aff
    for k in range(nbits):
        step = 1 << k
        inc_word = pltpu.roll(word, s - step, axis=1)
        inc_gate = pltpu.roll(gate, s - step, axis=1)
        take = (jnp.right_shift(inc_word, nbits + k) & 1) == 1
        leave = (jnp.right_shift(word, nbits + k) & 1) == 1
        word = jnp.where(take, inc_word, jnp.where(leave, 0, word))
        gate = jnp.where(take, inc_gate, gate)
    idx_ref[0] = word[:, :cap] & (s - 1)
    gate_ref[0] = gate[:, :cap]


def _select(aff_t, nb, s, cap):
    ne = aff_t.shape[0]
    return pl.pallas_call(
        functools.partial(_select_kernel, cap=cap),
        grid=(nb,),
        in_specs=[pl.BlockSpec((ne, s), lambda bi: (0, bi))],
        out_specs=[
            pl.BlockSpec((1, ne, cap), lambda bi: (bi, 0, 0)),
            pl.BlockSpec((1, ne, cap), lambda bi: (bi, 0, 0)),
        ],
        out_shape=[
            jax.ShapeDtypeStruct((nb, ne, cap), jnp.int32),
            jax.ShapeDtypeStruct((nb, ne, cap), f32),
        ],
        compiler_params=_params(("parallel",)),
        name="select",
    )(aff_t)


ROWS_PER_STEP = 16


def _gather_kernel(idx_ref, h2_ref, xe_ref, *, ne, cap):
    base = (pl.program_id(0) * ne + pl.program_id(1)) * cap

    def body(i, _):
        r0 = pl.multiple_of(i * ROWS_PER_STEP, ROWS_PER_STEP)
        rows = [h2_ref[pl.ds(idx_ref[base + r0 + j], 1), :] for j in range(ROWS_PER_STEP)]
        xe_ref[0, pl.ds(r0, ROWS_PER_STEP), :] = jnp.concatenate(rows, axis=0).astype(bf16)
        return 0

    lax.fori_loop(0, cap // ROWS_PER_STEP, body, 0)


def _gather(idx_flat, h2, nb, s, ne, cap):
    d = h2.shape[1]
    return pl.pallas_call(
        functools.partial(_gather_kernel, ne=ne, cap=cap),
        grid_spec=pltpu.PrefetchScalarGridSpec(
            num_scalar_prefetch=1,
            grid=(nb, ne),
            in_specs=[pl.BlockSpec((s, d), lambda bi, e, idx: (bi, 0))],
            out_specs=pl.BlockSpec((1, cap, d), lambda bi, e, idx: (e, bi, 0)),
        ),
        out_shape=jax.ShapeDtypeStruct((ne, nb * cap, d), bf16),
        compiler_params=_params(("arbitrary", "arbitrary")),
        name="gather",
    )(idx_flat, h2)


def _expert_kernel(x_ref, wg_ref, wu_ref, wd_ref, y_ref, acc_ref, wgb_ref, wub_ref, wdb_ref, *, sub):
    f = pl.program_id(2)
    nf = pl.num_programs(2)
    tmo = x_ref.shape[1]
    wgb_ref[...] = wg_ref[0].astype(bf16)
    wub_ref[...] = wu_ref[0].astype(bf16)
    wdb_ref[...] = wd_ref[0].astype(bf16)

    @pl.when(f == 0)
    def _():
        acc_ref[...] = jnp.zeros_like(acc_ref)

    def body(si, _):
        r0 = pl.multiple_of(si * sub, sub)
        xs = x_ref[0, pl.ds(r0, sub), :]
        a = _dot(xs, wgb_ref[...])
        g = _dot(xs, wub_ref[...])
        hm = (a * jax.nn.sigmoid(a) * g).astype(bf16)
        acc_ref[pl.ds(r0, sub), :] += _dot(hm, wdb_ref[...])
        return 0

    lax.fori_loop(0, tmo // sub, body, 0)

    @pl.when(f == nf - 1)
    def _():
        y_ref[0] = acc_ref[...].astype(bf16)


def _experts(xe, wg, wu, wd, tmo, tf, sub):
    ne, m, d = xe.shape
    ff = wg.shape[2]
    return pl.pallas_call(
        functools.partial(_expert_kernel, sub=sub),
        grid=(ne, m // tmo, ff // tf),
        in_specs=[
            pl.BlockSpec((1, tmo, d), lambda e, mi, fi: (e, mi, 0)),
            pl.BlockSpec((1, d, tf), lambda e, mi, fi: (e, 0, fi)),
            pl.BlockSpec((1, d, tf), lambda e, mi, fi: (e, 0, fi)),
            pl.BlockSpec((1, tf, d), lambda e, mi, fi: (e, fi, 0)),
        ],
        out_specs=pl.BlockSpec((1, tmo, d), lambda e, mi, fi: (e, mi, 0)),
        out_shape=jax.ShapeDtypeStruct((ne, m, d), bf16),
        scratch_shapes=[
            pltpu.VMEM((tmo, d), f32),
            pltpu.VMEM((d, tf), bf16),
            pltpu.VMEM((d, tf), bf16),
            pltpu.VMEM((tf, d), bf16),
        ],
        compiler_params=_params(("parallel", "parallel", "arbitrary")),
        name="experts",
    )(xe, wg, wu, wd)


def _combine_kernel(idx_ref, gate_ref, ye_ref, moe_ref, *, ne, cap):
    e = pl.program_id(1)
    base = (pl.program_id(0) * ne + e) * cap

    @pl.when(e == 0)
    def _():
        moe_ref[...] = jnp.zeros_like(moe_ref)

    def body(i, _):
        r0 = pl.multiple_of(i * ROWS_PER_STEP, ROWS_PER_STEP)
        blk = ye_ref[0, pl.ds(r0, ROWS_PER_STEP), :].astype(f32)
        toks = [idx_ref[base + r0 + j] for j in range(ROWS_PER_STEP)]
        rows = [moe_ref[pl.ds(tok, 1), :] for tok in toks]
        for j in range(ROWS_PER_STEP):
            moe_ref[pl.ds(toks[j], 1), :] = rows[j] + gate_ref[base + r0 + j] * blk[j:j + 1, :]
        return 0

    lax.fori_loop(0, cap // ROWS_PER_STEP, body, 0)


def _combine(idx_flat, gate_flat, ye, nb, s, ne, cap):
    d = ye.shape[2]
    return pl.pallas_call(
        functools.partial(_combine_kernel, ne=ne, cap=cap),
        grid_spec=pltpu.PrefetchScalarGridSpec(
            num_scalar_prefetch=2,
            grid=(nb, ne),
            in_specs=[pl.BlockSpec((1, cap, d), lambda bi, e, idx, gate: (e, bi, 0))],
            out_specs=pl.BlockSpec((s, d), lambda bi, e, idx, gate: (bi, 0)),
        ),
        out_shape=jax.ShapeDtypeStruct((nb * s, d), f32),
        compiler_params=_params(("arbitrary", "arbitrary")),
        name="combine",
    )(idx_flat, gate_flat, ye)


def _final_kernel(x1_ref, moe_ref, g_ref, o_ref, *, do_norm):
    y = x1_ref[...] + moe_ref[...]
    if do_norm:
        ms = jnp.mean(y * y, axis=-1, keepdims=True)
        y = y * lax.rsqrt(ms + EPS) * g_ref[...]
    o_ref[...] = y


def _final(x1, moe, g, tm, do_norm):
    t, d = x1.shape
    return pl.pallas_call(
        functools.partial(_final_kernel, do_norm=do_norm),
        grid=(t // tm,),
        in_specs=[
            pl.BlockSpec((tm, d), lambda i: (i, 0)),
            pl.BlockSpec((tm, d), lambda i: (i, 0)),
            pl.BlockSpec((1, d), lambda i: (0, 0)),
        ],
        out_specs=pl.BlockSpec((tm, d), lambda i: (i, 0)),
        out_shape=jax.ShapeDtypeStruct((t, d), f32),
        compiler_params=_params(("parallel",)),
        name="final",
    )(x1, moe, g)


def _channel_dft_table():
    k = jnp.arange(GROUP, dtype=jnp.int32)
    ang = ((k[:, None] * k[None, :]) % GROUP).astype(f32) * (2.0 * math.pi / GROUP)
    return jnp.concatenate([jnp.cos(ang), jnp.sin(ang)], axis=1).astype(bf16)


def _sequence_dft_tables(s, ts, scale):
    half = s // 2
    inner = 64
    k = jnp.arange(half, dtype=jnp.int32)[None, :]
    n1 = jnp.arange(half // inner, dtype=jnp.int32)[:, None] * inner
    n2 = jnp.arange(inner, dtype=jnp.int32)[:, None]
    w = 2.0 * math.pi / s
    a1 = ((n1 * k) % s).astype(f32) * w
    a2 = ((n2 * k) % s).astype(f32) * w
    c1, s1 = jnp.cos(a1)[:, None, :], jnp.sin(a1)[:, None, :]
    c2, s2 = jnp.cos(a2)[None, :, :] * scale, jnp.sin(a2)[None, :, :] * scale
    cp = (c1 * c2 - s1 * s2).reshape(half, half).astype(bf16)
    sm = (-(s1 * c2 + c1 * s2)).reshape(half, half).astype(bf16)
    rows = jnp.arange(ts, dtype=jnp.int32)[:, None]
    cols = jnp.arange(ts + MIRROR_PAD, dtype=jnp.int32)[None, :]
    flip = (cols == ts - rows).astype(bf16)
    kk = jnp.arange(half, dtype=jnp.int32)[None, :]
    alt = jnp.where(jnp.arange(8)[:, None] == 0, (1 - 2 * (kk & 1)).astype(f32) * scale, 0.0).astype(bf16)
    r = jnp.arange(GROUP, dtype=jnp.int32)
    j1 = ((r[None, :] == GROUP - r[:, None]) & (r[:, None] >= 1)).astype(bf16)
    e0 = ((r[:, None] == 0) & (r[None, :] == 0)).astype(bf16)
    return cp, sm, flip, alt, j1, e0


def _pick(n, pref):
    return pref if n % pref == 0 else n


def kernel(x, norm1_g, w_in, sgu_ln_g, sgu_ln_b, w_spatial, b_spatial, w_fourier_out, w_sgu_out, w_out,
           norm2_g, w_router, w_gate_e, w_up_e, w_down_e, final_g):
    nb, s, d = x.shape
    depth = norm1_g.shape[0]
    ne = w_router.shape[2]
    ff = w_gate_e.shape[3]
    cap = CAPACITY_FACTOR * s // ne
    t = nb * s
    assert s % GROUP == 0 and s & (s - 1) == 0 and cap % ROWS_PER_STEP == 0 and d % GROUP == 0

    tm_front = _pick(s, 512)
    ts_dft = _pick(s // 2, 512)
    tm_post = _pick(s, 512)
    dft_scale = 1.0 / math.sqrt(s * GROUP)
    cs_tab = _channel_dft_table()
    seq_tables = _sequence_dft_tables(s, ts_dft, dft_scale)
    tmo = _pick(nb * cap, 2048)
    tf = _pick(ff, 512)
    sub = _pick(tmo, 1024)

    x2 = x.reshape(t, d)
    for l in range(depth):
        a, b, ysg, sgf = _front(
            x2, norm1_g[l][None, :], w_in[l].astype(bf16), cs_tab,
            sgu_ln_g[l][None, :], sgu_ln_b[l][None, :], w_spatial[l].astype(bf16),
            b_spatial[l].T, w_sgu_out[l].astype(bf16), tm_front)
        fm = _seqdft(seq_tables, a, b, nb, s, ts_dft, dft_scale)
        wr = jnp.pad(w_router[l], ((0, 0), (0, ROUTER_LANES - ne)))
        wr_hi = wr.astype(bf16)
        wr_lo = (wr - wr_hi.astype(f32)).astype(bf16)
        wr_split = jnp.concatenate([jnp.concatenate([wr_hi, wr_lo], axis=1),
                                    jnp.concatenate([wr_hi, jnp.zeros_like(wr_hi)], axis=1)], axis=0)
        x1, h2, aff = _post(fm, sgf, ysg, x2, w_fourier_out[l].astype(bf16), w_out[l].astype(bf16),
                            norm2_g[l][None, :], wr_split, ne, tm_post)
        idx, gate = _select(aff, nb, s, cap)
        idx_flat = idx.reshape(-1)
        xe = _gather(idx_flat, h2, nb, s, ne, cap)
        ye = _experts(xe, w_gate_e[l], w_up_e[l], w_down_e[l], tmo, tf, sub)
        moe = _combine(idx_flat, gate.reshape(-1), ye, nb, s, ne, cap)
        last = l == depth - 1
        x2 = _final(x1, moe, final_g[None, :], tm_front, last)
    return x2.reshape(nb, s, d)
```

```python
import functools
import math

import jax
import jax.numpy as jnp
from jax import lax
from jax.experimental import pallas as pl
from jax.experimental.pallas import tpu as pltpu

EPS = 1e-6
GROUP = 128
N_GROUPS = 4
F_WIDTH = N_GROUPS * GROUP
S_WIDTH = N_GROUPS * GROUP
CAPACITY_FACTOR = 2
VMEM_LIMIT_V7X = 56 * 1024 * 1024

f32 = jnp.float32
bf16 = jnp.bfloat16


def _dot(a, b):
    return jnp.dot(a, b, preferred_element_type=f32)


def _params(sem, vmem=VMEM_LIMIT_V7X):
    return pltpu.CompilerParams(dimension_semantics=sem, vmem_limit_bytes=vmem)


def _front_kernel(x_ref, g1_ref, win_ref, cs_ref, lng_ref, lnb_ref, ws_ref, bs_ref, wso_ref,
                  a_ref, b_ref, ysg_ref, sgf_ref, mix_ref):
    tm, d = x_ref.shape
    x = x_ref[...]
    ms = jnp.mean(x * x, axis=-1, keepdims=True)
    h = (x * lax.rsqrt(ms + EPS) * g1_ref[...]).astype(bf16)

    def proj(lo, hi):
        return _dot(h, win_ref[:, lo:hi])

    zf = proj(0, F_WIDTH)
    for g in range(N_GROUPS):
        zg = zf[:, g * GROUP:(g + 1) * GROUP].astype(bf16)
        ab = _dot(zg, cs_ref[...])
        a_ref[:, g * GROUP:(g + 1) * GROUP] = ab[:, :GROUP].astype(bf16)
        b_ref[:, g * GROUP:(g + 1) * GROUP] = ab[:, GROUP:].astype(bf16)

    u = jax.nn.gelu(proj(F_WIDTH, F_WIDTH + S_WIDTH))
    v = jax.nn.gelu(proj(F_WIDTH + S_WIDTH, F_WIDTH + 2 * S_WIDTH))
    for g in range(N_GROUPS):
        vg = v[:, g * GROUP:(g + 1) * GROUP]
        mu = jnp.mean(vg, axis=-1, keepdims=True)
        dv = vg - mu
        var = jnp.mean(dv * dv, axis=-1, keepdims=True)
        vn = dv * lax.rsqrt(var + EPS) * lng_ref[:, g * GROUP:(g + 1) * GROUP] \
            + lnb_ref[:, g * GROUP:(g + 1) * GROUP]
        vnb = vn.astype(bf16)
        for c in range(tm // GROUP):
            m = _dot(ws_ref[g], vnb[c * GROUP:(c + 1) * GROUP, :]) + bs_ref[:, g:g + 1]
            mix_ref[c * GROUP:(c + 1) * GROUP, g * GROUP:(g + 1) * GROUP] = m
    sgu = (u * mix_ref[...]).astype(bf16)
    ys = _dot(sgu, wso_ref[...])
    c0 = F_WIDTH + 2 * S_WIDTH
    ysg_ref[...] = (jax.nn.sigmoid(proj(c0 + d, c0 + 2 * d)) * ys).astype(bf16)
    sgf_ref[...] = jax.nn.sigmoid(proj(c0, c0 + d)).astype(bf16)


def _front(x2, g1, win_b, cs_b, lng, lnb, ws_b, bs_t, wso_b, tm):
    t, d = x2.shape
    kin = win_b.shape[1]
    const = lambda *shape: pl.BlockSpec(shape, lambda i: (0,) * len(shape))
    return pl.pallas_call(
        _front_kernel,
        grid=(t // tm,),
        in_specs=[
            pl.BlockSpec((tm, d), lambda i: (i, 0)),
            const(1, d),
            const(d, kin),
            const(GROUP, 2 * GROUP),
            const(1, S_WIDTH),
            const(1, S_WIDTH),
            const(N_GROUPS, GROUP, GROUP),
            const(GROUP, N_GROUPS),
            const(S_WIDTH, d),
        ],
        out_specs=[
            pl.BlockSpec((tm, F_WIDTH), lambda i: (i, 0)),
            pl.BlockSpec((tm, F_WIDTH), lambda i: (i, 0)),
            pl.BlockSpec((tm, d), lambda i: (i, 0)),
            pl.BlockSpec((tm, d), lambda i: (i, 0)),
        ],
        out_shape=[
            jax.ShapeDtypeStruct((t, F_WIDTH), bf16),
            jax.ShapeDtypeStruct((t, F_WIDTH), bf16),
            jax.ShapeDtypeStruct((t, d), bf16),
            jax.ShapeDtypeStruct((t, d), bf16),
        ],
        scratch_shapes=[pltpu.VMEM((tm, S_WIDTH), f32)],
        compiler_params=_params(("parallel",)),
        name="front",
    )(x2, g1, win_b, cs_b, lng, lnb, ws_b, bs_t, wso_b)


MIRROR_PAD = 128


def _seqdft_kernel(cp_ref, sm_ref, flip_ref, altrow_ref, j1_ref, e0_ref, a_ref, b_ref, fm_ref,
                   ap_ref, bm_ref, gs_ref, *, scale):
    i = pl.program_id(1)
    ts = fm_ref.shape[0]
    s = a_ref.shape[0]
    half = s // 2
    a_mid = scale * a_ref[half:half + 1, :].astype(f32)

    @pl.when(i == 0)
    def _fold_inputs():
        for j in range(half // GROUP):
            lo = slice(j * GROUP, (j + 1) * GROUP)
            mir = slice(s - (j + 1) * GROUP, s - j * GROUP)
            a_m = _dot(j1_ref[...], a_ref[mir, :])
            b_m = _dot(j1_ref[...], b_ref[mir, :])
            if j > 0:
                nxt = slice(s - j * GROUP, s - (j - 1) * GROUP)
                a_m = a_m + _dot(e0_ref[...], a_ref[nxt, :])
                b_m = b_m + _dot(e0_ref[...], b_ref[nxt, :])
            ap_ref[lo, :] = (a_ref[lo, :].astype(f32) + a_m).astype(bf16)
            bm_ref[lo, :] = (b_ref[lo, :].astype(f32) - b_m).astype(bf16)
        first = lax.broadcasted_iota(jnp.int32, (8, 1), 0) == 0
        gs_ref[half:half + 8, :] = _dot(altrow_ref[...], ap_ref[...]) + jnp.where(first, a_mid, 0.0)
        gs_ref[half + 8:half + MIRROR_PAD, :] = jnp.zeros((MIRROR_PAD - 8, gs_ref.shape[1]), f32)

    @pl.when(i < half // ts)
    def _direct_tile():
        p = _dot(cp_ref[...], ap_ref[...])
        q = _dot(sm_ref[...], bm_ref[...])
        sign = (1 - 2 * (lax.broadcasted_iota(jnp.int32, (ts, 1), 0) & 1)).astype(f32)
        alt = sign * a_mid
        fm_ref[...] = (p + q + alt).astype(bf16)
        gs_ref[pl.ds(pl.multiple_of(i * ts, ts), ts), :] = p - q + alt

    @pl.when(i >= half // ts)
    def _mirror_tile():
        start = pl.multiple_of(s - (i + 1) * ts, ts)
        win = gs_ref[pl.ds(start, ts + MIRROR_PAD), :].astype(bf16)
        fm_ref[...] = _dot(flip_ref[...], win).astype(bf16)


def _seqdft(tables, a, b, nb, s, ts, scale):
    nt = s // ts
    half = s // 2
    nh = half // ts
    cp, sm, flip, altrow, j1, e0 = tables
    const = lambda *shape: pl.BlockSpec(shape, lambda bi, i: (0,) * len(shape))
    table_tile = pl.BlockSpec((ts, half), lambda bi, i: (jnp.minimum(i, nh - 1), 0))
    return pl.pallas_call(
        functools.partial(_seqdft_kernel, scale=scale),
        grid=(nb, nt),
        in_specs=[
            table_tile,
            table_tile,
            const(ts, ts + MIRROR_PAD),
            const(8, half),
            const(GROUP, GROUP),
            const(GROUP, GROUP),
            pl.BlockSpec((s, F_WIDTH), lambda bi, i: (bi, 0)),
            pl.BlockSpec((s, F_WIDTH), lambda bi, i: (bi, 0)),
        ],
        out_specs=pl.BlockSpec((ts, F_WIDTH), lambda bi, i: (bi * nt + i, 0)),
        out_shape=jax.ShapeDtypeStruct((nb * s, F_WIDTH), bf16),
        scratch_shapes=[
            pltpu.VMEM((half, F_WIDTH), bf16),
            pltpu.VMEM((half, F_WIDTH), bf16),
            pltpu.VMEM((half + MIRROR_PAD, F_WIDTH), f32),
        ],
        compiler_params=_params(("parallel", "arbitrary")),
        name="seqdft",
    )(cp, sm, flip, altrow, j1, e0, a, b)


ROUTER_LANES = 128


def _post_kernel(fm_ref, sgf_ref, ysg_ref, x_ref, wfo_ref, wout_ref, g2_ref, wr_ref,
                 x1_ref, h2_ref, aff_ref):
    ne = aff_ref.shape[0]
    yf = _dot(fm_ref[...], wfo_ref[...])
    merged = sgf_ref[...].astype(f32) * yf + ysg_ref[...].astype(f32)
    x1 = x_ref[...] + _dot(merged.astype(bf16), wout_ref[...])
    x1_ref[...] = x1
    ms = jnp.mean(x1 * x1, axis=-1, keepdims=True)
    h2 = x1 * lax.rsqrt(ms + EPS) * g2_ref[...]
    h2_ref[...] = h2
    h_hi = h2.astype(bf16)
    h_lo = (h2 - h_hi.astype(f32)).astype(bf16)
    both = _dot(jnp.concatenate([h_hi, h_lo], axis=1), wr_ref[...])
    logits = both[:, :ROUTER_LANES] + both[:, ROUTER_LANES:]
    logits = logits.T[:ne, :]
    mx = jnp.max(logits, axis=0, keepdims=True)
    ex = jnp.exp(logits - mx)
    aff_ref[...] = ex / jnp.sum(ex, axis=0, keepdims=True)


def _post(fm, sgf, ysg, x2, wfo_b, wout_b, g2, wr_split, ne, tm):
    t, d = x2.shape
    const = lambda *shape: pl.BlockSpec(shape, lambda i: (0,) * len(shape))
    tile = lambda w: pl.BlockSpec((tm, w), lambda i: (i, 0))
    return pl.pallas_call(
        _post_kernel,
        grid=(t // tm,),
        in_specs=[
            tile(F_WIDTH), tile(d), tile(d), tile(d),
            const(F_WIDTH, d),
            const(d, d),
            const(1, d),
            const(2 * d, 2 * ROUTER_LANES),
        ],
        out_specs=[tile(d), tile(d), pl.BlockSpec((ne, tm), lambda i: (0, i))],
        out_shape=[
            jax.ShapeDtypeStruct((t, d), f32),
            jax.ShapeDtypeStruct((t, d), f32),
            jax.ShapeDtypeStruct((ne, t), f32),
        ],
        compiler_params=_params(("parallel",)),
        name="post",
    )(fm, sgf, ysg, x2, wfo_b, wout_b, g2, wr_split)


def _select_kernel(aff_ref, idx_ref, gate_ref, *, cap):
    ne, s = aff_ref.shape
    nbits = s.bit_length() - 1
    aff = aff_ref[...]
    lane = lax.broadcasted_iota(jnp.int32, (ne, s), 1)

    def bit_step(i, thr):
        cand = thr | jnp.left_shift(jnp.int32(1), 30 - i)
        cnt = jnp.sum((aff >= pltpu.bitcast(cand, f32)).astype(jnp.int32), axis=1, keepdims=True)
        return jnp.where(cnt >= cap, cand, thr)

    thr = lax.fori_loop(0, 31, bit_step, jnp.zeros((ne, 1), jnp.int32))
    gt = aff >= pltpu.bitcast(thr + 1, f32)
    eq = jnp.logical_and(aff >= pltpu.bitcast(thr, f32), jnp.logical_not(gt))
    need = cap - jnp.sum(gt.astype(jnp.int32), axis=1, keepdims=True)

    def cumsum_tokens(v):
        k = 1
        while k < s:
            v = v + jnp.where(lane >= k, pltpu.roll(v, k, axis=1), 0)
            k *= 2
        return v

    sel = jnp.logical_or(gt, jnp.logical_and(eq, cumsum_tokens(eq.astype(jnp.int32)) <= need))
    slot = cumsum_tokens(sel.astype(jnp.int32)) - 1
    word = jnp.where(sel, lane | jnp.left_shift(lane - slot, nbits) | (1 << (2 * nbits)), 0)
    gate = aff
    for k in range(nbits):
        step = 1 << k
        inc_word = pltpu.roll(word, s - step, axis=1)
        inc_gate = pltpu.roll(gate, s - step, axis=1)
        take = (jnp.right_shift(inc_word, nbits + k) & 1) == 1
        leave = (jnp.right_shift(word, nbits + k) & 1) == 1
        word = jnp.where(take, inc_word, jnp.where(leave, 0, word))
        gate = jnp.where(take, inc_gate, gate)
    idx_ref[0] = word[:, :cap] & (s - 1)
    gate_ref[0] = gate[:, :cap]


def _select(aff_t, nb, s, cap):
    ne = aff_t.shape[0]
    return pl.pallas_call(
        functools.partial(_select_kernel, cap=cap),
        grid=(nb,),
        in_specs=[pl.BlockSpec((ne, s), lambda bi: (0, bi))],
        out_specs=[
            pl.BlockSpec((1, ne, cap), lambda bi: (bi, 0, 0)),
            pl.BlockSpec((1, ne, cap), lambda bi: (bi, 0, 0)),
        ],
        out_shape=[
            jax.ShapeDtypeStruct((nb, ne, cap), jnp.int32),
            jax.ShapeDtypeStruct((nb, ne, cap), f32),
        ],
        compiler_params=_params(("parallel",)),
        name="select",
    )(aff_t)


ROWS_PER_STEP = 16


def _gather_kernel(idx_ref, h2_ref, xe_ref, *, ne, cap):
    base = (pl.program_id(0) * ne + pl.program_id(1)) * cap

    def body(i, _):
        r0 = pl.multiple_of(i * ROWS_PER_STEP, ROWS_PER_STEP)
        rows = [h2_ref[pl.ds(idx_ref[base + r0 + j], 1), :] for j in range(ROWS_PER_STEP)]
        xe_ref[0, pl.ds(r0, ROWS_PER_STEP), :] = jnp.concatenate(rows, axis=0).astype(bf16)
        return 0

    lax.fori_loop(0, cap // ROWS_PER_STEP, body, 0)


def _gather(idx_flat, h2, nb, s, ne, cap):
    d = h2.shape[1]
    return pl.pallas_call(
        functools.partial(_gather_kernel, ne=ne, cap=cap),
        grid_spec=pltpu.PrefetchScalarGridSpec(
            num_scalar_prefetch=1,
            grid=(nb, ne),
            in_specs=[pl.BlockSpec((s, d), lambda bi, e, idx: (bi, 0))],
            out_specs=pl.BlockSpec((1, cap, d), lambda bi, e, idx: (e, bi, 0)),
        ),
        out_shape=jax.ShapeDtypeStruct((ne, nb * cap, d), bf16),
        compiler_params=_params(("arbitrary", "arbitrary")),
        name="gather",
    )(idx_flat, h2)


def _expert_kernel(x_ref, wg_ref, wu_ref, wd_ref, gate_ref, y_ref, acc_ref, wgb_ref, wub_ref, wdb_ref, *, sub):
    f = pl.program_id(2)
    nf = pl.num_programs(2)
    tmo = x_ref.shape[1]
    wgb_ref[...] = wg_ref[0].astype(bf16)
    wub_ref[...] = wu_ref[0].astype(bf16)
    wdb_ref[...] = wd_ref[0].astype(bf16)

    @pl.when(f == 0)
    def _():
        acc_ref[...] = jnp.zeros_like(acc_ref)

    def body(si, _):
        r0 = pl.multiple_of(si * sub, sub)
        xs = x_ref[0, pl.ds(r0, sub), :]
        a = _dot(xs, wgb_ref[...])
        g = _dot(xs, wub_ref[...])
        hm = (a * jax.nn.sigmoid(a) * g).astype(bf16)
        acc_ref[pl.ds(r0, sub), :] += _dot(hm, wdb_ref[...])
        return 0

    lax.fori_loop(0, tmo // sub, body, 0)

    @pl.when(f == nf - 1)
    def _():
        d = acc_ref.shape[1]
        for c in range(tmo // GROUP):
            rows = slice(c * GROUP, (c + 1) * GROUP)
            g_col = jnp.broadcast_to(gate_ref[0, :, rows], (GROUP, GROUP)).T
            y_ref[0, rows, :] = (acc_ref[rows, :] * jnp.tile(g_col, (1, d // GROUP))).astype(bf16)


def _experts(xe, wg, wu, wd, gate_row, tmo, tf, sub):
    ne, m, d = xe.shape
    ff = wg.shape[2]
    return pl.pallas_call(
        functools.partial(_expert_kernel, sub=sub),
        grid=(ne, m // tmo, ff // tf),
        in_specs=[
            pl.BlockSpec((1, tmo, d), lambda e, mi, fi: (e, mi, 0)),
            pl.BlockSpec((1, d, tf), lambda e, mi, fi: (e, 0, fi)),
            pl.BlockSpec((1, d, tf), lambda e, mi, fi: (e, 0, fi)),
            pl.BlockSpec((1, tf, d), lambda e, mi, fi: (e, fi, 0)),
            pl.BlockSpec((1, 1, tmo), lambda e, mi, fi: (e, 0, mi)),
        ],
        out_specs=pl.BlockSpec((1, tmo, d), lambda e, mi, fi: (e, mi, 0)),
        out_shape=jax.ShapeDtypeStruct((ne, m, d), bf16),
        scratch_shapes=[
            pltpu.VMEM((tmo, d), f32),
            pltpu.VMEM((d, tf), bf16),
            pltpu.VMEM((d, tf), bf16),
            pltpu.VMEM((tf, d), bf16),
        ],
        compiler_params=_params(("parallel", "parallel", "arbitrary")),
        name="experts",
    )(xe, wg, wu, wd, gate_row)


NORM_ROWS = 256


def _combine_kernel(idx_ref, ye_ref, x1_hbm, g_ref, out_ref, x1_ref, sem, *, ne, cap, do_norm):
    b = pl.program_id(0)
    e = pl.program_id(1)
    s = out_ref.shape[0]
    base = (b * ne + e) * cap
    x1_copy = pltpu.make_async_copy(x1_hbm.at[pl.ds(pl.multiple_of(b * s, s), s), :], x1_ref, sem)

    @pl.when(e == 0)
    def _():
        x1_copy.start()
        out_ref[...] = jnp.zeros_like(out_ref)

    def body(i, _):
        r0 = pl.multiple_of(i * ROWS_PER_STEP, ROWS_PER_STEP)
        blk = ye_ref[0, pl.ds(r0, ROWS_PER_STEP), :].astype(f32)
        toks = [idx_ref[base + r0 + j] for j in range(ROWS_PER_STEP)]
        rows = [out_ref[pl.ds(tok, 1), :] for tok in toks]
        for j in range(ROWS_PER_STEP):
            out_ref[pl.ds(toks[j], 1), :] = rows[j] + blk[j:j + 1, :]
        return 0

    lax.fori_loop(0, cap // ROWS_PER_STEP, body, 0)

    @pl.when(e == ne - 1)
    def _():
        x1_copy.wait()

        def norm_rows(c, _):
            rows = pl.ds(pl.multiple_of(c * NORM_ROWS, NORM_ROWS), NORM_ROWS)
            y = x1_ref[rows, :] + out_ref[rows, :]
            if do_norm:
                ms = jnp.mean(y * y, axis=-1, keepdims=True)
                y = y * lax.rsqrt(ms + EPS) * g_ref[...]
            out_ref[rows, :] = y
            return 0

        lax.fori_loop(0, s // NORM_ROWS, norm_rows, 0)


def _combine(idx_flat, ye, x1, g, nb, s, ne, cap, do_norm):
    d = ye.shape[2]
    return pl.pallas_call(
        functools.partial(_combine_kernel, ne=ne, cap=cap, do_norm=do_norm),
        grid_spec=pltpu.PrefetchScalarGridSpec(
            num_scalar_prefetch=1,
            grid=(nb, ne),
            in_specs=[
                pl.BlockSpec((1, cap, d), lambda bi, e, idx: (e, bi, 0)),
                pl.BlockSpec(memory_space=pl.ANY),
                pl.BlockSpec((1, d), lambda bi, e, idx: (0, 0)),
            ],
            out_specs=pl.BlockSpec((s, d), lambda bi, e, idx: (bi, 0)),
            scratch_shapes=[pltpu.VMEM((s, d), f32), pltpu.SemaphoreType.DMA],
        ),
        out_shape=jax.ShapeDtypeStruct((nb * s, d), f32),
        compiler_params=_params(("arbitrary", "arbitrary")),
        name="combine",
    )(idx_flat, ye, x1, g)


def _channel_dft_table():
    k = jnp.arange(GROUP, dtype=jnp.int32)
    ang = ((k[:, None] * k[None, :]) % GROUP).astype(f32) * (2.0 * math.pi / GROUP)
    return jnp.concatenate([jnp.cos(ang), jnp.sin(ang)], axis=1).astype(bf16)


def _sequence_dft_tables(s, ts, scale):
    half = s // 2
    inner = 64
    k = jnp.arange(half, dtype=jnp.int32)[None, :]
    n1 = jnp.arange(half // inner, dtype=jnp.int32)[:, None] * inner
    n2 = jnp.arange(inner, dtype=jnp.int32)[:, None]
    w = 2.0 * math.pi / s
    a1 = ((n1 * k) % s).astype(f32) * w
    a2 = ((n2 * k) % s).astype(f32) * w
    c1, s1 = jnp.cos(a1)[:, None, :], jnp.sin(a1)[:, None, :]
    c2, s2 = jnp.cos(a2)[None, :, :] * scale, jnp.sin(a2)[None, :, :] * scale
    cp = (c1 * c2 - s1 * s2).reshape(half, half).astype(bf16)
    sm = (-(s1 * c2 + c1 * s2)).reshape(half, half).astype(bf16)
    rows = jnp.arange(ts, dtype=jnp.int32)[:, None]
    cols = jnp.arange(ts + MIRROR_PAD, dtype=jnp.int32)[None, :]
    flip = (cols == ts - rows).astype(bf16)
    kk = jnp.arange(half, dtype=jnp.int32)[None, :]
    alt = jnp.where(jnp.arange(8)[:, None] == 0, (1 - 2 * (kk & 1)).astype(f32) * scale, 0.0).astype(bf16)
    r = jnp.arange(GROUP, dtype=jnp.int32)
    j1 = ((r[None, :] == GROUP - r[:, None]) & (r[:, None] >= 1)).astype(bf16)
    e0 = ((r[:, None] == 0) & (r[None, :] == 0)).astype(bf16)
    return cp, sm, flip, alt, j1, e0


def _pick(n, pref):
    return pref if n % pref == 0 else n


def kernel(x, norm1_g, w_in, sgu_ln_g, sgu_ln_b, w_spatial, b_spatial, w_fourier_out, w_sgu_out, w_out,
           norm2_g, w_router, w_gate_e, w_up_e, w_down_e, final_g):
    nb, s, d = x.shape
    depth = norm1_g.shape[0]
    ne = w_router.shape[2]
    ff = w_gate_e.shape[3]
    cap = CAPACITY_FACTOR * s // ne
    t = nb * s
    assert s % NORM_ROWS == 0 and s & (s - 1) == 0 and cap % ROWS_PER_STEP == 0 and d % GROUP == 0

    tm_front = _pick(s, 512)
    ts_dft = _pick(s // 2, 512)
    tm_post = _pick(s, 512)
    dft_scale = 1.0 / math.sqrt(s * GROUP)
    cs_tab = _channel_dft_table()
    seq_tables = _sequence_dft_tables(s, ts_dft, dft_scale)
    tmo = _pick(nb * cap, 2048)
    tf = _pick(ff, 512)
    sub = _pick(tmo, 1024)

    x2 = x.reshape(t, d)
    for l in range(depth):
        a, b, ysg, sgf = _front(
            x2, norm1_g[l][None, :], w_in[l].astype(bf16), cs_tab,
            sgu_ln_g[l][None, :], sgu_ln_b[l][None, :], w_spatial[l].astype(bf16),
            b_spatial[l].T, w_sgu_out[l].astype(bf16), tm_front)
        fm = _seqdft(seq_tables, a, b, nb, s, ts_dft, dft_scale)
        wr = jnp.pad(w_router[l], ((0, 0), (0, ROUTER_LANES - ne)))
        wr_hi = wr.astype(bf16)
        wr_lo = (wr - wr_hi.astype(f32)).astype(bf16)
        wr_split = jnp.concatenate([jnp.concatenate([wr_hi, wr_lo], axis=1),
                                    jnp.concatenate([wr_hi, jnp.zeros_like(wr_hi)], axis=1)], axis=0)
        x1, h2, aff = _post(fm, sgf, ysg, x2, w_fourier_out[l].astype(bf16), w_out[l].astype(bf16),
                            norm2_g[l][None, :], wr_split, ne, tm_post)
        idx, gate = _select(aff, nb, s, cap)
        idx_flat = idx.reshape(-1)
        xe = _gather(idx_flat, h2, nb, s, ne, cap)
        gate_row = jnp.transpose(gate, (1, 0, 2)).reshape(ne, 1, nb * cap)
        ye = _experts(xe, w_gate_e[l], w_up_e[l], w_down_e[l], gate_row, tmo, tf, sub)
        x2 = _combine(idx_flat, ye, x1, final_g[None, :], nb, s, ne, cap, l == depth - 1)
    return x2.reshape(nb, s, d)
```

```python
import functools
import math

import jax
import jax.numpy as jnp
from jax import lax
from jax.experimental import pallas as pl
from jax.experimental.pallas import tpu as pltpu

EPS = 1e-6
GROUP = 128
N_GROUPS = 4
F_WIDTH = N_GROUPS * GROUP
S_WIDTH = N_GROUPS * GROUP
CAPACITY_FACTOR = 2
VMEM_LIMIT_V7X = 56 * 1024 * 1024

f32 = jnp.float32
bf16 = jnp.bfloat16


def _dot(a, b):
    return jnp.dot(a, b, preferred_element_type=f32)


def _params(sem, vmem=VMEM_LIMIT_V7X):
    return pltpu.CompilerParams(dimension_semantics=sem, vmem_limit_bytes=vmem)


def _front_kernel(x_ref, g1_ref, win_ref, cs_ref, lng_ref, lnb_ref, ws_ref, bs_ref, wso_ref,
                  a_ref, b_ref, ysg_ref, sgf_ref, mix_ref):
    tm, d = x_ref.shape
    x = x_ref[...]
    ms = jnp.mean(x * x, axis=-1, keepdims=True)
    h = (x * lax.rsqrt(ms + EPS) * g1_ref[...]).astype(bf16)

    def proj(lo, hi):
        return _dot(h, win_ref[:, lo:hi])

    zf = proj(0, F_WIDTH)
    for g in range(N_GROUPS):
        zg = zf[:, g * GROUP:(g + 1) * GROUP].astype(bf16)
        ab = _dot(zg, cs_ref[...])
        a_ref[:, g * GROUP:(g + 1) * GROUP] = ab[:, :GROUP].astype(bf16)
        b_ref[:, g * GROUP:(g + 1) * GROUP] = ab[:, GROUP:].astype(bf16)

    u = jax.nn.gelu(proj(F_WIDTH, F_WIDTH + S_WIDTH))
    v = jax.nn.gelu(proj(F_WIDTH + S_WIDTH, F_WIDTH + 2 * S_WIDTH))
    for g in range(N_GROUPS):
        vg = v[:, g * GROUP:(g + 1) * GROUP]
        mu = jnp.mean(vg, axis=-1, keepdims=True)
        dv = vg - mu
        var = jnp.mean(dv * dv, axis=-1, keepdims=True)
        vn = dv * lax.rsqrt(var + EPS) * lng_ref[:, g * GROUP:(g + 1) * GROUP] \
            + lnb_ref[:, g * GROUP:(g + 1) * GROUP]
        vnb = vn.astype(bf16)
        for c in range(tm // GROUP):
            m = _dot(ws_ref[g], vnb[c * GROUP:(c + 1) * GROUP, :]) + bs_ref[:, g:g + 1]
            mix_ref[c * GROUP:(c + 1) * GROUP, g * GROUP:(g + 1) * GROUP] = m
    sgu = (u * mix_ref[...]).astype(bf16)
    ys = _dot(sgu, wso_ref[...])
    c0 = F_WIDTH + 2 * S_WIDTH
    ysg_ref[...] = (jax.nn.sigmoid(proj(c0 + d, c0 + 2 * d)) * ys).astype(bf16)
    sgf_ref[...] = jax.nn.sigmoid(proj(c0, c0 + d)).astype(bf16)


def _front(x2, g1, win_b, cs_b, lng, lnb, ws_b, bs_t, wso_b, tm):
    t, d = x2.shape
    kin = win_b.shape[1]
    const = lambda *shape: pl.BlockSpec(shape, lambda i: (0,) * len(shape))
    return pl.pallas_call(
        _front_kernel,
        grid=(t // tm,),
        in_specs=[
            pl.BlockSpec((tm, d), lambda i: (i, 0)),
            const(1, d),
            const(d, kin),
            const(GROUP, 2 * GROUP),
            const(1, S_WIDTH),
            const(1, S_WIDTH),
            const(N_GROUPS, GROUP, GROUP),
            const(GROUP, N_GROUPS),
            const(S_WIDTH, d),
        ],
        out_specs=[
            pl.BlockSpec((tm, F_WIDTH), lambda i: (i, 0)),
            pl.BlockSpec((tm, F_WIDTH), lambda i: (i, 0)),
            pl.BlockSpec((tm, d), lambda i: (i, 0)),
            pl.BlockSpec((tm, d), lambda i: (i, 0)),
        ],
        out_shape=[
            jax.ShapeDtypeStruct((t, F_WIDTH), bf16),
            jax.ShapeDtypeStruct((t, F_WIDTH), bf16),
            jax.ShapeDtypeStruct((t, d), bf16),
            jax.ShapeDtypeStruct((t, d), bf16),
        ],
        scratch_shapes=[pltpu.VMEM((tm, S_WIDTH), f32)],
        compiler_params=_params(("parallel",)),
        name="front",
    )(x2, g1, win_b, cs_b, lng, lnb, ws_b, bs_t, wso_b)


MIRROR_PAD = 128


def _seqdft_kernel(cp_ref, sm_ref, flip_ref, altrow_ref, j1_ref, e0_ref, a_ref, b_ref, fm_ref,
                   ap_ref, bm_ref, gs_ref, *, scale):
    i = pl.program_id(1)
    ts = fm_ref.shape[0]
    s = a_ref.shape[0]
    half = s // 2
    a_mid = scale * a_ref[half:half + 1, :].astype(f32)

    @pl.when(i == 0)
    def _fold_inputs():
        for j in range(half // GROUP):
            lo = slice(j * GROUP, (j + 1) * GROUP)
            mir = slice(s - (j + 1) * GROUP, s - j * GROUP)
            a_m = _dot(j1_ref[...], a_ref[mir, :])
            b_m = _dot(j1_ref[...], b_ref[mir, :])
            if j > 0:
                nxt = slice(s - j * GROUP, s - (j - 1) * GROUP)
                a_m = a_m + _dot(e0_ref[...], a_ref[nxt, :])
                b_m = b_m + _dot(e0_ref[...], b_ref[nxt, :])
            ap_ref[lo, :] = (a_ref[lo, :].astype(f32) + a_m).astype(bf16)
            bm_ref[lo, :] = (b_ref[lo, :].astype(f32) - b_m).astype(bf16)
        first = lax.broadcasted_iota(jnp.int32, (8, 1), 0) == 0
        gs_ref[half:half + 8, :] = _dot(altrow_ref[...], ap_ref[...]) + jnp.where(first, a_mid, 0.0)
        gs_ref[half + 8:half + MIRROR_PAD, :] = jnp.zeros((MIRROR_PAD - 8, gs_ref.shape[1]), f32)

    @pl.when(i < half // ts)
    def _direct_tile():
        p = _dot(cp_ref[...], ap_ref[...])
        q = _dot(sm_ref[...], bm_ref[...])
        sign = (1 - 2 * (lax.broadcasted_iota(jnp.int32, (ts, 1), 0) & 1)).astype(f32)
        alt = sign * a_mid
        fm_ref[...] = (p + q + alt).astype(bf16)
        gs_ref[pl.ds(pl.multiple_of(i * ts, ts), ts), :] = p - q + alt

    @pl.when(i >= half // ts)
    def _mirror_tile():
        start = pl.multiple_of(s - (i + 1) * ts, ts)
        win = gs_ref[pl.ds(start, ts + MIRROR_PAD), :].astype(bf16)
        fm_ref[...] = _dot(flip_ref[...], win).astype(bf16)


def _seqdft(tables, a, b, nb, s, ts, scale):
    nt = s // ts
    half = s // 2
    nh = half // ts
    cp, sm, flip, altrow, j1, e0 = tables
    const = lambda *shape: pl.BlockSpec(shape, lambda bi, i: (0,) * len(shape))
    table_tile = pl.BlockSpec((ts, half), lambda bi, i: (jnp.minimum(i, nh - 1), 0))
    return pl.pallas_call(
        functools.partial(_seqdft_kernel, scale=scale),
        grid=(nb, nt),
        in_specs=[
            table_tile,
            table_tile,
            const(ts, ts + MIRROR_PAD),
            const(8, half),
            const(GROUP, GROUP),
            const(GROUP, GROUP),
            pl.BlockSpec((s, F_WIDTH), lambda bi, i: (bi, 0)),
            pl.BlockSpec((s, F_WIDTH), lambda bi, i: (bi, 0)),
        ],
        out_specs=pl.BlockSpec((ts, F_WIDTH), lambda bi, i: (bi * nt + i, 0)),
        out_shape=jax.ShapeDtypeStruct((nb * s, F_WIDTH), bf16),
        scratch_shapes=[
            pltpu.VMEM((half, F_WIDTH), bf16),
            pltpu.VMEM((half, F_WIDTH), bf16),
            pltpu.VMEM((half + MIRROR_PAD, F_WIDTH), f32),
        ],
        compiler_params=_params(("parallel", "arbitrary")),
        name="seqdft",
    )(cp, sm, flip, altrow, j1, e0, a, b)


ROUTER_LANES = 128


def _post_kernel(fm_ref, sgf_ref, ysg_ref, x_ref, wfo_ref, wout_ref, g2_ref, wr_ref,
                 x1_ref, h2_ref, aff_ref):
    ne = aff_ref.shape[0]
    yf = _dot(fm_ref[...], wfo_ref[...])
    merged = sgf_ref[...].astype(f32) * yf + ysg_ref[...].astype(f32)
    x1 = x_ref[...] + _dot(merged.astype(bf16), wout_ref[...])
    x1_ref[...] = x1
    ms = jnp.mean(x1 * x1, axis=-1, keepdims=True)
    h2 = x1 * lax.rsqrt(ms + EPS) * g2_ref[...]
    h2_ref[...] = h2
    h_hi = h2.astype(bf16)
    h_lo = (h2 - h_hi.astype(f32)).astype(bf16)
    both = _dot(jnp.concatenate([h_hi, h_lo], axis=1), wr_ref[...])
    logits = both[:, :ROUTER_LANES] + both[:, ROUTER_LANES:]
    logits = logits.T[:ne, :]
    mx = jnp.max(logits, axis=0, keepdims=True)
    ex = jnp.exp(logits - mx)
    aff_ref[...] = ex / jnp.sum(ex, axis=0, keepdims=True)


def _post(fm, sgf, ysg, x2, wfo_b, wout_b, g2, wr_split, ne, tm):
    t, d = x2.shape
    const = lambda *shape: pl.BlockSpec(shape, lambda i: (0,) * len(shape))
    tile = lambda w: pl.BlockSpec((tm, w), lambda i: (i, 0))
    return pl.pallas_call(
        _post_kernel,
        grid=(t // tm,),
        in_specs=[
            tile(F_WIDTH), tile(d), tile(d), tile(d),
            const(F_WIDTH, d),
            const(d, d),
            const(1, d),
            const(2 * d, 2 * ROUTER_LANES),
        ],
        out_specs=[tile(d), tile(d), pl.BlockSpec((ne, tm), lambda i: (0, i))],
        out_shape=[
            jax.ShapeDtypeStruct((t, d), f32),
            jax.ShapeDtypeStruct((t, d), f32),
            jax.ShapeDtypeStruct((ne, t), f32),
        ],
        compiler_params=_params(("parallel",)),
        name="post",
    )(fm, sgf, ysg, x2, wfo_b, wout_b, g2, wr_split)


def _select_kernel(aff_ref, idx_ref, gate_ref, *, cap):
    ne, s = aff_ref.shape
    nbits = s.bit_length() - 1
    aff = aff_ref[...]
    lane = lax.broadcasted_iota(jnp.int32, (ne, s), 1)

    def bit_step(i, thr):
        cand = thr | jnp.left_shift(jnp.int32(1), 30 - i)
        cnt = jnp.sum((aff >= pltpu.bitcast(cand, f32)).astype(jnp.int32), axis=1, keepdims=True)
        return jnp.where(cnt >= cap, cand, thr)

    thr = lax.fori_loop(0, 31, bit_step, jnp.zeros((ne, 1), jnp.int32))
    gt = aff >= pltpu.bitcast(thr + 1, f32)
    eq = jnp.logical_and(aff >= pltpu.bitcast(thr, f32), jnp.logical_not(gt))
    need = cap - jnp.sum(gt.astype(jnp.int32), axis=1, keepdims=True)

    def cumsum_tokens(v):
        k = 1
        while k < s:
            v = v + jnp.where(lane >= k, pltpu.roll(v, k, axis=1), 0)
            k *= 2
        return v

    sel = jnp.logical_or(gt, jnp.logical_and(eq, cumsum_tokens(eq.astype(jnp.int32)) <= need))
    slot = cumsum_tokens(sel.astype(jnp.int32)) - 1
    word = jnp.where(sel, lane | jnp.left_shift(lane - slot, nbits) | (1 << (2 * nbits)), 0)
    gate = aff
    for k in range(nbits):
        step = 1 << k
        inc_word = pltpu.roll(word, s - step, axis=1)
        inc_gate = pltpu.roll(gate, s - step, axis=1)
        take = (jnp.right_shift(inc_word, nbits + k) & 1) == 1
        leave = (jnp.right_shift(word, nbits + k) & 1) == 1
        word = jnp.where(take, inc_word, jnp.where(leave, 0, word))
        gate = jnp.where(take, inc_gate, gate)
    idx_ref[0] = word[:, :cap] & (s - 1)
    gate_ref[0] = gate[:, :cap]


def _select(aff_t, nb, s, cap):
    ne = aff_t.shape[0]
    return pl.pallas_call(
        functools.partial(_select_kernel, cap=cap),
        grid=(nb,),
        in_specs=[pl.BlockSpec((ne, s), lambda bi: (0, bi))],
        out_specs=[
            pl.BlockSpec((1, ne, cap), lambda bi: (bi, 0, 0)),
            pl.BlockSpec((1, ne, cap), lambda bi: (bi, 0, 0)),
        ],
        out_shape=[
            jax.ShapeDtypeStruct((nb, ne, cap), jnp.int32),
            jax.ShapeDtypeStruct((nb, ne, cap), f32),
        ],
        compiler_params=_params(("parallel",)),
        name="select",
    )(aff_t)


ROWS_PER_STEP = 16


def _gather_kernel(idx_ref, h2_hbm, xe_ref, buf_ref, sem, *, ne, cap):
    b = pl.program_id(0)
    e = pl.program_id(1)
    s = buf_ref.shape[1]
    slot = b % 2
    base = (b * ne + e) * cap

    def batch_copy(batch, into):
        rows = pl.ds(pl.multiple_of(batch * s, s), s)
        return pltpu.make_async_copy(h2_hbm.at[rows, :], buf_ref.at[into], sem.at[into])

    @pl.when(jnp.logical_and(b == 0, e == 0))
    def _():
        batch_copy(0, 0).start()

    @pl.when(e == 0)
    def _():
        @pl.when(b + 1 < pl.num_programs(0))
        def _():
            batch_copy(b + 1, 1 - slot).start()

        batch_copy(b, slot).wait()

    h2_ref = buf_ref.at[slot]

    def body(i, _):
        r0 = pl.multiple_of(i * ROWS_PER_STEP, ROWS_PER_STEP)
        rows = [h2_ref[pl.ds(idx_ref[base + r0 + j], 1), :] for j in range(ROWS_PER_STEP)]
        xe_ref[0, pl.ds(r0, ROWS_PER_STEP), :] = jnp.concatenate(rows, axis=0).astype(bf16)
        return 0

    lax.fori_loop(0, cap // ROWS_PER_STEP, body, 0)


def _gather(idx_flat, h2, nb, s, ne, cap):
    d = h2.shape[1]
    return pl.pallas_call(
        functools.partial(_gather_kernel, ne=ne, cap=cap),
        grid_spec=pltpu.PrefetchScalarGridSpec(
            num_scalar_prefetch=1,
            grid=(nb, ne),
            in_specs=[pl.BlockSpec(memory_space=pl.ANY)],
            out_specs=pl.BlockSpec((1, cap, d), lambda bi, e, idx: (e, bi, 0)),
            scratch_shapes=[pltpu.VMEM((2, s, d), f32), pltpu.SemaphoreType.DMA((2,))],
        ),
        out_shape=jax.ShapeDtypeStruct((ne, nb * cap, d), bf16),
        compiler_params=_params(("arbitrary", "arbitrary")),
        name="gather",
    )(idx_flat, h2)


def _expert_kernel(x_ref, wg_ref, wu_ref, wd_ref, gate_ref, y_ref, acc_ref, *, sub, nf):
    f = pl.program_id(2)
    tmo, d = acc_ref.shape

    def partial_out(rows):
        xs = x_ref[0, rows, :]
        a = _dot(xs, wg_ref[0].astype(bf16))
        g = _dot(xs, wu_ref[0].astype(bf16))
        hm = (a * jax.nn.sigmoid(a) * g).astype(bf16)
        return _dot(hm, wd_ref[0].astype(bf16))

    def store_gated(rows, y):
        for c in range(rows.start // GROUP, rows.stop // GROUP):
            chunk = slice(c * GROUP, (c + 1) * GROUP)
            g_col = jnp.broadcast_to(gate_ref[0, :, chunk], (GROUP, GROUP)).T
            y_c = y[chunk.start - rows.start:chunk.stop - rows.start, :]
            y_ref[0, chunk, :] = (y_c * jnp.tile(g_col, (1, d // GROUP))).astype(bf16)

    def run(first, last):
        for r0 in range(0, tmo, sub):
            rows = slice(r0, r0 + sub)
            y = partial_out(rows)
            if not first:
                y = acc_ref[rows, :] + y
            if last:
                store_gated(rows, y)
            else:
                acc_ref[rows, :] = y

    if nf == 1:
        run(True, True)
    else:
        pl.when(f == 0)(lambda: run(True, False))
        pl.when(f == nf - 1)(lambda: run(False, True))
        if nf > 2:
            pl.when(jnp.logical_and(f > 0, f < nf - 1))(lambda: run(False, False))


def _experts(xe, wg, wu, wd, gate_row, tmo, tf, sub):
    ne, m, d = xe.shape
    ff = wg.shape[2]
    return pl.pallas_call(
        functools.partial(_expert_kernel, sub=sub, nf=ff // tf),
        grid=(ne, m // tmo, ff // tf),
        in_specs=[
            pl.BlockSpec((1, tmo, d), lambda e, mi, fi: (e, mi, 0)),
            pl.BlockSpec((1, d, tf), lambda e, mi, fi: (e, 0, fi)),
            pl.BlockSpec((1, d, tf), lambda e, mi, fi: (e, 0, fi)),
            pl.BlockSpec((1, tf, d), lambda e, mi, fi: (e, fi, 0)),
            pl.BlockSpec((1, 1, tmo), lambda e, mi, fi: (e, 0, mi)),
        ],
        out_specs=pl.BlockSpec((1, tmo, d), lambda e, mi, fi: (e, mi, 0)),
        out_shape=jax.ShapeDtypeStruct((ne, m, d), bf16),
        scratch_shapes=[pltpu.VMEM((tmo, d), f32)],
        compiler_params=_params(("parallel", "parallel", "arbitrary")),
        name="experts",
    )(xe, wg, wu, wd, gate_row)


NORM_ROWS = 256


def _combine_kernel(idx_ref, ye_ref, x1_hbm, g_ref, out_ref, x1_ref, sem, *, ne, cap, do_norm):
    b = pl.program_id(0)
    e = pl.program_id(1)
    s = out_ref.shape[0]
    base = (b * ne + e) * cap
    x1_copy = pltpu.make_async_copy(x1_hbm.at[pl.ds(pl.multiple_of(b * s, s), s), :], x1_ref, sem)

    @pl.when(e == 0)
    def _():
        x1_copy.start()
        out_ref[...] = jnp.zeros_like(out_ref)

    def body(i, _):
        r0 = pl.multiple_of(i * ROWS_PER_STEP, ROWS_PER_STEP)
        blk = ye_ref[0, pl.ds(r0, ROWS_PER_STEP), :].astype(f32)
        toks = [idx_ref[base + r0 + j] for j in range(ROWS_PER_STEP)]
        rows = [out_ref[pl.ds(tok, 1), :] for tok in toks]
        for j in range(ROWS_PER_STEP):
            out_ref[pl.ds(toks[j], 1), :] = rows[j] + blk[j:j + 1, :]
        return 0

    lax.fori_loop(0, cap // ROWS_PER_STEP, body, 0)

    @pl.when(e == ne - 1)
    def _():
        x1_copy.wait()

        def norm_rows(c, _):
            rows = pl.ds(pl.multiple_of(c * NORM_ROWS, NORM_ROWS), NORM_ROWS)
            y = x1_ref[rows, :] + out_ref[rows, :]
            if do_norm:
                ms = jnp.mean(y * y, axis=-1, keepdims=True)
                y = y * lax.rsqrt(ms + EPS) * g_ref[...]
            out_ref[rows, :] = y
            return 0

        lax.fori_loop(0, s // NORM_ROWS, norm_rows, 0)


def _combine(idx_flat, ye, x1, g, nb, s, ne, cap, do_norm):
    d = ye.shape[2]
    return pl.pallas_call(
        functools.partial(_combine_kernel, ne=ne, cap=cap, do_norm=do_norm),
        grid_spec=pltpu.PrefetchScalarGridSpec(
            num_scalar_prefetch=1,
            grid=(nb, ne),
            in_specs=[
                pl.BlockSpec((1, cap, d), lambda bi, e, idx: (e, bi, 0)),
                pl.BlockSpec(memory_space=pl.ANY),
                pl.BlockSpec((1, d), lambda bi, e, idx: (0, 0)),
            ],
            out_specs=pl.BlockSpec((s, d), lambda bi, e, idx: (bi, 0)),
            scratch_shapes=[pltpu.VMEM((s, d), f32), pltpu.SemaphoreType.DMA],
        ),
        out_shape=jax.ShapeDtypeStruct((nb * s, d), f32),
        compiler_params=_params(("arbitrary", "arbitrary")),
        name="combine",
    )(idx_flat, ye, x1, g)


def _channel_dft_table():
    k = jnp.arange(GROUP, dtype=jnp.int32)
    ang = ((k[:, None] * k[None, :]) % GROUP).astype(f32) * (2.0 * math.pi / GROUP)
    return jnp.concatenate([jnp.cos(ang), jnp.sin(ang)], axis=1).astype(bf16)


def _sequence_dft_tables(s, ts, scale):
    half = s // 2
    inner = 64
    k = jnp.arange(half, dtype=jnp.int32)[None, :]
    n1 = jnp.arange(half // inner, dtype=jnp.int32)[:, None] * inner
    n2 = jnp.arange(inner, dtype=jnp.int32)[:, None]
    w = 2.0 * math.pi / s
    a1 = ((n1 * k) % s).astype(f32) * w
    a2 = ((n2 * k) % s).astype(f32) * w
    c1, s1 = jnp.cos(a1)[:, None, :], jnp.sin(a1)[:, None, :]
    c2, s2 = jnp.cos(a2)[None, :, :] * scale, jnp.sin(a2)[None, :, :] * scale
    cp = (c1 * c2 - s1 * s2).reshape(half, half).astype(bf16)
    sm = (-(s1 * c2 + c1 * s2)).reshape(half, half).astype(bf16)
    rows = jnp.arange(ts, dtype=jnp.int32)[:, None]
    cols = jnp.arange(ts + MIRROR_PAD, dtype=jnp.int32)[None, :]
    flip = (cols == ts - rows).astype(bf16)
    kk = jnp.arange(half, dtype=jnp.int32)[None, :]
    alt = jnp.where(jnp.arange(8)[:, None] == 0, (1 - 2 * (kk & 1)).astype(f32) * scale, 0.0).astype(bf16)
    r = jnp.arange(GROUP, dtype=jnp.int32)
    j1 = ((r[None, :] == GROUP - r[:, None]) & (r[:, None] >= 1)).astype(bf16)
    e0 = ((r[:, None] == 0) & (r[None, :] == 0)).astype(bf16)
    return cp, sm, flip, alt, j1, e0


def _pick(n, pref):
    return pref if n % pref == 0 else n


def kernel(x, norm1_g, w_in, sgu_ln_g, sgu_ln_b, w_spatial, b_spatial, w_fourier_out, w_sgu_out, w_out,
           norm2_g, w_router, w_gate_e, w_up_e, w_down_e, final_g):
    nb, s, d = x.shape
    depth = norm1_g.shape[0]
    ne = w_router.shape[2]
    ff = w_gate_e.shape[3]
    cap = CAPACITY_FACTOR * s // ne
    t = nb * s
    assert s % NORM_ROWS == 0 and s & (s - 1) == 0 and cap % ROWS_PER_STEP == 0 and d % GROUP == 0
    assert (nb * cap) % GROUP == 0

    tm_front = _pick(s, 1024)
    ts_dft = _pick(s // 2, 512)
    tm_post = _pick(s, 1024)
    dft_scale = 1.0 / math.sqrt(s * GROUP)
    cs_tab = _channel_dft_table()
    seq_tables = _sequence_dft_tables(s, ts_dft, dft_scale)
    tmo = _pick(nb * cap, 2048)
    tf = _pick(ff, 512)
    sub = _pick(tmo, 1024)

    x2 = x.reshape(t, d)
    for l in range(depth):
        a, b, ysg, sgf = _front(
            x2, norm1_g[l][None, :], w_in[l].astype(bf16), cs_tab,
            sgu_ln_g[l][None, :], sgu_ln_b[l][None, :], w_spatial[l].astype(bf16),
            b_spatial[l].T, w_sgu_out[l].astype(bf16), tm_front)
        fm = _seqdft(seq_tables, a, b, nb, s, ts_dft, dft_scale)
        wr = jnp.pad(w_router[l], ((0, 0), (0, ROUTER_LANES - ne)))
        wr_hi = wr.astype(bf16)
        wr_lo = (wr - wr_hi.astype(f32)).astype(bf16)
        wr_split = jnp.concatenate([jnp.concatenate([wr_hi, wr_lo], axis=1),
                                    jnp.concatenate([wr_hi, jnp.zeros_like(wr_hi)], axis=1)], axis=0)
        x1, h2, aff = _post(fm, sgf, ysg, x2, w_fourier_out[l].astype(bf16), w_out[l].astype(bf16),
                            norm2_g[l][None, :], wr_split, ne, tm_post)
        idx, gate = _select(aff, nb, s, cap)
        idx_flat = idx.reshape(-1)
        xe = _gather(idx_flat, h2, nb, s, ne, cap)
        gate_row = jnp.transpose(gate, (1, 0, 2)).reshape(ne, 1, nb * cap)
        ye = _experts(xe, w_gate_e[l], w_up_e[l], w_down_e[l], gate_row, tmo, tf, sub)
        x2 = _combine(idx_flat, ye, x1, final_g[None, :], nb, s, ne, cap, l == depth - 1)
    return x2.reshape(nb, s, d)
```

```python
import functools
import math

import jax
import jax.numpy as jnp
from jax import lax
from jax.experimental import pallas as pl
from jax.experimental.pallas import tpu as pltpu

EPS = 1e-6
GROUP = 128
N_GROUPS = 4
F_WIDTH = N_GROUPS * GROUP
S_WIDTH = N_GROUPS * GROUP
CAPACITY_FACTOR = 2
VMEM_LIMIT_V7X = 56 * 1024 * 1024

f32 = jnp.float32
bf16 = jnp.bfloat16


def _dot(a, b):
    return jnp.dot(a, b, preferred_element_type=f32)


def _params(sem, vmem=VMEM_LIMIT_V7X):
    return pltpu.CompilerParams(dimension_semantics=sem, vmem_limit_bytes=vmem)


FRONT_ROWS = 1024


def _front_kernel(x_ref, g1_ref, win_ref, cs_ref, lng_ref, lnb_ref, ws_ref, bs_ref, wso_ref,
                  a_ref, b_ref, ysg_ref, sgf_ref, mix_ref):
    tm, d = x_ref.shape
    c0 = F_WIDTH + 2 * S_WIDTH
    chunk = min(FRONT_ROWS, tm)
    for r0 in range(0, tm, chunk):
        rows = slice(r0, r0 + chunk)
        x = x_ref[rows, :]
        ms = jnp.mean(x * x, axis=-1, keepdims=True)
        h = (x * lax.rsqrt(ms + EPS) * g1_ref[...]).astype(bf16)

        def proj(lo, hi):
            return _dot(h, win_ref[:, lo:hi])

        zf = proj(0, F_WIDTH)
        for g in range(N_GROUPS):
            cols = slice(g * GROUP, (g + 1) * GROUP)
            ab = _dot(zf[:, cols].astype(bf16), cs_ref[...])
            a_ref[rows, cols] = ab[:, :GROUP].astype(bf16)
            b_ref[rows, cols] = ab[:, GROUP:].astype(bf16)

        u = jax.nn.gelu(proj(F_WIDTH, F_WIDTH + S_WIDTH))
        v = jax.nn.gelu(proj(F_WIDTH + S_WIDTH, F_WIDTH + 2 * S_WIDTH))
        for g in range(N_GROUPS):
            cols = slice(g * GROUP, (g + 1) * GROUP)
            vg = v[:, cols]
            mu = jnp.mean(vg, axis=-1, keepdims=True)
            dv = vg - mu
            var = jnp.mean(dv * dv, axis=-1, keepdims=True)
            vnb = (dv * lax.rsqrt(var + EPS) * lng_ref[:, cols] + lnb_ref[:, cols]).astype(bf16)
            for c in range(chunk // GROUP):
                m = _dot(ws_ref[g], vnb[c * GROUP:(c + 1) * GROUP, :]) + bs_ref[:, g:g + 1]
                mix_ref[r0 + c * GROUP:r0 + (c + 1) * GROUP, cols] = m
        sgu = (u * mix_ref[rows, :]).astype(bf16)
        ys = _dot(sgu, wso_ref[...])
        ysg_ref[rows, :] = (jax.nn.sigmoid(proj(c0 + d, c0 + 2 * d)) * ys).astype(bf16)
        sgf_ref[rows, :] = jax.nn.sigmoid(proj(c0, c0 + d)).astype(bf16)


def _front(x2, g1, win_b, cs_b, lng, lnb, ws_b, bs_t, wso_b, tm):
    t, d = x2.shape
    kin = win_b.shape[1]
    const = lambda *shape: pl.BlockSpec(shape, lambda i: (0,) * len(shape))
    return pl.pallas_call(
        _front_kernel,
        grid=(t // tm,),
        in_specs=[
            pl.BlockSpec((tm, d), lambda i: (i, 0)),
            const(1, d),
            const(d, kin),
            const(GROUP, 2 * GROUP),
            const(1, S_WIDTH),
            const(1, S_WIDTH),
            const(N_GROUPS, GROUP, GROUP),
            const(GROUP, N_GROUPS),
            const(S_WIDTH, d),
        ],
        out_specs=[
            pl.BlockSpec((tm, F_WIDTH), lambda i: (i, 0)),
            pl.BlockSpec((tm, F_WIDTH), lambda i: (i, 0)),
            pl.BlockSpec((tm, d), lambda i: (i, 0)),
            pl.BlockSpec((tm, d), lambda i: (i, 0)),
        ],
        out_shape=[
            jax.ShapeDtypeStruct((t, F_WIDTH), bf16),
            jax.ShapeDtypeStruct((t, F_WIDTH), bf16),
            jax.ShapeDtypeStruct((t, d), bf16),
            jax.ShapeDtypeStruct((t, d), bf16),
        ],
        scratch_shapes=[pltpu.VMEM((tm, S_WIDTH), f32)],
        compiler_params=_params(("parallel",)),
        name="front",
    )(x2, g1, win_b, cs_b, lng, lnb, ws_b, bs_t, wso_b)


def _seqdft_kernel(cp_ref, sm_ref, flip_ref, altrow_ref, j1_ref, e0_ref, a_ref, b_ref, fmd_ref, fmm_ref,
                   ap_ref, bm_ref, edge_ref, *, scale):
    i = pl.program_id(1)
    ts = fmd_ref.shape[0]
    s = a_ref.shape[0]
    half = s // 2
    a_mid = scale * a_ref[half:half + 1, :].astype(f32)

    @pl.when(i == 0)
    def _fold_inputs():
        for j in range(half // GROUP):
            lo = slice(j * GROUP, (j + 1) * GROUP)
            mir = slice(s - (j + 1) * GROUP, s - j * GROUP)
            a_m = _dot(j1_ref[...], a_ref[mir, :])
            b_m = _dot(j1_ref[...], b_ref[mir, :])
            if j > 0:
                nxt = slice(s - j * GROUP, s - (j - 1) * GROUP)
                a_m = a_m + _dot(e0_ref[...], a_ref[nxt, :])
                b_m = b_m + _dot(e0_ref[...], b_ref[nxt, :])
            ap_ref[lo, :] = (a_ref[lo, :].astype(f32) + a_m).astype(bf16)
            bm_ref[lo, :] = (b_ref[lo, :].astype(f32) - b_m).astype(bf16)
        first = lax.broadcasted_iota(jnp.int32, (8, 1), 0) == 0
        edge_ref[...] = _dot(altrow_ref[...], ap_ref[...]) + jnp.where(first, a_mid, 0.0)

    p = _dot(cp_ref[...], ap_ref[...])
    q = _dot(sm_ref[...], bm_ref[...])
    row = lax.broadcasted_iota(jnp.int32, (ts, 1), 0)
    alt = (1 - 2 * (row & 1)).astype(f32) * a_mid
    fmd_ref[...] = (p + q + alt).astype(bf16)
    g = p - q + alt
    flipped = _dot(flip_ref[...], g.astype(bf16))
    fmm_ref[...] = jnp.where(row == 0, edge_ref[0:1, :], flipped).astype(bf16)
    edge_ref[...] = g[0:8, :]


def _seqdft(tables, a, b, nb, s, ts, scale):
    half = s // 2
    nh = half // ts
    cp, sm, flip, altrow, j1, e0 = tables
    const = lambda *shape: pl.BlockSpec(shape, lambda bi, i: (0,) * len(shape))
    table_tile = pl.BlockSpec((ts, half), lambda bi, i: (nh - 1 - i, 0))
    return pl.pallas_call(
        functools.partial(_seqdft_kernel, scale=scale),
        grid=(nb, nh),
        in_specs=[
            table_tile,
            table_tile,
            const(ts, ts),
            const(8, half),
            const(GROUP, GROUP),
            const(GROUP, GROUP),
            pl.BlockSpec((s, F_WIDTH), lambda bi, i: (bi, 0)),
            pl.BlockSpec((s, F_WIDTH), lambda bi, i: (bi, 0)),
        ],
        out_specs=[
            pl.BlockSpec((ts, F_WIDTH), lambda bi, i: (bi * nh + nh - 1 - i, 0)),
            pl.BlockSpec((ts, F_WIDTH), lambda bi, i: (bi * nh + i, 0)),
        ],
        out_shape=[
            jax.ShapeDtypeStruct((nb * half, F_WIDTH), bf16),
            jax.ShapeDtypeStruct((nb * half, F_WIDTH), bf16),
        ],
        scratch_shapes=[
            pltpu.VMEM((half, F_WIDTH), bf16),
            pltpu.VMEM((half, F_WIDTH), bf16),
            pltpu.VMEM((8, F_WIDTH), f32),
        ],
        compiler_params=_params(("parallel", "arbitrary")),
        name="seqdft",
    )(cp, sm, flip, altrow, j1, e0, a, b)


ROUTER_LANES = 128
POST_ROWS = 256


def _post_kernel(fmd_ref, fmm_ref, sgf_ref, ysg_ref, x_ref, wfo_ref, wout_ref, g2_ref, wr_ref,
                 x1_ref, h2_ref, aff_ref, *, tiles_per_half):
    ne = aff_ref.shape[0]
    tm = x_ref.shape[0]
    in_first_half = (pl.program_id(0) % (2 * tiles_per_half)) < tiles_per_half
    chunk = min(POST_ROWS, tm)
    for r0 in range(0, tm, chunk):
        rows = slice(r0, r0 + chunk)
        fm = jnp.where(in_first_half, fmd_ref[rows, :], fmm_ref[rows, :])
        yf = _dot(fm, wfo_ref[...])
        merged = sgf_ref[rows, :].astype(f32) * yf + ysg_ref[rows, :].astype(f32)
        x1 = x_ref[rows, :] + _dot(merged.astype(bf16), wout_ref[...])
        x1_ref[rows, :] = x1
        ms = jnp.mean(x1 * x1, axis=-1, keepdims=True)
        h2 = x1 * lax.rsqrt(ms + EPS) * g2_ref[...]
        h2_ref[rows, :] = h2
        h_hi = h2.astype(bf16)
        h_lo = (h2 - h_hi.astype(f32)).astype(bf16)
        both = _dot(jnp.concatenate([h_hi, h_lo], axis=1), wr_ref[...])
        logits = both[:, :ROUTER_LANES] + both[:, ROUTER_LANES:]
        logits = logits.T[:ne, :]
        mx = jnp.max(logits, axis=0, keepdims=True)
        ex = jnp.exp(logits - mx)
        aff_ref[:, rows] = ex / jnp.sum(ex, axis=0, keepdims=True)


def _post(fm_first, fm_second, sgf, ysg, x2, wfo_b, wout_b, g2, wr_split, ne, s, tm):
    t, d = x2.shape
    th = s // 2 // tm
    const = lambda *shape: pl.BlockSpec(shape, lambda i: (0,) * len(shape))
    tile = lambda w: pl.BlockSpec((tm, w), lambda i: (i, 0))
    first = pl.BlockSpec((tm, F_WIDTH), lambda i: (i // (2 * th) * th + jnp.minimum(i % (2 * th), th - 1), 0))
    second = pl.BlockSpec((tm, F_WIDTH), lambda i: (i // (2 * th) * th + jnp.maximum(i % (2 * th) - th, 0), 0))
    return pl.pallas_call(
        functools.partial(_post_kernel, tiles_per_half=th),
        grid=(t // tm,),
        in_specs=[
            first, second, tile(d), tile(d), tile(d),
            const(F_WIDTH, d),
            const(d, d),
            const(1, d),
            const(2 * d, 2 * ROUTER_LANES),
        ],
        out_specs=[tile(d), tile(d), pl.BlockSpec((ne, tm), lambda i: (0, i))],
        out_shape=[
            jax.ShapeDtypeStruct((t, d), f32),
            jax.ShapeDtypeStruct((t, d), f32),
            jax.ShapeDtypeStruct((ne, t), f32),
        ],
        compiler_params=_params(("parallel",)),
        name="post",
    )(fm_first, fm_second, sgf, ysg, x2, wfo_b, wout_b, g2, wr_split)


def _select_kernel(aff_ref, idx_ref, gate_ref, *, cap):
    ne, s = aff_ref.shape
    nbits = s.bit_length() - 1
    aff = aff_ref[...]
    lane = lax.broadcasted_iota(jnp.int32, (ne, s), 1)

    def bit_step(i, thr):
        cand = thr | jnp.left_shift(jnp.int32(1), 30 - i)
        cnt = jnp.sum((aff >= pltpu.bitcast(cand, f32)).astype(jnp.int32), axis=1, keepdims=True)
        return jnp.where(cnt >= cap, cand, thr)

    thr = lax.fori_loop(0, 31, bit_step, jnp.zeros((ne, 1), jnp.int32))
    gt = aff >= pltpu.bitcast(thr + 1, f32)
    eq = jnp.logical_and(aff >= pltpu.bitcast(thr, f32), jnp.logical_not(gt))
    need = cap - jnp.sum(gt.astype(jnp.int32), axis=1, keepdims=True)

    def cumsum_tokens(v):
        k = 1
        while k < s:
            v = v + jnp.where(lane >= k, pltpu.roll(v, k, axis=1), 0)
            k *= 2
        return v

    sel = jnp.logical_or(gt, jnp.logical_and(eq, cumsum_tokens(eq.astype(jnp.int32)) <= need))
    slot = cumsum_tokens(sel.astype(jnp.int32)) - 1
    word = jnp.where(sel, lane | jnp.left_shift(lane - slot, nbits) | (1 << (2 * nbits)), 0)
    gate = aff
    for k in range(nbits):
        step = 1 << k
        inc_word = pltpu.roll(word, s - step, axis=1)
        inc_gate = pltpu.roll(gate, s - step, axis=1)
        take = (jnp.right_shift(inc_word, nbits + k) & 1) == 1
        leave = (jnp.right_shift(word, nbits + k) & 1) == 1
        word = jnp.where(take, inc_word, jnp.where(leave, 0, word))
        gate = jnp.where(take, inc_gate, gate)
    idx_ref[0] = word[:, :cap] & (s - 1)
    gate_ref[0] = gate[:, :cap]


def _select(aff_t, nb, s, cap):
    ne = aff_t.shape[0]
    return pl.pallas_call(
        functools.partial(_select_kernel, cap=cap),
        grid=(nb,),
        in_specs=[pl.BlockSpec((ne, s), lambda bi: (0, bi))],
        out_specs=[
            pl.BlockSpec((1, ne, cap), lambda bi: (bi, 0, 0)),
            pl.BlockSpec((1, ne, cap), lambda bi: (bi, 0, 0)),
        ],
        out_shape=[
            jax.ShapeDtypeStruct((nb, ne, cap), jnp.int32),
            jax.ShapeDtypeStruct((nb, ne, cap), f32),
        ],
        compiler_params=_params(("parallel",)),
        name="select",
    )(aff_t)


ROWS_PER_STEP = 16


def _gather_kernel(idx_ref, h2_hbm, xe_ref, buf_ref, sem, *, ne, cap):
    b = pl.program_id(0)
    e = pl.program_id(1)
    s = buf_ref.shape[1]
    slot = b % 2
    base = (b * ne + e) * cap

    def batch_copy(batch, into):
        rows = pl.ds(pl.multiple_of(batch * s, s), s)
        return pltpu.make_async_copy(h2_hbm.at[rows, :], buf_ref.at[into], sem.at[into])

    @pl.when(jnp.logical_and(b == 0, e == 0))
    def _():
        batch_copy(0, 0).start()

    @pl.when(e == 0)
    def _():
        @pl.when(b + 1 < pl.num_programs(0))
        def _():
            batch_copy(b + 1, 1 - slot).start()

        batch_copy(b, slot).wait()

    h2_ref = buf_ref.at[slot]

    def body(i, _):
        r0 = pl.multiple_of(i * ROWS_PER_STEP, ROWS_PER_STEP)
        rows = [h2_ref[pl.ds(idx_ref[base + r0 + j], 1), :] for j in range(ROWS_PER_STEP)]
        xe_ref[0, pl.ds(r0, ROWS_PER_STEP), :] = jnp.concatenate(rows, axis=0).astype(bf16)
        return 0

    lax.fori_loop(0, cap // ROWS_PER_STEP, body, 0)


def _gather(idx_flat, h2, nb, s, ne, cap):
    d = h2.shape[1]
    return pl.pallas_call(
        functools.partial(_gather_kernel, ne=ne, cap=cap),
        grid_spec=pltpu.PrefetchScalarGridSpec(
            num_scalar_prefetch=1,
            grid=(nb, ne),
            in_specs=[pl.BlockSpec(memory_space=pl.ANY)],
            out_specs=pl.BlockSpec((1, cap, d), lambda bi, e, idx: (e, bi, 0)),
            scratch_shapes=[pltpu.VMEM((2, s, d), f32), pltpu.SemaphoreType.DMA((2,))],
        ),
        out_shape=jax.ShapeDtypeStruct((ne, nb * cap, d), bf16),
        compiler_params=_params(("arbitrary", "arbitrary")),
        name="gather",
    )(idx_flat, h2)


def _expert_kernel(x_ref, wg_ref, wu_ref, wd_ref, gate_ref, y_ref, acc_ref, *, sub, nf):
    f = pl.program_id(2)
    tmo, d = acc_ref.shape

    def partial_out(rows):
        xs = x_ref[0, rows, :]
        a = _dot(xs, wg_ref[0].astype(bf16))
        g = _dot(xs, wu_ref[0].astype(bf16))
        hm = (a * jax.nn.sigmoid(a) * g).astype(bf16)
        return _dot(hm, wd_ref[0].astype(bf16))

    def store_gated(rows, y):
        for c in range(rows.start // GROUP, rows.stop // GROUP):
            chunk = slice(c * GROUP, (c + 1) * GROUP)
            g_col = jnp.broadcast_to(gate_ref[0, :, chunk], (GROUP, GROUP)).T
            y_c = y[chunk.start - rows.start:chunk.stop - rows.start, :]
            y_ref[0, chunk, :] = (y_c * jnp.tile(g_col, (1, d // GROUP))).astype(bf16)

    def run(first, last):
        for r0 in range(0, tmo, sub):
            rows = slice(r0, r0 + sub)
            y = partial_out(rows)
            if not first:
                y = acc_ref[rows, :] + y
            if last:
                store_gated(rows, y)
            else:
                acc_ref[rows, :] = y

    if nf == 1:
        run(True, True)
    else:
        pl.when(f == 0)(lambda: run(True, False))
        pl.when(f == nf - 1)(lambda: run(False, True))
        if nf > 2:
            pl.when(jnp.logical_and(f > 0, f < nf - 1))(lambda: run(False, False))


def _experts(xe, wg, wu, wd, gate_row, tmo, tf, sub):
    ne, m, d = xe.shape
    ff = wg.shape[2]
    return pl.pallas_call(
        functools.partial(_expert_kernel, sub=sub, nf=ff // tf),
        grid=(ne, m // tmo, ff // tf),
        in_specs=[
            pl.BlockSpec((1, tmo, d), lambda e, mi, fi: (e, mi, 0)),
            pl.BlockSpec((1, d, tf), lambda e, mi, fi: (e, 0, fi)),
            pl.BlockSpec((1, d, tf), lambda e, mi, fi: (e, 0, fi)),
            pl.BlockSpec((1, tf, d), lambda e, mi, fi: (e, fi, 0)),
            pl.BlockSpec((1, 1, tmo), lambda e, mi, fi: (e, 0, mi)),
        ],
        out_specs=pl.BlockSpec((1, tmo, d), lambda e, mi, fi: (e, mi, 0)),
        out_shape=jax.ShapeDtypeStruct((ne, m, d), bf16),
        scratch_shapes=[pltpu.VMEM((tmo, d), f32)],
        compiler_params=_params(("parallel", "parallel", "arbitrary")),
        name="experts",
    )(xe, wg, wu, wd, gate_row)


NORM_ROWS = 256


def _combine_kernel(idx_ref, ye_ref, x1_hbm, g_ref, out_ref, x1_ref, sem, *, ne, cap, do_norm):
    b = pl.program_id(0)
    e = pl.program_id(1)
    s = out_ref.shape[0]
    base = (b * ne + e) * cap
    x1_copy = pltpu.make_async_copy(x1_hbm.at[pl.ds(pl.multiple_of(b * s, s), s), :], x1_ref, sem)

    @pl.when(e == 0)
    def _():
        x1_copy.start()
        out_ref[...] = jnp.zeros_like(out_ref)

    def body(i, _):
        r0 = pl.multiple_of(i * ROWS_PER_STEP, ROWS_PER_STEP)
        blk = ye_ref[0, pl.ds(r0, ROWS_PER_STEP), :].astype(f32)
        toks = [idx_ref[base + r0 + j] for j in range(ROWS_PER_STEP)]
        rows = [out_ref[pl.ds(tok, 1), :] for tok in toks]
        for j in range(ROWS_PER_STEP):
            out_ref[pl.ds(toks[j], 1), :] = rows[j] + blk[j:j + 1, :]
        return 0

    lax.fori_loop(0, cap // ROWS_PER_STEP, body, 0)

    @pl.when(e == ne - 1)
    def _():
        x1_copy.wait()

        def norm_rows(c, _):
            rows = pl.ds(pl.multiple_of(c * NORM_ROWS, NORM_ROWS), NORM_ROWS)
            y = x1_ref[rows, :] + out_ref[rows, :]
            if do_norm:
                ms = jnp.mean(y * y, axis=-1, keepdims=True)
                y = y * lax.rsqrt(ms + EPS) * g_ref[...]
            out_ref[rows, :] = y
            return 0

        lax.fori_loop(0, s // NORM_ROWS, norm_rows, 0)


def _combine(idx_flat, ye, x1, g, nb, s, ne, cap, do_norm):
    d = ye.shape[2]
    return pl.pallas_call(
        functools.partial(_combine_kernel, ne=ne, cap=cap, do_norm=do_norm),
        grid_spec=pltpu.PrefetchScalarGridSpec(
            num_scalar_prefetch=1,
            grid=(nb, ne),
            in_specs=[
                pl.BlockSpec((1, cap, d), lambda bi, e, idx: (e, bi, 0)),
                pl.BlockSpec(memory_space=pl.ANY),
                pl.BlockSpec((1, d), lambda bi, e, idx: (0, 0)),
            ],
            out_specs=pl.BlockSpec((s, d), lambda bi, e, idx: (bi, 0)),
            scratch_shapes=[pltpu.VMEM((s, d), f32), pltpu.SemaphoreType.DMA],
        ),
        out_shape=jax.ShapeDtypeStruct((nb * s, d), f32),
        compiler_params=_params(("arbitrary", "arbitrary")),
        name="combine",
    )(idx_flat, ye, x1, g)


def _channel_dft_table():
    k = jnp.arange(GROUP, dtype=jnp.int32)
    ang = ((k[:, None] * k[None, :]) % GROUP).astype(f32) * (2.0 * math.pi / GROUP)
    return jnp.concatenate([jnp.cos(ang), jnp.sin(ang)], axis=1).astype(bf16)


def _sequence_dft_tables(s, ts, scale):
    half = s // 2
    inner = 64
    k = jnp.arange(half, dtype=jnp.int32)[None, :]
    n1 = jnp.arange(half // inner, dtype=jnp.int32)[:, None] * inner
    n2 = jnp.arange(inner, dtype=jnp.int32)[:, None]
    w = 2.0 * math.pi / s
    a1 = ((n1 * k) % s).astype(f32) * w
    a2 = ((n2 * k) % s).astype(f32) * w
    c1, s1 = jnp.cos(a1)[:, None, :], jnp.sin(a1)[:, None, :]
    c2, s2 = jnp.cos(a2)[None, :, :] * scale, jnp.sin(a2)[None, :, :] * scale
    cp = (c1 * c2 - s1 * s2).reshape(half, half).astype(bf16)
    sm = (-(s1 * c2 + c1 * s2)).reshape(half, half).astype(bf16)
    rows = jnp.arange(ts, dtype=jnp.int32)[:, None]
    cols = jnp.arange(ts, dtype=jnp.int32)[None, :]
    flip = (cols == ts - rows).astype(bf16)
    kk = jnp.arange(half, dtype=jnp.int32)[None, :]
    alt = jnp.where(jnp.arange(8)[:, None] == 0, (1 - 2 * (kk & 1)).astype(f32) * scale, 0.0).astype(bf16)
    r = jnp.arange(GROUP, dtype=jnp.int32)
    j1 = ((r[None, :] == GROUP - r[:, None]) & (r[:, None] >= 1)).astype(bf16)
    e0 = ((r[:, None] == 0) & (r[None, :] == 0)).astype(bf16)
    return cp, sm, flip, alt, j1, e0


def _pick(n, pref):
    return pref if n % pref == 0 else n


def kernel(x, norm1_g, w_in, sgu_ln_g, sgu_ln_b, w_spatial, b_spatial, w_fourier_out, w_sgu_out, w_out,
           norm2_g, w_router, w_gate_e, w_up_e, w_down_e, final_g):
    nb, s, d = x.shape
    depth = norm1_g.shape[0]
    ne = w_router.shape[2]
    ff = w_gate_e.shape[3]
    cap = CAPACITY_FACTOR * s // ne
    t = nb * s
    assert s % NORM_ROWS == 0 and s & (s - 1) == 0 and cap % ROWS_PER_STEP == 0 and d % GROUP == 0
    assert (nb * cap) % GROUP == 0

    tm_front = _pick(s, 1024)
    ts_dft = _pick(s // 2, 512)
    tm_post = _pick(s // 2, 1024)
    dft_scale = 1.0 / math.sqrt(s * GROUP)
    cs_tab = _channel_dft_table()
    seq_tables = _sequence_dft_tables(s, ts_dft, dft_scale)
    tmo = _pick(nb * cap, 2048)
    tf = _pick(ff, 512)
    sub = _pick(tmo, 1024)

    x2 = x.reshape(t, d)
    for l in range(depth):
        a, b, ysg, sgf = _front(
            x2, norm1_g[l][None, :], w_in[l].astype(bf16), cs_tab,
            sgu_ln_g[l][None, :], sgu_ln_b[l][None, :], w_spatial[l].astype(bf16),
            b_spatial[l].T, w_sgu_out[l].astype(bf16), tm_front)
        fm_first, fm_second = _seqdft(seq_tables, a, b, nb, s, ts_dft, dft_scale)
        wr = jnp.pad(w_router[l], ((0, 0), (0, ROUTER_LANES - ne)))
        wr_hi = wr.astype(bf16)
        wr_lo = (wr - wr_hi.astype(f32)).astype(bf16)
        wr_split = jnp.concatenate([jnp.concatenate([wr_hi, wr_lo], axis=1),
                                    jnp.concatenate([wr_hi, jnp.zeros_like(wr_hi)], axis=1)], axis=0)
        x1, h2, aff = _post(fm_first, fm_second, sgf, ysg, x2, w_fourier_out[l].astype(bf16),
                            w_out[l].astype(bf16), norm2_g[l][None, :], wr_split, ne, s, tm_post)
        idx, gate = _select(aff, nb, s, cap)
        idx_flat = idx.reshape(-1)
        xe = _gather(idx_flat, h2, nb, s, ne, cap)
        gate_row = jnp.transpose(gate, (1, 0, 2)).reshape(ne, 1, nb * cap)
        ye = _experts(xe, w_gate_e[l], w_up_e[l], w_down_e[l], gate_row, tmo, tf, sub)
        x2 = _combine(idx_flat, ye, x1, final_g[None, :], nb, s, ne, cap, l == depth - 1)
    return x2.reshape(nb, s, d)
```

```python
import functools
import math

import jax
import jax.numpy as jnp
from jax import lax
from jax.experimental import pallas as pl
from jax.experimental.pallas import tpu as pltpu

EPS = 1e-6
GROUP = 128
N_GROUPS = 4
F_WIDTH = N_GROUPS * GROUP
S_WIDTH = N_GROUPS * GROUP
CAPACITY_FACTOR = 2
VMEM_LIMIT_V7X = 56 * 1024 * 1024

f32 = jnp.float32
bf16 = jnp.bfloat16


def _dot(a, b):
    return jnp.dot(a, b, preferred_element_type=f32)


def _params(sem, vmem=VMEM_LIMIT_V7X):
    return pltpu.CompilerParams(dimension_semantics=sem, vmem_limit_bytes=vmem)


FRONT_ROWS = 1024


def _front_kernel(x_ref, g1_ref, win_ref, cs_ref, lng_ref, lnb_ref, ws_ref, bs_ref, wso_ref,
                  a_ref, b_ref, ysg_ref, sgf_ref, mix_ref):
    tm, d = x_ref.shape
    c0 = F_WIDTH + 2 * S_WIDTH
    chunk = min(FRONT_ROWS, tm)
    for r0 in range(0, tm, chunk):
        rows = slice(r0, r0 + chunk)
        x = x_ref[rows, :]
        ms = jnp.mean(x * x, axis=-1, keepdims=True)
        h = (x * lax.rsqrt(ms + EPS) * g1_ref[...]).astype(bf16)

        def proj(lo, hi):
            return _dot(h, win_ref[:, lo:hi])

        zf = proj(0, F_WIDTH)
        for g in range(N_GROUPS):
            cols = slice(g * GROUP, (g + 1) * GROUP)
            ab = _dot(zf[:, cols].astype(bf16), cs_ref[...])
            a_ref[rows, cols] = ab[:, :GROUP].astype(bf16)
            b_ref[rows, cols] = ab[:, GROUP:].astype(bf16)

        u_pre = proj(F_WIDTH, F_WIDTH + S_WIDTH)
        v_pre = proj(F_WIDTH + S_WIDTH, F_WIDTH + 2 * S_WIDTH)
        sgf_ref[rows, :] = jax.nn.sigmoid(proj(c0, c0 + d)).astype(bf16)
        gs_pre = proj(c0 + d, c0 + 2 * d)
        u = jax.nn.gelu(u_pre)
        v = jax.nn.gelu(v_pre)
        for g in range(N_GROUPS):
            cols = slice(g * GROUP, (g + 1) * GROUP)
            vg = v[:, cols]
            mu = jnp.mean(vg, axis=-1, keepdims=True)
            dv = vg - mu
            var = jnp.mean(dv * dv, axis=-1, keepdims=True)
            vnb = (dv * lax.rsqrt(var + EPS) * lng_ref[:, cols] + lnb_ref[:, cols]).astype(bf16)
            for c in range(chunk // GROUP):
                m = _dot(ws_ref[g], vnb[c * GROUP:(c + 1) * GROUP, :]) + bs_ref[:, g:g + 1]
                mix_ref[r0 + c * GROUP:r0 + (c + 1) * GROUP, cols] = m
        sgu = (u * mix_ref[rows, :]).astype(bf16)
        ys = _dot(sgu, wso_ref[...])
        ysg_ref[rows, :] = (jax.nn.sigmoid(gs_pre) * ys).astype(bf16)


def _front(x2, g1, win_b, cs_b, lng, lnb, ws_b, bs_t, wso_b, tm):
    t, d = x2.shape
    kin = win_b.shape[1]
    const = lambda *shape: pl.BlockSpec(shape, lambda i: (0,) * len(shape))
    return pl.pallas_call(
        _front_kernel,
        grid=(t // tm,),
        in_specs=[
            pl.BlockSpec((tm, d), lambda i: (i, 0)),
            const(1, d),
            const(d, kin),
            const(GROUP, 2 * GROUP),
            const(1, S_WIDTH),
            const(1, S_WIDTH),
            const(N_GROUPS, GROUP, GROUP),
            const(GROUP, N_GROUPS),
            const(S_WIDTH, d),
        ],
        out_specs=[
            pl.BlockSpec((tm, F_WIDTH), lambda i: (i, 0)),
            pl.BlockSpec((tm, F_WIDTH), lambda i: (i, 0)),
            pl.BlockSpec((tm, d), lambda i: (i, 0)),
            pl.BlockSpec((tm, d), lambda i: (i, 0)),
        ],
        out_shape=[
            jax.ShapeDtypeStruct((t, F_WIDTH), bf16),
            jax.ShapeDtypeStruct((t, F_WIDTH), bf16),
            jax.ShapeDtypeStruct((t, d), bf16),
            jax.ShapeDtypeStruct((t, d), bf16),
        ],
        scratch_shapes=[pltpu.VMEM((tm, S_WIDTH), f32)],
        compiler_params=_params(("parallel",)),
        name="front",
    )(x2, g1, win_b, cs_b, lng, lnb, ws_b, bs_t, wso_b)


def _seqdft_kernel(cp_ref, sm_ref, flip_ref, altrow_ref, j1_ref, e0_ref, a_ref, b_ref, fmd_ref, fmm_ref,
                   ap_ref, bm_ref, edge_ref, *, scale):
    i = pl.program_id(1)
    ts = fmd_ref.shape[0]
    s = a_ref.shape[0]
    half = s // 2
    a_mid = scale * a_ref[half:half + 1, :].astype(f32)

    @pl.when(i == 0)
    def _fold_inputs():
        for j in range(half // GROUP):
            lo = slice(j * GROUP, (j + 1) * GROUP)
            mir = slice(s - (j + 1) * GROUP, s - j * GROUP)
            a_m = _dot(j1_ref[...], a_ref[mir, :])
            b_m = _dot(j1_ref[...], b_ref[mir, :])
            if j > 0:
                nxt = slice(s - j * GROUP, s - (j - 1) * GROUP)
                a_m = a_m + _dot(e0_ref[...], a_ref[nxt, :])
                b_m = b_m + _dot(e0_ref[...], b_ref[nxt, :])
            ap_ref[lo, :] = (a_ref[lo, :].astype(f32) + a_m).astype(bf16)
            bm_ref[lo, :] = (b_ref[lo, :].astype(f32) - b_m).astype(bf16)
        first = lax.broadcasted_iota(jnp.int32, (8, 1), 0) == 0
        edge_ref[...] = _dot(altrow_ref[...], ap_ref[...]) + jnp.where(first, a_mid, 0.0)

    p = _dot(cp_ref[...], ap_ref[...])
    q = _dot(sm_ref[...], bm_ref[...])
    row = lax.broadcasted_iota(jnp.int32, (ts, 1), 0)
    alt = (1 - 2 * (row & 1)).astype(f32) * a_mid
    fmd_ref[...] = (p + q + alt).astype(bf16)
    g = p - q + alt
    flipped = _dot(flip_ref[...], g.astype(bf16))
    fmm_ref[...] = jnp.where(row == 0, edge_ref[0:1, :], flipped).astype(bf16)
    edge_ref[...] = g[0:8, :]


def _seqdft(tables, a, b, nb, s, ts, scale):
    half = s // 2
    nh = half // ts
    cp, sm, flip, altrow, j1, e0 = tables
    const = lambda *shape: pl.BlockSpec(shape, lambda bi, i: (0,) * len(shape))
    table_tile = pl.BlockSpec((ts, half), lambda bi, i: (nh - 1 - i, 0))
    return pl.pallas_call(
        functools.partial(_seqdft_kernel, scale=scale),
        grid=(nb, nh),
        in_specs=[
            table_tile,
            table_tile,
            const(ts, ts),
            const(8, half),
            const(GROUP, GROUP),
            const(GROUP, GROUP),
            pl.BlockSpec((s, F_WIDTH), lambda bi, i: (bi, 0)),
            pl.BlockSpec((s, F_WIDTH), lambda bi, i: (bi, 0)),
        ],
        out_specs=[
            pl.BlockSpec((ts, F_WIDTH), lambda bi, i: (bi * nh + nh - 1 - i, 0)),
            pl.BlockSpec((ts, F_WIDTH), lambda bi, i: (bi * nh + i, 0)),
        ],
        out_shape=[
            jax.ShapeDtypeStruct((nb * half, F_WIDTH), bf16),
            jax.ShapeDtypeStruct((nb * half, F_WIDTH), bf16),
        ],
        scratch_shapes=[
            pltpu.VMEM((half, F_WIDTH), bf16),
            pltpu.VMEM((half, F_WIDTH), bf16),
            pltpu.VMEM((8, F_WIDTH), f32),
        ],
        compiler_params=_params(("parallel", "arbitrary")),
        name="seqdft",
    )(cp, sm, flip, altrow, j1, e0, a, b)


ROUTER_LANES = 128
POST_ROWS = 256


def _post_kernel(fmd_ref, fmm_ref, sgf_ref, ysg_ref, x_ref, wfo_ref, wout_ref, g2_ref, wr_ref,
                 x1_ref, h2_ref, aff_ref, *, tiles_per_half):
    ne = aff_ref.shape[0]
    tm = x_ref.shape[0]
    in_first_half = (pl.program_id(0) % (2 * tiles_per_half)) < tiles_per_half
    chunk = min(POST_ROWS, tm)
    for r0 in range(0, tm, chunk):
        rows = slice(r0, r0 + chunk)
        fm = jnp.where(in_first_half, fmd_ref[rows, :], fmm_ref[rows, :])
        yf = _dot(fm, wfo_ref[...])
        merged = sgf_ref[rows, :].astype(f32) * yf + ysg_ref[rows, :].astype(f32)
        x1 = x_ref[rows, :] + _dot(merged.astype(bf16), wout_ref[...])
        x1_ref[rows, :] = x1
        ms = jnp.mean(x1 * x1, axis=-1, keepdims=True)
        h2 = x1 * lax.rsqrt(ms + EPS) * g2_ref[...]
        h2_ref[rows, :] = h2
        h_hi = h2.astype(bf16)
        h_lo = (h2 - h_hi.astype(f32)).astype(bf16)
        both = _dot(jnp.concatenate([h_hi, h_lo], axis=1), wr_ref[...])
        logits = both[:, :ROUTER_LANES] + both[:, ROUTER_LANES:]
        logits = logits.T[:ne, :]
        mx = jnp.max(logits, axis=0, keepdims=True)
        ex = jnp.exp(logits - mx)
        aff_ref[:, rows] = ex / jnp.sum(ex, axis=0, keepdims=True)


def _post(fm_first, fm_second, sgf, ysg, x2, wfo_b, wout_b, g2, wr_split, ne, s, tm):
    t, d = x2.shape
    th = s // 2 // tm
    const = lambda *shape: pl.BlockSpec(shape, lambda i: (0,) * len(shape))
    tile = lambda w: pl.BlockSpec((tm, w), lambda i: (i, 0))
    first = pl.BlockSpec((tm, F_WIDTH), lambda i: (i // (2 * th) * th + jnp.minimum(i % (2 * th), th - 1), 0))
    second = pl.BlockSpec((tm, F_WIDTH), lambda i: (i // (2 * th) * th + jnp.maximum(i % (2 * th) - th, 0), 0))
    return pl.pallas_call(
        functools.partial(_post_kernel, tiles_per_half=th),
        grid=(t // tm,),
        in_specs=[
            first, second, tile(d), tile(d), tile(d),
            const(F_WIDTH, d),
            const(d, d),
            const(1, d),
            const(2 * d, 2 * ROUTER_LANES),
        ],
        out_specs=[tile(d), tile(d), pl.BlockSpec((ne, tm), lambda i: (0, i))],
        out_shape=[
            jax.ShapeDtypeStruct((t, d), f32),
            jax.ShapeDtypeStruct((t, d), f32),
            jax.ShapeDtypeStruct((ne, t), f32),
        ],
        compiler_params=_params(("parallel",)),
        name="post",
    )(fm_first, fm_second, sgf, ysg, x2, wfo_b, wout_b, g2, wr_split)


def _select_kernel(aff_ref, idx_ref, gate_ref, *, cap):
    ne, s = aff_ref.shape
    nbits = s.bit_length() - 1
    aff = aff_ref[...]
    lane = lax.broadcasted_iota(jnp.int32, (ne, s), 1)

    def bit_step(i, thr):
        cand = thr | jnp.left_shift(jnp.int32(1), 30 - i)
        cnt = jnp.sum((aff >= pltpu.bitcast(cand, f32)).astype(jnp.int32), axis=1, keepdims=True)
        return jnp.where(cnt >= cap, cand, thr)

    thr = lax.fori_loop(0, 31, bit_step, jnp.zeros((ne, 1), jnp.int32))
    gt = aff >= pltpu.bitcast(thr + 1, f32)
    eq = jnp.logical_and(aff >= pltpu.bitcast(thr, f32), jnp.logical_not(gt))
    need = cap - jnp.sum(gt.astype(jnp.int32), axis=1, keepdims=True)

    def cumsum_tokens(v):
        k = 1
        while k < s:
            v = v + jnp.where(lane >= k, pltpu.roll(v, k, axis=1), 0)
            k *= 2
        return v

    sel = jnp.logical_or(gt, jnp.logical_and(eq, cumsum_tokens(eq.astype(jnp.int32)) <= need))
    slot = cumsum_tokens(sel.astype(jnp.int32)) - 1
    word = jnp.where(sel, lane | jnp.left_shift(lane - slot, nbits) | (1 << (2 * nbits)), 0)
    gate = aff
    for k in range(nbits):
        step = 1 << k
        inc_word = pltpu.roll(word, s - step, axis=1)
        inc_gate = pltpu.roll(gate, s - step, axis=1)
        take = (jnp.right_shift(inc_word, nbits + k) & 1) == 1
        leave = (jnp.right_shift(word, nbits + k) & 1) == 1
        word = jnp.where(take, inc_word, jnp.where(leave, 0, word))
        gate = jnp.where(take, inc_gate, gate)
    idx_ref[0] = word[:, :cap] & (s - 1)
    gate_ref[0] = gate[:, :cap]


def _select(aff_t, nb, s, cap):
    ne = aff_t.shape[0]
    return pl.pallas_call(
        functools.partial(_select_kernel, cap=cap),
        grid=(nb,),
        in_specs=[pl.BlockSpec((ne, s), lambda bi: (0, bi))],
        out_specs=[
            pl.BlockSpec((1, ne, cap), lambda bi: (bi, 0, 0)),
            pl.BlockSpec((1, ne, cap), lambda bi: (bi, 0, 0)),
        ],
        out_shape=[
            jax.ShapeDtypeStruct((nb, ne, cap), jnp.int32),
            jax.ShapeDtypeStruct((nb, ne, cap), f32),
        ],
        compiler_params=_params(("parallel",)),
        name="select",
    )(aff_t)


GATHER_ROWS = 32
SCATTER_ROWS = 16


def _gather_kernel(idx_ref, h2_hbm, xe_ref, buf_ref, sem, *, ne, cap):
    b = pl.program_id(0)
    e = pl.program_id(1)
    s = buf_ref.shape[1]
    slot = b % 2
    base = (b * ne + e) * cap

    def batch_copy(batch, into):
        rows = pl.ds(pl.multiple_of(batch * s, s), s)
        return pltpu.make_async_copy(h2_hbm.at[rows, :], buf_ref.at[into], sem.at[into])

    @pl.when(jnp.logical_and(b == 0, e == 0))
    def _():
        batch_copy(0, 0).start()

    @pl.when(e == 0)
    def _():
        @pl.when(b + 1 < pl.num_programs(0))
        def _():
            batch_copy(b + 1, 1 - slot).start()

        batch_copy(b, slot).wait()

    h2_ref = buf_ref.at[slot]

    def body(i, _):
        r0 = pl.multiple_of(i * GATHER_ROWS, GATHER_ROWS)
        rows = [h2_ref[pl.ds(idx_ref[base + r0 + j], 1), :] for j in range(GATHER_ROWS)]
        xe_ref[0, pl.ds(r0, GATHER_ROWS), :] = jnp.concatenate(rows, axis=0).astype(bf16)
        return 0

    lax.fori_loop(0, cap // GATHER_ROWS, body, 0)


def _gather(idx_flat, h2, nb, s, ne, cap):
    d = h2.shape[1]
    return pl.pallas_call(
        functools.partial(_gather_kernel, ne=ne, cap=cap),
        grid_spec=pltpu.PrefetchScalarGridSpec(
            num_scalar_prefetch=1,
            grid=(nb, ne),
            in_specs=[pl.BlockSpec(memory_space=pl.ANY)],
            out_specs=pl.BlockSpec((1, cap, d), lambda bi, e, idx: (e, bi, 0)),
            scratch_shapes=[pltpu.VMEM((2, s, d), f32), pltpu.SemaphoreType.DMA((2,))],
        ),
        out_shape=jax.ShapeDtypeStruct((ne, nb * cap, d), bf16),
        compiler_params=_params(("arbitrary", "arbitrary")),
        name="gather",
    )(idx_flat, h2)


def _expert_kernel(x_ref, wg_ref, wu_ref, wd_ref, gate_ref, y_ref, acc_ref, *, sub, nf):
    f = pl.program_id(2)
    tmo, d = acc_ref.shape

    def partial_out(rows):
        xs = x_ref[0, rows, :]
        a = _dot(xs, wg_ref[0].astype(bf16))
        g = _dot(xs, wu_ref[0].astype(bf16))
        hm = (a * jax.nn.sigmoid(a) * g).astype(bf16)
        return _dot(hm, wd_ref[0].astype(bf16))

    def store_gated(rows, y):
        for c in range(rows.start // GROUP, rows.stop // GROUP):
            chunk = slice(c * GROUP, (c + 1) * GROUP)
            g_col = jnp.broadcast_to(gate_ref[0, :, chunk], (GROUP, GROUP)).T
            y_c = y[chunk.start - rows.start:chunk.stop - rows.start, :]
            y_ref[0, chunk, :] = (y_c * jnp.tile(g_col, (1, d // GROUP))).astype(bf16)

    def run(first, last):
        for r0 in range(0, tmo, sub):
            rows = slice(r0, r0 + sub)
            y = partial_out(rows)
            if not first:
                y = acc_ref[rows, :] + y
            if last:
                store_gated(rows, y)
            else:
                acc_ref[rows, :] = y

    if nf == 1:
        run(True, True)
    else:
        pl.when(f == 0)(lambda: run(True, False))
        pl.when(f == nf - 1)(lambda: run(False, True))
        if nf > 2:
            pl.when(jnp.logical_and(f > 0, f < nf - 1))(lambda: run(False, False))


def _experts(xe, wg, wu, wd, gate_row, tmo, tf, sub):
    ne, m, d = xe.shape
    ff = wg.shape[2]
    return pl.pallas_call(
        functools.partial(_expert_kernel, sub=sub, nf=ff // tf),
        grid=(ne, m // tmo, ff // tf),
        in_specs=[
            pl.BlockSpec((1, tmo, d), lambda e, mi, fi: (e, mi, 0)),
            pl.BlockSpec((1, d, tf), lambda e, mi, fi: (e, 0, fi)),
            pl.BlockSpec((1, d, tf), lambda e, mi, fi: (e, 0, fi)),
            pl.BlockSpec((1, tf, d), lambda e, mi, fi: (e, fi, 0)),
            pl.BlockSpec((1, 1, tmo), lambda e, mi, fi: (e, 0, mi)),
        ],
        out_specs=pl.BlockSpec((1, tmo, d), lambda e, mi, fi: (e, mi, 0)),
        out_shape=jax.ShapeDtypeStruct((ne, m, d), bf16),
        scratch_shapes=[pltpu.VMEM((tmo, d), f32)],
        compiler_params=_params(("parallel", "parallel", "arbitrary")),
        name="experts",
    )(xe, wg, wu, wd, gate_row)


NORM_ROWS = 256


def _combine_kernel(idx_ref, ye_ref, x1_hbm, g_ref, out_ref, x1_ref, sem, *, ne, cap, do_norm):
    b = pl.program_id(0)
    e = pl.program_id(1)
    s = out_ref.shape[0]
    base = (b * ne + e) * cap
    x1_copy = pltpu.make_async_copy(x1_hbm.at[pl.ds(pl.multiple_of(b * s, s), s), :], x1_ref, sem)

    @pl.when(e == 0)
    def _():
        x1_copy.start()
        out_ref[...] = jnp.zeros_like(out_ref)

    def body(i, _):
        r0 = pl.multiple_of(i * SCATTER_ROWS, SCATTER_ROWS)
        blk = ye_ref[0, pl.ds(r0, SCATTER_ROWS), :].astype(f32)
        toks = [idx_ref[base + r0 + j] for j in range(SCATTER_ROWS)]
        rows = [out_ref[pl.ds(tok, 1), :] for tok in toks]
        for j in range(SCATTER_ROWS):
            out_ref[pl.ds(toks[j], 1), :] = rows[j] + blk[j:j + 1, :]
        return 0

    lax.fori_loop(0, cap // SCATTER_ROWS, body, 0)

    @pl.when(e == ne - 1)
    def _():
        x1_copy.wait()

        def norm_rows(c, _):
            rows = pl.ds(pl.multiple_of(c * NORM_ROWS, NORM_ROWS), NORM_ROWS)
            y = x1_ref[rows, :] + out_ref[rows, :]
            if do_norm:
                ms = jnp.mean(y * y, axis=-1, keepdims=True)
                y = y * lax.rsqrt(ms + EPS) * g_ref[...]
            out_ref[rows, :] = y
            return 0

        lax.fori_loop(0, s // NORM_ROWS, norm_rows, 0)


def _combine(idx_flat, ye, x1, g, nb, s, ne, cap, do_norm):
    d = ye.shape[2]
    return pl.pallas_call(
        functools.partial(_combine_kernel, ne=ne, cap=cap, do_norm=do_norm),
        grid_spec=pltpu.PrefetchScalarGridSpec(
            num_scalar_prefetch=1,
            grid=(nb, ne),
            in_specs=[
                pl.BlockSpec((1, cap, d), lambda bi, e, idx: (e, bi, 0)),
                pl.BlockSpec(memory_space=pl.ANY),
                pl.BlockSpec((1, d), lambda bi, e, idx: (0, 0)),
            ],
            out_specs=pl.BlockSpec((s, d), lambda bi, e, idx: (bi, 0)),
            scratch_shapes=[pltpu.VMEM((s, d), f32), pltpu.SemaphoreType.DMA],
        ),
        out_shape=jax.ShapeDtypeStruct((nb * s, d), f32),
        compiler_params=_params(("arbitrary", "arbitrary")),
        name="combine",
    )(idx_flat, ye, x1, g)


def _channel_dft_table():
    k = jnp.arange(GROUP, dtype=jnp.int32)
    ang = ((k[:, None] * k[None, :]) % GROUP).astype(f32) * (2.0 * math.pi / GROUP)
    return jnp.concatenate([jnp.cos(ang), jnp.sin(ang)], axis=1).astype(bf16)


def _sequence_dft_tables(s, ts, scale):
    half = s // 2
    inner = 64
    k = jnp.arange(half, dtype=jnp.int32)[None, :]
    n1 = jnp.arange(half // inner, dtype=jnp.int32)[:, None] * inner
    n2 = jnp.arange(inner, dtype=jnp.int32)[:, None]
    w = 2.0 * math.pi / s
    a1 = ((n1 * k) % s).astype(f32) * w
    a2 = ((n2 * k) % s).astype(f32) * w
    c1, s1 = jnp.cos(a1)[:, None, :], jnp.sin(a1)[:, None, :]
    c2, s2 = jnp.cos(a2)[None, :, :] * scale, jnp.sin(a2)[None, :, :] * scale
    cp = (c1 * c2 - s1 * s2).reshape(half, half).astype(bf16)
    sm = (-(s1 * c2 + c1 * s2)).reshape(half, half).astype(bf16)
    rows = jnp.arange(ts, dtype=jnp.int32)[:, None]
    cols = jnp.arange(ts, dtype=jnp.int32)[None, :]
    flip = (cols == ts - rows).astype(bf16)
    kk = jnp.arange(half, dtype=jnp.int32)[None, :]
    alt = jnp.where(jnp.arange(8)[:, None] == 0, (1 - 2 * (kk & 1)).astype(f32) * scale, 0.0).astype(bf16)
    r = jnp.arange(GROUP, dtype=jnp.int32)
    j1 = ((r[None, :] == GROUP - r[:, None]) & (r[:, None] >= 1)).astype(bf16)
    e0 = ((r[:, None] == 0) & (r[None, :] == 0)).astype(bf16)
    return cp, sm, flip, alt, j1, e0


def _pick(n, pref):
    return pref if n % pref == 0 else n


def kernel(x, norm1_g, w_in, sgu_ln_g, sgu_ln_b, w_spatial, b_spatial, w_fourier_out, w_sgu_out, w_out,
           norm2_g, w_router, w_gate_e, w_up_e, w_down_e, final_g):
    nb, s, d = x.shape
    depth = norm1_g.shape[0]
    ne = w_router.shape[2]
    ff = w_gate_e.shape[3]
    cap = CAPACITY_FACTOR * s // ne
    t = nb * s
    assert s % NORM_ROWS == 0 and s & (s - 1) == 0 and cap % GATHER_ROWS == 0 and d % GROUP == 0
    assert (nb * cap) % GROUP == 0

    tm_front = _pick(s, 1024)
    ts_dft = _pick(s // 2, 512)
    tm_post = _pick(s // 2, 1024)
    dft_scale = 1.0 / math.sqrt(s * GROUP)
    cs_tab = _channel_dft_table()
    seq_tables = _sequence_dft_tables(s, ts_dft, dft_scale)
    tmo = _pick(nb * cap, 2048)
    tf = _pick(ff, 512)
    sub = _pick(tmo, 1024)

    x2 = x.reshape(t, d)
    for l in range(depth):
        a, b, ysg, sgf = _front(
            x2, norm1_g[l][None, :], w_in[l].astype(bf16), cs_tab,
            sgu_ln_g[l][None, :], sgu_ln_b[l][None, :], w_spatial[l].astype(bf16),
            b_spatial[l].T, w_sgu_out[l].astype(bf16), tm_front)
        fm_first, fm_second = _seqdft(seq_tables, a, b, nb, s, ts_dft, dft_scale)
        wr = jnp.pad(w_router[l], ((0, 0), (0, ROUTER_LANES - ne)))
        wr_hi = wr.astype(bf16)
        wr_lo = (wr - wr_hi.astype(f32)).astype(bf16)
        wr_split = jnp.concatenate([jnp.concatenate([wr_hi, wr_lo], axis=1),
                                    jnp.concatenate([wr_hi, jnp.zeros_like(wr_hi)], axis=1)], axis=0)
        x1, h2, aff = _post(fm_first, fm_second, sgf, ysg, x2, w_fourier_out[l].astype(bf16),
                            w_out[l].astype(bf16), norm2_g[l][None, :], wr_split, ne, s, tm_post)
        idx, gate = _select(aff, nb, s, cap)
        idx_flat = idx.reshape(-1)
        xe = _gather(idx_flat, h2, nb, s, ne, cap)
        gate_row = jnp.transpose(gate, (1, 0, 2)).reshape(ne, 1, nb * cap)
        ye = _experts(xe, w_gate_e[l], w_up_e[l], w_down_e[l], gate_row, tmo, tf, sub)
        x2 = _combine(idx_flat, ye, x1, final_g[None, :], nb, s, ne, cap, l == depth - 1)
    return x2.reshape(nb, s, d)
```

```python
import functools
import math

import jax
import jax.numpy as jnp
from jax import lax
from jax.experimental import pallas as pl
from jax.experimental.pallas import tpu as pltpu

EPS = 1e-6
GROUP = 128
N_GROUPS = 4
F_WIDTH = N_GROUPS * GROUP
S_WIDTH = N_GROUPS * GROUP
CAPACITY_FACTOR = 2
VMEM_LIMIT_V7X = 56 * 1024 * 1024

f32 = jnp.float32
bf16 = jnp.bfloat16


def _dot(a, b):
    return jnp.dot(a, b, preferred_element_type=f32)


def _params(sem, vmem=VMEM_LIMIT_V7X):
    return pltpu.CompilerParams(dimension_semantics=sem, vmem_limit_bytes=vmem)


FRONT_ROWS = 1024


def _front_kernel(x_ref, g1_ref, win_ref, cs_ref, lng_ref, lnb_ref, ws_ref, bs_ref, wso_ref,
                  a_ref, b_ref, ysg_ref, sgf_ref, mix_ref):
    tm, d = x_ref.shape
    c0 = F_WIDTH + 2 * S_WIDTH
    chunk = min(FRONT_ROWS, tm)
    for r0 in range(0, tm, chunk):
        rows = slice(r0, r0 + chunk)
        x = x_ref[rows, :]
        ms = jnp.mean(x * x, axis=-1, keepdims=True)
        h = (x * lax.rsqrt(ms + EPS) * g1_ref[...]).astype(bf16)

        def proj(lo, hi):
            return _dot(h, win_ref[:, lo:hi])

        zf = proj(0, F_WIDTH)
        for g in range(N_GROUPS):
            cols = slice(g * GROUP, (g + 1) * GROUP)
            ab = _dot(zf[:, cols].astype(bf16), cs_ref[...])
            a_ref[rows, cols] = ab[:, :GROUP].astype(bf16)
            b_ref[rows, cols] = ab[:, GROUP:].astype(bf16)

        u_pre = proj(F_WIDTH, F_WIDTH + S_WIDTH)
        v_pre = proj(F_WIDTH + S_WIDTH, F_WIDTH + 2 * S_WIDTH)
        sgf_ref[rows, :] = jax.nn.sigmoid(proj(c0, c0 + d)).astype(bf16)
        gs_pre = proj(c0 + d, c0 + 2 * d)
        u = jax.nn.gelu(u_pre)
        v = jax.nn.gelu(v_pre)
        for g in range(N_GROUPS):
            cols = slice(g * GROUP, (g + 1) * GROUP)
            vg = v[:, cols]
            mu = jnp.mean(vg, axis=-1, keepdims=True)
            dv = vg - mu
            var = jnp.mean(dv * dv, axis=-1, keepdims=True)
            vnb = (dv * lax.rsqrt(var + EPS) * lng_ref[:, cols] + lnb_ref[:, cols]).astype(bf16)
            for c in range(chunk // GROUP):
                m = _dot(ws_ref[g], vnb[c * GROUP:(c + 1) * GROUP, :]) + bs_ref[:, g:g + 1]
                mix_ref[r0 + c * GROUP:r0 + (c + 1) * GROUP, cols] = m
        sgu = (u * mix_ref[rows, :]).astype(bf16)
        ys = _dot(sgu, wso_ref[...])
        ysg_ref[rows, :] = (jax.nn.sigmoid(gs_pre) * ys).astype(bf16)


def _front(x2, g1, win_b, cs_b, lng, lnb, ws_b, bs_t, wso_b, tm):
    t, d = x2.shape
    kin = win_b.shape[1]
    const = lambda *shape: pl.BlockSpec(shape, lambda i: (0,) * len(shape))
    return pl.pallas_call(
        _front_kernel,
        grid=(t // tm,),
        in_specs=[
            pl.BlockSpec((tm, d), lambda i: (i, 0)),
            const(1, d),
            const(d, kin),
            const(GROUP, 2 * GROUP),
            const(1, S_WIDTH),
            const(1, S_WIDTH),
            const(N_GROUPS, GROUP, GROUP),
            const(GROUP, N_GROUPS),
            const(S_WIDTH, d),
        ],
        out_specs=[
            pl.BlockSpec((tm, F_WIDTH), lambda i: (i, 0)),
            pl.BlockSpec((tm, F_WIDTH), lambda i: (i, 0)),
            pl.BlockSpec((tm, d), lambda i: (i, 0)),
            pl.BlockSpec((tm, d), lambda i: (i, 0)),
        ],
        out_shape=[
            jax.ShapeDtypeStruct((t, F_WIDTH), bf16),
            jax.ShapeDtypeStruct((t, F_WIDTH), bf16),
            jax.ShapeDtypeStruct((t, d), bf16),
            jax.ShapeDtypeStruct((t, d), bf16),
        ],
        scratch_shapes=[pltpu.VMEM((tm, S_WIDTH), f32)],
        compiler_params=_params(("parallel",)),
        name="front",
    )(x2, g1, win_b, cs_b, lng, lnb, ws_b, bs_t, wso_b)


FOLD_ROWS = 256


def _seqdft_kernel(cp_ref, sm_ref, flip_ref, altrow_ref, j1_ref, a_ref, b_ref, fmd_ref, fmm_ref,
                   ap_ref, bm_ref, edge_ref, *, scale):
    i = pl.program_id(1)
    ts = fmd_ref.shape[0]
    s = a_ref.shape[0]
    half = s // 2
    a_mid = scale * a_ref[half:half + 1, :].astype(f32)

    @pl.when(i == 0)
    def _fold_inputs():
        fold = j1_ref.shape[0]
        first_row = lax.broadcasted_iota(jnp.int32, (fold, 1), 0) == 0
        for j in range(half // fold):
            lo = slice(j * fold, (j + 1) * fold)
            mir = slice(s - (j + 1) * fold, s - j * fold)
            a_m = _dot(j1_ref[...], a_ref[mir, :])
            b_m = _dot(j1_ref[...], b_ref[mir, :])
            if j > 0:
                a_m = jnp.where(first_row, a_ref[mir.stop:mir.stop + 1, :].astype(f32), a_m)
                b_m = jnp.where(first_row, b_ref[mir.stop:mir.stop + 1, :].astype(f32), b_m)
            ap_ref[lo, :] = (a_ref[lo, :].astype(f32) + a_m).astype(bf16)
            bm_ref[lo, :] = (b_ref[lo, :].astype(f32) - b_m).astype(bf16)
        first = lax.broadcasted_iota(jnp.int32, (8, 1), 0) == 0
        edge_ref[...] = _dot(altrow_ref[...], ap_ref[...]) + jnp.where(first, a_mid, 0.0)

    p = _dot(cp_ref[...], ap_ref[...])
    q = _dot(sm_ref[...], bm_ref[...])
    row = lax.broadcasted_iota(jnp.int32, (ts, 1), 0)
    alt = (1 - 2 * (row & 1)).astype(f32) * a_mid
    fmd_ref[...] = (p + q + alt).astype(bf16)
    g = p - q + alt
    flipped = _dot(flip_ref[...], g.astype(bf16))
    fmm_ref[...] = jnp.where(row == 0, edge_ref[0:1, :], flipped).astype(bf16)
    edge_ref[...] = g[0:8, :]


def _seqdft(tables, a, b, nb, s, ts, scale):
    half = s // 2
    nh = half // ts
    cp, sm, flip, altrow, j1 = tables
    const = lambda *shape: pl.BlockSpec(shape, lambda bi, i: (0,) * len(shape))
    table_tile = pl.BlockSpec((ts, half), lambda bi, i: (nh - 1 - i, 0))
    return pl.pallas_call(
        functools.partial(_seqdft_kernel, scale=scale),
        grid=(nb, nh),
        in_specs=[
            table_tile,
            table_tile,
            const(ts, ts),
            const(8, half),
            const(*j1.shape),
            pl.BlockSpec((s, F_WIDTH), lambda bi, i: (bi, 0)),
            pl.BlockSpec((s, F_WIDTH), lambda bi, i: (bi, 0)),
        ],
        out_specs=[
            pl.BlockSpec((ts, F_WIDTH), lambda bi, i: (bi * nh + nh - 1 - i, 0)),
            pl.BlockSpec((ts, F_WIDTH), lambda bi, i: (bi * nh + i, 0)),
        ],
        out_shape=[
            jax.ShapeDtypeStruct((nb * half, F_WIDTH), bf16),
            jax.ShapeDtypeStruct((nb * half, F_WIDTH), bf16),
        ],
        scratch_shapes=[
            pltpu.VMEM((half, F_WIDTH), bf16),
            pltpu.VMEM((half, F_WIDTH), bf16),
            pltpu.VMEM((8, F_WIDTH), f32),
        ],
        compiler_params=_params(("parallel", "arbitrary")),
        name="seqdft",
    )(cp, sm, flip, altrow, j1, a, b)


ROUTER_LANES = 128
POST_ROWS = 256


def _post_kernel(fmd_ref, fmm_ref, sgf_ref, ysg_ref, x_ref, wfo_ref, wout_ref, g2_ref, wr_ref,
                 x1_ref, h2_ref, aff_ref, *, tiles_per_half):
    ne = aff_ref.shape[0]
    tm = x_ref.shape[0]
    in_first_half = (pl.program_id(0) % (2 * tiles_per_half)) < tiles_per_half
    chunk = min(POST_ROWS, tm)
    for r0 in range(0, tm, chunk):
        rows = slice(r0, r0 + chunk)
        fm = jnp.where(in_first_half, fmd_ref[rows, :], fmm_ref[rows, :])
        yf = _dot(fm, wfo_ref[...])
        merged = sgf_ref[rows, :].astype(f32) * yf + ysg_ref[rows, :].astype(f32)
        x1 = x_ref[rows, :] + _dot(merged.astype(bf16), wout_ref[...])
        x1_ref[rows, :] = x1
        ms = jnp.mean(x1 * x1, axis=-1, keepdims=True)
        h2 = x1 * lax.rsqrt(ms + EPS) * g2_ref[...]
        h2_ref[rows, :] = h2
        h_hi = h2.astype(bf16)
        h_lo = (h2 - h_hi.astype(f32)).astype(bf16)
        both = _dot(jnp.concatenate([h_hi, h_lo], axis=1), wr_ref[...])
        logits = both[:, :ROUTER_LANES] + both[:, ROUTER_LANES:]
        logits = logits.T[:ne, :]
        mx = jnp.max(logits, axis=0, keepdims=True)
        ex = jnp.exp(logits - mx)
        aff_ref[:, rows] = ex / jnp.sum(ex, axis=0, keepdims=True)


def _post(fm_first, fm_second, sgf, ysg, x2, wfo_b, wout_b, g2, wr_split, ne, s, tm):
    t, d = x2.shape
    th = s // 2 // tm
    const = lambda *shape: pl.BlockSpec(shape, lambda i: (0,) * len(shape))
    tile = lambda w: pl.BlockSpec((tm, w), lambda i: (i, 0))
    first = pl.BlockSpec((tm, F_WIDTH), lambda i: (i // (2 * th) * th + jnp.minimum(i % (2 * th), th - 1), 0))
    second = pl.BlockSpec((tm, F_WIDTH), lambda i: (i // (2 * th) * th + jnp.maximum(i % (2 * th) - th, 0), 0))
    return pl.pallas_call(
        functools.partial(_post_kernel, tiles_per_half=th),
        grid=(t // tm,),
        in_specs=[
            first, second, tile(d), tile(d), tile(d),
            const(F_WIDTH, d),
            const(d, d),
            const(1, d),
            const(2 * d, 2 * ROUTER_LANES),
        ],
        out_specs=[tile(d), tile(d), pl.BlockSpec((ne, tm), lambda i: (0, i))],
        out_shape=[
            jax.ShapeDtypeStruct((t, d), f32),
            jax.ShapeDtypeStruct((t, d), f32),
            jax.ShapeDtypeStruct((ne, t), f32),
        ],
        compiler_params=_params(("parallel",)),
        name="post",
    )(fm_first, fm_second, sgf, ysg, x2, wfo_b, wout_b, g2, wr_split)


def _select_kernel(aff_ref, idx_ref, gate_ref, *, cap):
    ne, s = aff_ref.shape
    nbits = s.bit_length() - 1
    aff = aff_ref[...]
    lane = lax.broadcasted_iota(jnp.int32, (ne, s), 1)

    def bit_step(i, thr):
        cand = thr | jnp.left_shift(jnp.int32(1), 30 - i)
        cnt = jnp.sum((aff >= pltpu.bitcast(cand, f32)).astype(jnp.int32), axis=1, keepdims=True)
        return jnp.where(cnt >= cap, cand, thr)

    thr = lax.fori_loop(1, 31, bit_step, jnp.zeros((ne, 1), jnp.int32))
    gt = aff >= pltpu.bitcast(thr + 1, f32)
    eq = jnp.logical_and(aff >= pltpu.bitcast(thr, f32), jnp.logical_not(gt))
    need = cap - jnp.sum(gt.astype(jnp.int32), axis=1, keepdims=True)

    def cumsum_tokens(v):
        k = 1
        while k < s:
            v = v + jnp.where(lane >= k, pltpu.roll(v, k, axis=1), 0)
            k *= 2
        return v

    sel = jnp.logical_or(gt, jnp.logical_and(eq, cumsum_tokens(eq.astype(jnp.int32)) <= need))
    slot = cumsum_tokens(sel.astype(jnp.int32)) - 1
    word = jnp.where(sel, lane | jnp.left_shift(lane - slot, nbits) | (1 << (2 * nbits)), 0)
    gate = aff
    for k in range(nbits):
        step = 1 << k
        inc_word = pltpu.roll(word, s - step, axis=1)
        inc_gate = pltpu.roll(gate, s - step, axis=1)
        take = (jnp.right_shift(inc_word, nbits + k) & 1) == 1
        leave = (jnp.right_shift(word, nbits + k) & 1) == 1
        word = jnp.where(take, inc_word, jnp.where(leave, 0, word))
        gate = jnp.where(take, inc_gate, gate)
    idx_ref[0] = word[:, :cap] & (s - 1)
    gate_ref[0] = gate[:, :cap]


def _select(aff_t, nb, s, cap):
    ne = aff_t.shape[0]
    return pl.pallas_call(
        functools.partial(_select_kernel, cap=cap),
        grid=(nb,),
        in_specs=[pl.BlockSpec((ne, s), lambda bi: (0, bi))],
        out_specs=[
            pl.BlockSpec((1, ne, cap), lambda bi: (bi, 0, 0)),
            pl.BlockSpec((1, ne, cap), lambda bi: (bi, 0, 0)),
        ],
        out_shape=[
            jax.ShapeDtypeStruct((nb, ne, cap), jnp.int32),
            jax.ShapeDtypeStruct((nb, ne, cap), f32),
        ],
        compiler_params=_params(("parallel",)),
        name="select",
    )(aff_t)


GATHER_ROWS = 64
SCATTER_ROWS = 16


def _gather_kernel(idx_ref, h2_hbm, xe_ref, buf_ref, sem, *, ne, cap):
    b = pl.program_id(0)
    e = pl.program_id(1)
    s = buf_ref.shape[1]
    slot = b % 2
    base = (b * ne + e) * cap

    def batch_copy(batch, into):
        rows = pl.ds(pl.multiple_of(batch * s, s), s)
        return pltpu.make_async_copy(h2_hbm.at[rows, :], buf_ref.at[into], sem.at[into])

    @pl.when(jnp.logical_and(b == 0, e == 0))
    def _():
        batch_copy(0, 0).start()

    @pl.when(e == 0)
    def _():
        @pl.when(b + 1 < pl.num_programs(0))
        def _():
            batch_copy(b + 1, 1 - slot).start()

        batch_copy(b, slot).wait()

    h2_ref = buf_ref.at[slot]

    def body(i, _):
        r0 = pl.multiple_of(i * GATHER_ROWS, GATHER_ROWS)
        rows = [h2_ref[pl.ds(idx_ref[base + r0 + j], 1), :] for j in range(GATHER_ROWS)]
        xe_ref[0, pl.ds(r0, GATHER_ROWS), :] = jnp.concatenate(rows, axis=0).astype(bf16)
        return 0

    lax.fori_loop(0, cap // GATHER_ROWS, body, 0)


def _gather(idx_flat, h2, nb, s, ne, cap):
    d = h2.shape[1]
    return pl.pallas_call(
        functools.partial(_gather_kernel, ne=ne, cap=cap),
        grid_spec=pltpu.PrefetchScalarGridSpec(
            num_scalar_prefetch=1,
            grid=(nb, ne),
            in_specs=[pl.BlockSpec(memory_space=pl.ANY)],
            out_specs=pl.BlockSpec((1, cap, d), lambda bi, e, idx: (e, bi, 0)),
            scratch_shapes=[pltpu.VMEM((2, s, d), f32), pltpu.SemaphoreType.DMA((2,))],
        ),
        out_shape=jax.ShapeDtypeStruct((ne, nb * cap, d), bf16),
        compiler_params=_params(("arbitrary", "arbitrary")),
        name="gather",
    )(idx_flat, h2)


def _expert_kernel(x_ref, wg_ref, wu_ref, wd_ref, gate_ref, y_ref, acc_ref, *, sub, nf):
    f = pl.program_id(2)
    tmo, d = acc_ref.shape

    def partial_out(rows):
        xs = x_ref[0, rows, :]
        a = _dot(xs, wg_ref[0].astype(bf16))
        g = _dot(xs, wu_ref[0].astype(bf16))
        hm = (a * jax.nn.sigmoid(a) * g).astype(bf16)
        return _dot(hm, wd_ref[0].astype(bf16))

    def store_gated(rows, y):
        for c in range(rows.start // GROUP, rows.stop // GROUP):
            chunk = slice(c * GROUP, (c + 1) * GROUP)
            g_col = jnp.broadcast_to(gate_ref[0, :, chunk], (GROUP, GROUP)).T
            y_c = y[chunk.start - rows.start:chunk.stop - rows.start, :]
            y_ref[0, chunk, :] = (y_c * jnp.tile(g_col, (1, d // GROUP))).astype(bf16)

    def run(first, last):
        for r0 in range(0, tmo, sub):
            rows = slice(r0, r0 + sub)
            y = partial_out(rows)
            if not first:
                y = acc_ref[rows, :] + y
            if last:
                store_gated(rows, y)
            else:
                acc_ref[rows, :] = y

    if nf == 1:
        run(True, True)
    else:
        pl.when(f == 0)(lambda: run(True, False))
        pl.when(f == nf - 1)(lambda: run(False, True))
        if nf > 2:
            pl.when(jnp.logical_and(f > 0, f < nf - 1))(lambda: run(False, False))


def _experts(xe, wg, wu, wd, gate_row, tmo, tf, sub):
    ne, m, d = xe.shape
    ff = wg.shape[2]
    return pl.pallas_call(
        functools.partial(_expert_kernel, sub=sub, nf=ff // tf),
        grid=(ne, m // tmo, ff // tf),
        in_specs=[
            pl.BlockSpec((1, tmo, d), lambda e, mi, fi: (e, mi, 0)),
            pl.BlockSpec((1, d, tf), lambda e, mi, fi: (e, 0, fi)),
            pl.BlockSpec((1, d, tf), lambda e, mi, fi: (e, 0, fi)),
            pl.BlockSpec((1, tf, d), lambda e, mi, fi: (e, fi, 0)),
            pl.BlockSpec((1, 1, tmo), lambda e, mi, fi: (e, 0, mi)),
        ],
        out_specs=pl.BlockSpec((1, tmo, d), lambda e, mi, fi: (e, mi, 0)),
        out_shape=jax.ShapeDtypeStruct((ne, m, d), bf16),
        scratch_shapes=[pltpu.VMEM((tmo, d), f32)],
        compiler_params=_params(("parallel", "parallel", "arbitrary")),
        name="experts",
    )(xe, wg, wu, wd, gate_row)


NORM_ROWS = 256


def _combine_kernel(idx_ref, ye_ref, x1_hbm, g_ref, out_ref, x1_ref, sem, *, ne, cap, do_norm):
    b = pl.program_id(0)
    e = pl.program_id(1)
    s = out_ref.shape[0]
    base = (b * ne + e) * cap
    x1_copy = pltpu.make_async_copy(x1_hbm.at[pl.ds(pl.multiple_of(b * s, s), s), :], x1_ref, sem)

    @pl.when(e == 0)
    def _():
        x1_copy.start()
        out_ref[...] = jnp.zeros_like(out_ref)

    def body(i, _):
        r0 = pl.multiple_of(i * SCATTER_ROWS, SCATTER_ROWS)
        blk = ye_ref[0, pl.ds(r0, SCATTER_ROWS), :].astype(f32)
        toks = [idx_ref[base + r0 + j] for j in range(SCATTER_ROWS)]
        rows = [out_ref[pl.ds(tok, 1), :] for tok in toks]
        for j in range(SCATTER_ROWS):
            out_ref[pl.ds(toks[j], 1), :] = rows[j] + blk[j:j + 1, :]
        return 0

    lax.fori_loop(0, cap // SCATTER_ROWS, body, 0)

    @pl.when(e == ne - 1)
    def _():
        x1_copy.wait()

        def norm_rows(c, _):
            rows = pl.ds(pl.multiple_of(c * NORM_ROWS, NORM_ROWS), NORM_ROWS)
            y = x1_ref[rows, :] + out_ref[rows, :]
            if do_norm:
                ms = jnp.mean(y * y, axis=-1, keepdims=True)
                y = y * lax.rsqrt(ms + EPS) * g_ref[...]
            out_ref[rows, :] = y
            return 0

        lax.fori_loop(0, s // NORM_ROWS, norm_rows, 0)


def _combine(idx_flat, ye, x1, g, nb, s, ne, cap, do_norm):
    d = ye.shape[2]
    return pl.pallas_call(
        functools.partial(_combine_kernel, ne=ne, cap=cap, do_norm=do_norm),
        grid_spec=pltpu.PrefetchScalarGridSpec(
            num_scalar_prefetch=1,
            grid=(nb, ne),
            in_specs=[
                pl.BlockSpec((1, cap, d), lambda bi, e, idx: (e, bi, 0)),
                pl.BlockSpec(memory_space=pl.ANY),
                pl.BlockSpec((1, d), lambda bi, e, idx: (0, 0)),
            ],
            out_specs=pl.BlockSpec((s, d), lambda bi, e, idx: (bi, 0)),
            scratch_shapes=[pltpu.VMEM((s, d), f32), pltpu.SemaphoreType.DMA],
        ),
        out_shape=jax.ShapeDtypeStruct((nb * s, d), f32),
        compiler_params=_params(("arbitrary", "arbitrary")),
        name="combine",
    )(idx_flat, ye, x1, g)


def _channel_dft_table():
    k = jnp.arange(GROUP, dtype=jnp.int32)
    ang = ((k[:, None] * k[None, :]) % GROUP).astype(f32) * (2.0 * math.pi / GROUP)
    return jnp.concatenate([jnp.cos(ang), jnp.sin(ang)], axis=1).astype(bf16)


def _sequence_dft_tables(s, ts, scale):
    half = s // 2
    inner = 64
    k = jnp.arange(half, dtype=jnp.int32)[None, :]
    n1 = jnp.arange(half // inner, dtype=jnp.int32)[:, None] * inner
    n2 = jnp.arange(inner, dtype=jnp.int32)[:, None]
    w = 2.0 * math.pi / s
    a1 = ((n1 * k) % s).astype(f32) * w
    a2 = ((n2 * k) % s).astype(f32) * w
    c1, s1 = jnp.cos(a1)[:, None, :], jnp.sin(a1)[:, None, :]
    c2, s2 = jnp.cos(a2)[None, :, :] * scale, jnp.sin(a2)[None, :, :] * scale
    cp = (c1 * c2 - s1 * s2).reshape(half, half).astype(bf16)
    sm = (-(s1 * c2 + c1 * s2)).reshape(half, half).astype(bf16)
    rows = jnp.arange(ts, dtype=jnp.int32)[:, None]
    cols = jnp.arange(ts, dtype=jnp.int32)[None, :]
    flip = (cols == ts - rows).astype(bf16)
    kk = jnp.arange(half, dtype=jnp.int32)[None, :]
    alt = jnp.where(jnp.arange(8)[:, None] == 0, (1 - 2 * (kk & 1)).astype(f32) * scale, 0.0).astype(bf16)
    fold = min(FOLD_ROWS, half)
    r = jnp.arange(fold, dtype=jnp.int32)
    j1 = ((r[None, :] == fold - r[:, None]) & (r[:, None] >= 1)).astype(bf16)
    return cp, sm, flip, alt, j1


def _pick(n, pref):
    return pref if n % pref == 0 else n


def kernel(x, norm1_g, w_in, sgu_ln_g, sgu_ln_b, w_spatial, b_spatial, w_fourier_out, w_sgu_out, w_out,
           norm2_g, w_router, w_gate_e, w_up_e, w_down_e, final_g):
    nb, s, d = x.shape
    depth = norm1_g.shape[0]
    ne = w_router.shape[2]
    ff = w_gate_e.shape[3]
    cap = CAPACITY_FACTOR * s // ne
    t = nb * s
    assert s % NORM_ROWS == 0 and s & (s - 1) == 0 and cap % GATHER_ROWS == 0 and d % GROUP == 0
    assert (nb * cap) % GROUP == 0

    tm_front = _pick(s, 1024)
    ts_dft = _pick(s // 2, 512)
    tm_post = _pick(s // 2, 1024)
    dft_scale = 1.0 / math.sqrt(s * GROUP)
    cs_tab = _channel_dft_table()
    seq_tables = _sequence_dft_tables(s, ts_dft, dft_scale)
    tmo = _pick(nb * cap, 2048)
    tf = _pick(ff, 512)
    sub = _pick(tmo, 1024)

    x2 = x.reshape(t, d)
    for l in range(depth):
        a, b, ysg, sgf = _front(
            x2, norm1_g[l][None, :], w_in[l].astype(bf16), cs_tab,
            sgu_ln_g[l][None, :], sgu_ln_b[l][None, :], w_spatial[l].astype(bf16),
            b_spatial[l].T, w_sgu_out[l].astype(bf16), tm_front)
        fm_first, fm_second = _seqdft(seq_tables, a, b, nb, s, ts_dft, dft_scale)
        wr = jnp.pad(w_router[l], ((0, 0), (0, ROUTER_LANES - ne)))
        wr_hi = wr.astype(bf16)
        wr_lo = (wr - wr_hi.astype(f32)).astype(bf16)
        wr_split = jnp.concatenate([jnp.concatenate([wr_hi, wr_lo], axis=1),
                                    jnp.concatenate([wr_hi, jnp.zeros_like(wr_hi)], axis=1)], axis=0)
        x1, h2, aff = _post(fm_first, fm_second, sgf, ysg, x2, w_fourier_out[l].astype(bf16),
                            w_out[l].astype(bf16), norm2_g[l][None, :], wr_split, ne, s, tm_post)
        idx, gate = _select(aff, nb, s, cap)
        idx_flat = idx.reshape(-1)
        xe = _gather(idx_flat, h2, nb, s, ne, cap)
        gate_row = jnp.transpose(gate, (1, 0, 2)).reshape(ne, 1, nb * cap)
        ye = _experts(xe, w_gate_e[l], w_up_e[l], w_down_e[l], gate_row, tmo, tf, sub)
        x2 = _combine(idx_flat, ye, x1, final_g[None, :], nb, s, ne, cap, l == depth - 1)
    return x2.reshape(nb, s, d)
```

```python
import functools
import math

import jax
import jax.numpy as jnp
from jax import lax
from jax.experimental import pallas as pl
from jax.experimental.pallas import tpu as pltpu

EPS = 1e-6
GROUP = 128
N_GROUPS = 4
F_WIDTH = N_GROUPS * GROUP
S_WIDTH = N_GROUPS * GROUP
CAPACITY_FACTOR = 2
VMEM_LIMIT_V7X = 56 * 1024 * 1024

f32 = jnp.float32
bf16 = jnp.bfloat16


def _dot(a, b):
    return jnp.dot(a, b, preferred_element_type=f32)


def _params(sem, vmem=VMEM_LIMIT_V7X):
    return pltpu.CompilerParams(dimension_semantics=sem, vmem_limit_bytes=vmem)


FRONT_ROWS = 1024


def _front_kernel(x_ref, g1_ref, win_ref, cs_ref, lng_ref, lnb_ref, ws_ref, bs_ref, wso_ref,
                  a_ref, b_ref, ysg_ref, sgf_ref, mix_ref):
    tm, d = x_ref.shape
    c0 = F_WIDTH + 2 * S_WIDTH
    chunk = min(FRONT_ROWS, tm)
    for r0 in range(0, tm, chunk):
        rows = slice(r0, r0 + chunk)
        x = x_ref[rows, :]
        ms = jnp.mean(x * x, axis=-1, keepdims=True)
        h = (x * lax.rsqrt(ms + EPS) * g1_ref[...]).astype(bf16)

        def proj(lo, hi):
            return _dot(h, win_ref[:, lo:hi])

        zf = proj(0, F_WIDTH)
        for g in range(N_GROUPS):
            cols = slice(g * GROUP, (g + 1) * GROUP)
            ab = _dot(zf[:, cols].astype(bf16), cs_ref[...])
            a_ref[rows, cols] = ab[:, :GROUP].astype(bf16)
            b_ref[rows, cols] = ab[:, GROUP:].astype(bf16)

        u_pre = proj(F_WIDTH, F_WIDTH + S_WIDTH)
        v_pre = proj(F_WIDTH + S_WIDTH, F_WIDTH + 2 * S_WIDTH)
        sgf_ref[rows, :] = jax.nn.sigmoid(proj(c0, c0 + d)).astype(bf16)
        gs_pre = proj(c0 + d, c0 + 2 * d)
        u = jax.nn.gelu(u_pre)
        v = jax.nn.gelu(v_pre)
        for g in range(N_GROUPS):
            cols = slice(g * GROUP, (g + 1) * GROUP)
            vg = v[:, cols]
            mu = jnp.mean(vg, axis=-1, keepdims=True)
            dv = vg - mu
            var = jnp.mean(dv * dv, axis=-1, keepdims=True)
            vnb = (dv * lax.rsqrt(var + EPS) * lng_ref[:, cols] + lnb_ref[:, cols]).astype(bf16)
            for c in range(chunk // GROUP):
                m = _dot(ws_ref[g], vnb[c * GROUP:(c + 1) * GROUP, :]) + bs_ref[:, g:g + 1]
                mix_ref[r0 + c * GROUP:r0 + (c + 1) * GROUP, cols] = m
        sgu = (u * mix_ref[rows, :]).astype(bf16)
        ys = _dot(sgu, wso_ref[...])
        ysg_ref[rows, :] = (jax.nn.sigmoid(gs_pre) * ys).astype(bf16)


def _front(x2, g1, win_b, cs_b, lng, lnb, ws_b, bs_t, wso_b, tm):
    t, d = x2.shape
    kin = win_b.shape[1]
    const = lambda *shape: pl.BlockSpec(shape, lambda i: (0,) * len(shape))
    return pl.pallas_call(
        _front_kernel,
        grid=(t // tm,),
        in_specs=[
            pl.BlockSpec((tm, d), lambda i: (i, 0)),
            const(1, d),
            const(d, kin),
            const(GROUP, 2 * GROUP),
            const(1, S_WIDTH),
            const(1, S_WIDTH),
            const(N_GROUPS, GROUP, GROUP),
            const(GROUP, N_GROUPS),
            const(S_WIDTH, d),
        ],
        out_specs=[
            pl.BlockSpec((tm, F_WIDTH), lambda i: (i, 0)),
            pl.BlockSpec((tm, F_WIDTH), lambda i: (i, 0)),
            pl.BlockSpec((tm, d), lambda i: (i, 0)),
            pl.BlockSpec((tm, d), lambda i: (i, 0)),
        ],
        out_shape=[
            jax.ShapeDtypeStruct((t, F_WIDTH), bf16),
            jax.ShapeDtypeStruct((t, F_WIDTH), bf16),
            jax.ShapeDtypeStruct((t, d), bf16),
            jax.ShapeDtypeStruct((t, d), bf16),
        ],
        scratch_shapes=[pltpu.VMEM((tm, S_WIDTH), f32)],
        compiler_params=_params(("parallel",)),
        name="front",
    )(x2, g1, win_b, cs_b, lng, lnb, ws_b, bs_t, wso_b)


FOLD_ROWS = 256


def _seqdft_kernel(cp_ref, sm_ref, flip_ref, altrow_ref, j1_ref, a_ref, b_ref, fmd_ref, fmm_ref,
                   ap_ref, bm_ref, edge_ref, *, scale):
    i = pl.program_id(1)
    ts = fmd_ref.shape[0]
    s = a_ref.shape[0]
    half = s // 2
    a_mid = scale * a_ref[half:half + 1, :].astype(f32)

    @pl.when(i == 0)
    def _fold_inputs():
        fold = j1_ref.shape[0]
        first_row = lax.broadcasted_iota(jnp.int32, (fold, 1), 0) == 0
        for j in range(half // fold):
            lo = slice(j * fold, (j + 1) * fold)
            mir = slice(s - (j + 1) * fold, s - j * fold)
            a_m = _dot(j1_ref[...], a_ref[mir, :])
            b_m = _dot(j1_ref[...], b_ref[mir, :])
            if j > 0:
                a_m = jnp.where(first_row, a_ref[mir.stop:mir.stop + 1, :].astype(f32), a_m)
                b_m = jnp.where(first_row, b_ref[mir.stop:mir.stop + 1, :].astype(f32), b_m)
            ap_ref[lo, :] = (a_ref[lo, :].astype(f32) + a_m).astype(bf16)
            bm_ref[lo, :] = (b_ref[lo, :].astype(f32) - b_m).astype(bf16)
        first = lax.broadcasted_iota(jnp.int32, (8, 1), 0) == 0
        edge_ref[...] = _dot(altrow_ref[...], ap_ref[...]) + jnp.where(first, a_mid, 0.0)

    p = _dot(cp_ref[...], ap_ref[...])
    q = _dot(sm_ref[...], bm_ref[...])
    row = lax.broadcasted_iota(jnp.int32, (ts, 1), 0)
    alt = (1 - 2 * (row & 1)).astype(f32) * a_mid
    fmd_ref[...] = (p + q + alt).astype(bf16)
    g = p - q + alt
    flipped = _dot(flip_ref[...], g.astype(bf16))
    fmm_ref[...] = jnp.where(row == 0, edge_ref[0:1, :], flipped).astype(bf16)
    edge_ref[...] = g[0:8, :]


def _seqdft(tables, a, b, nb, s, ts, scale):
    half = s // 2
    nh = half // ts
    cp, sm, flip, altrow, j1 = tables
    const = lambda *shape: pl.BlockSpec(shape, lambda bi, i: (0,) * len(shape))
    table_tile = pl.BlockSpec((ts, half), lambda bi, i: (nh - 1 - i, 0))
    return pl.pallas_call(
        functools.partial(_seqdft_kernel, scale=scale),
        grid=(nb, nh),
        in_specs=[
            table_tile,
            table_tile,
            const(ts, ts),
            const(8, half),
            const(*j1.shape),
            pl.BlockSpec((s, F_WIDTH), lambda bi, i: (bi, 0)),
            pl.BlockSpec((s, F_WIDTH), lambda bi, i: (bi, 0)),
        ],
        out_specs=[
            pl.BlockSpec((ts, F_WIDTH), lambda bi, i: (bi * nh + nh - 1 - i, 0)),
            pl.BlockSpec((ts, F_WIDTH), lambda bi, i: (bi * nh + i, 0)),
        ],
        out_shape=[
            jax.ShapeDtypeStruct((nb * half, F_WIDTH), bf16),
            jax.ShapeDtypeStruct((nb * half, F_WIDTH), bf16),
        ],
        scratch_shapes=[
            pltpu.VMEM((half, F_WIDTH), bf16),
            pltpu.VMEM((half, F_WIDTH), bf16),
            pltpu.VMEM((8, F_WIDTH), f32),
        ],
        compiler_params=_params(("parallel", "arbitrary")),
        name="seqdft",
    )(cp, sm, flip, altrow, j1, a, b)


ROUTER_LANES = 128
POST_ROWS = 256


def _post_kernel(fmd_ref, fmm_ref, sgf_ref, ysg_ref, x_ref, wfo_ref, wout_ref, g2_ref, wr_ref,
                 x1_ref, h2_ref, aff_ref, *, tiles_per_half):
    ne = aff_ref.shape[0]
    tm = x_ref.shape[0]
    in_first_half = (pl.program_id(0) % (2 * tiles_per_half)) < tiles_per_half
    chunk = min(POST_ROWS, tm)
    for r0 in range(0, tm, chunk):
        rows = slice(r0, r0 + chunk)
        fm = jnp.where(in_first_half, fmd_ref[rows, :], fmm_ref[rows, :])
        yf = _dot(fm, wfo_ref[...])
        merged = sgf_ref[rows, :].astype(f32) * yf + ysg_ref[rows, :].astype(f32)
        x1 = x_ref[rows, :] + _dot(merged.astype(bf16), wout_ref[...])
        x1_ref[rows, :] = x1
        ms = jnp.mean(x1 * x1, axis=-1, keepdims=True)
        h2 = x1 * lax.rsqrt(ms + EPS) * g2_ref[...]
        h2_ref[rows, :] = h2
        h_hi = h2.astype(bf16)
        h_lo = (h2 - h_hi.astype(f32)).astype(bf16)
        both = _dot(jnp.concatenate([h_hi, h_lo], axis=1), wr_ref[...])
        logits = both[:, :ROUTER_LANES] + both[:, ROUTER_LANES:]
        logits = logits.T[:ne, :]
        mx = jnp.max(logits, axis=0, keepdims=True)
        ex = jnp.exp(logits - mx)
        aff_ref[:, rows] = ex / jnp.sum(ex, axis=0, keepdims=True)


def _post(fm_first, fm_second, sgf, ysg, x2, wfo_b, wout_b, g2, wr_split, ne, s, tm):
    t, d = x2.shape
    th = s // 2 // tm
    const = lambda *shape: pl.BlockSpec(shape, lambda i: (0,) * len(shape))
    tile = lambda w: pl.BlockSpec((tm, w), lambda i: (i, 0))
    first = pl.BlockSpec((tm, F_WIDTH), lambda i: (i // (2 * th) * th + jnp.minimum(i % (2 * th), th - 1), 0))
    second = pl.BlockSpec((tm, F_WIDTH), lambda i: (i // (2 * th) * th + jnp.maximum(i % (2 * th) - th, 0), 0))
    return pl.pallas_call(
        functools.partial(_post_kernel, tiles_per_half=th),
        grid=(t // tm,),
        in_specs=[
            first, second, tile(d), tile(d), tile(d),
            const(F_WIDTH, d),
            const(d, d),
            const(1, d),
            const(2 * d, 2 * ROUTER_LANES),
        ],
        out_specs=[tile(d), tile(d), pl.BlockSpec((ne, tm), lambda i: (0, i))],
        out_shape=[
            jax.ShapeDtypeStruct((t, d), f32),
            jax.ShapeDtypeStruct((t, d), f32),
            jax.ShapeDtypeStruct((ne, t), f32),
        ],
        compiler_params=_params(("parallel",)),
        name="post",
    )(fm_first, fm_second, sgf, ysg, x2, wfo_b, wout_b, g2, wr_split)


def _select_kernel(aff_ref, idx_ref, gate_ref, *, cap):
    ne, s = aff_ref.shape
    nbits = s.bit_length() - 1
    aff = aff_ref[...]
    lane = lax.broadcasted_iota(jnp.int32, (ne, s), 1)

    def bit_step(i, thr):
        cand = thr | jnp.left_shift(jnp.int32(1), 30 - i)
        cnt = jnp.sum((aff >= pltpu.bitcast(cand, f32)).astype(jnp.int32), axis=1, keepdims=True)
        return jnp.where(cnt >= cap, cand, thr)

    thr = lax.fori_loop(1, 31, bit_step, jnp.zeros((ne, 1), jnp.int32))
    gt = aff >= pltpu.bitcast(thr + 1, f32)
    eq = jnp.logical_and(aff >= pltpu.bitcast(thr, f32), jnp.logical_not(gt))
    need = cap - jnp.sum(gt.astype(jnp.int32), axis=1, keepdims=True)

    def cumsum_tokens(v):
        k = 1
        while k < s:
            v = v + jnp.where(lane >= k, pltpu.roll(v, k, axis=1), 0)
            k *= 2
        return v

    sel = jnp.logical_or(gt, jnp.logical_and(eq, cumsum_tokens(eq.astype(jnp.int32)) <= need))
    slot = cumsum_tokens(sel.astype(jnp.int32)) - 1
    word = jnp.where(sel, lane | jnp.left_shift(lane - slot, nbits) | (1 << (2 * nbits)), 0)
    gate = aff
    for k in range(nbits):
        step = 1 << k
        inc_word = pltpu.roll(word, s - step, axis=1)
        inc_gate = pltpu.roll(gate, s - step, axis=1)
        take = (jnp.right_shift(inc_word, nbits + k) & 1) == 1
        leave = (jnp.right_shift(word, nbits + k) & 1) == 1
        word = jnp.where(take, inc_word, jnp.where(leave, 0, word))
        gate = jnp.where(take, inc_gate, gate)
    idx_ref[0] = word[:, :cap] & (s - 1)
    gate_ref[0] = gate[:, :cap]


def _select(aff_t, nb, s, cap):
    ne = aff_t.shape[0]
    return pl.pallas_call(
        functools.partial(_select_kernel, cap=cap),
        grid=(nb,),
        in_specs=[pl.BlockSpec((ne, s), lambda bi: (0, bi))],
        out_specs=[
            pl.BlockSpec((1, ne, cap), lambda bi: (bi, 0, 0)),
            pl.BlockSpec((1, ne, cap), lambda bi: (bi, 0, 0)),
        ],
        out_shape=[
            jax.ShapeDtypeStruct((nb, ne, cap), jnp.int32),
            jax.ShapeDtypeStruct((nb, ne, cap), f32),
        ],
        compiler_params=_params(("parallel",)),
        name="select",
    )(aff_t)


GATHER_ROWS = 64
SCATTER_ROWS = 16


def _gather_kernel(idx_ref, h2_hbm, xe_ref, buf_ref, sem, *, ne, cap):
    b = pl.program_id(0)
    e = pl.program_id(1)
    s = buf_ref.shape[1]
    slot = b % 2
    base = (b * ne + e) * cap

    def batch_copy(batch, into):
        rows = pl.ds(pl.multiple_of(batch * s, s), s)
        return pltpu.make_async_copy(h2_hbm.at[rows, :], buf_ref.at[into], sem.at[into])

    @pl.when(jnp.logical_and(b == 0, e == 0))
    def _():
        batch_copy(0, 0).start()

    @pl.when(e == 0)
    def _():
        @pl.when(b + 1 < pl.num_programs(0))
        def _():
            batch_copy(b + 1, 1 - slot).start()

        batch_copy(b, slot).wait()

    h2_ref = buf_ref.at[slot]

    def body(i, _):
        r0 = pl.multiple_of(i * GATHER_ROWS, GATHER_ROWS)
        rows = [h2_ref[pl.ds(idx_ref[base + r0 + j], 1), :] for j in range(GATHER_ROWS)]
        xe_ref[0, pl.ds(r0, GATHER_ROWS), :] = jnp.concatenate(rows, axis=0).astype(bf16)
        return 0

    lax.fori_loop(0, cap // GATHER_ROWS, body, 0)


def _gather(idx_flat, h2, nb, s, ne, cap):
    d = h2.shape[1]
    return pl.pallas_call(
        functools.partial(_gather_kernel, ne=ne, cap=cap),
        grid_spec=pltpu.PrefetchScalarGridSpec(
            num_scalar_prefetch=1,
            grid=(nb, ne),
            in_specs=[pl.BlockSpec(memory_space=pl.ANY)],
            out_specs=pl.BlockSpec((1, cap, d), lambda bi, e, idx: (e, bi, 0)),
            scratch_shapes=[pltpu.VMEM((2, s, d), f32), pltpu.SemaphoreType.DMA((2,))],
        ),
        out_shape=jax.ShapeDtypeStruct((ne, nb * cap, d), bf16),
        compiler_params=_params(("arbitrary", "arbitrary")),
        name="gather",
    )(idx_flat, h2)


def _expert_kernel(x_ref, wg_ref, wu_ref, wd_ref, gate_ref, y_ref, acc_ref, *, sub, nf):
    f = pl.program_id(2)
    tmo, d = acc_ref.shape

    def partial_out(rows, w_gate, w_up, w_down):
        xs = x_ref[0, rows, :]
        a = _dot(xs, w_gate)
        g = _dot(xs, w_up)
        hm = (a * jax.nn.sigmoid(a) * g).astype(bf16)
        return _dot(hm, w_down)

    def store_gated(rows, y):
        for c in range(rows.start // GROUP, rows.stop // GROUP):
            chunk = slice(c * GROUP, (c + 1) * GROUP)
            g_col = jnp.broadcast_to(gate_ref[0, :, chunk], (GROUP, GROUP)).T
            y_c = y[chunk.start - rows.start:chunk.stop - rows.start, :]
            y_ref[0, chunk, :] = (y_c * jnp.tile(g_col, (1, d // GROUP))).astype(bf16)

    def run(first, last):
        weights = [w_ref[0].astype(bf16) for w_ref in (wg_ref, wu_ref, wd_ref)]
        for r0 in range(0, tmo, sub):
            rows = slice(r0, r0 + sub)
            y = partial_out(rows, *weights)
            if not first:
                y = acc_ref[rows, :] + y
            if last:
                store_gated(rows, y)
            else:
                acc_ref[rows, :] = y

    if nf == 1:
        run(True, True)
    else:
        pl.when(f == 0)(lambda: run(True, False))
        pl.when(f == nf - 1)(lambda: run(False, True))
        if nf > 2:
            pl.when(jnp.logical_and(f > 0, f < nf - 1))(lambda: run(False, False))


def _experts(xe, wg, wu, wd, gate_row, tmo, tf, sub):
    ne, m, d = xe.shape
    ff = wg.shape[2]
    return pl.pallas_call(
        functools.partial(_expert_kernel, sub=sub, nf=ff // tf),
        grid=(ne, m // tmo, ff // tf),
        in_specs=[
            pl.BlockSpec((1, tmo, d), lambda e, mi, fi: (e, mi, 0)),
            pl.BlockSpec((1, d, tf), lambda e, mi, fi: (e, 0, fi)),
            pl.BlockSpec((1, d, tf), lambda e, mi, fi: (e, 0, fi)),
            pl.BlockSpec((1, tf, d), lambda e, mi, fi: (e, fi, 0)),
            pl.BlockSpec((1, 1, tmo), lambda e, mi, fi: (e, 0, mi)),
        ],
        out_specs=pl.BlockSpec((1, tmo, d), lambda e, mi, fi: (e, mi, 0)),
        out_shape=jax.ShapeDtypeStruct((ne, m, d), bf16),
        scratch_shapes=[pltpu.VMEM((tmo, d), f32)],
        compiler_params=_params(("parallel", "parallel", "arbitrary")),
        name="experts",
    )(xe, wg, wu, wd, gate_row)


NORM_ROWS = 256


def _combine_kernel(idx_ref, ye_ref, x1_hbm, g_ref, out_ref, x1_ref, sem, *, ne, cap, do_norm):
    b = pl.program_id(0)
    e = pl.program_id(1)
    s = out_ref.shape[0]
    base = (b * ne + e) * cap

    def x1_copy(step):
        rows = pl.ds(pl.multiple_of(b * s, s) + step * NORM_ROWS, NORM_ROWS)
        return pltpu.make_async_copy(x1_hbm.at[rows, :], x1_ref.at[step % 2], sem.at[step % 2])

    nsteps = s // NORM_ROWS

    @pl.when(e == 0)
    def _():
        out_ref[...] = jnp.zeros_like(out_ref)

    @pl.when(e == ne - 1)
    def _():
        for step in range(min(2, nsteps)):
            x1_copy(step).start()

    def body(i, _):
        r0 = pl.multiple_of(i * SCATTER_ROWS, SCATTER_ROWS)
        blk = ye_ref[0, pl.ds(r0, SCATTER_ROWS), :].astype(f32)
        toks = [idx_ref[base + r0 + j] for j in range(SCATTER_ROWS)]
        rows = [out_ref[pl.ds(tok, 1), :] for tok in toks]
        for j in range(SCATTER_ROWS):
            out_ref[pl.ds(toks[j], 1), :] = rows[j] + blk[j:j + 1, :]
        return 0

    lax.fori_loop(0, cap // SCATTER_ROWS, body, 0)

    @pl.when(e == ne - 1)
    def _():
        for step in range(nsteps):
            x1_copy(step).wait()
            rows = slice(step * NORM_ROWS, (step + 1) * NORM_ROWS)
            y = x1_ref[step % 2] + out_ref[rows, :]
            if step + 2 < nsteps:
                x1_copy(step + 2).start()
            if do_norm:
                ms = jnp.mean(y * y, axis=-1, keepdims=True)
                y = y * lax.rsqrt(ms + EPS) * g_ref[...]
            out_ref[rows, :] = y


def _combine(idx_flat, ye, x1, g, nb, s, ne, cap, do_norm):
    d = ye.shape[2]
    return pl.pallas_call(
        functools.partial(_combine_kernel, ne=ne, cap=cap, do_norm=do_norm),
        grid_spec=pltpu.PrefetchScalarGridSpec(
            num_scalar_prefetch=1,
            grid=(nb, ne),
            in_specs=[
                pl.BlockSpec((1, cap, d), lambda bi, e, idx: (e, bi, 0)),
                pl.BlockSpec(memory_space=pl.ANY),
                pl.BlockSpec((1, d), lambda bi, e, idx: (0, 0)),
            ],
            out_specs=pl.BlockSpec((s, d), lambda bi, e, idx: (bi, 0)),
            scratch_shapes=[pltpu.VMEM((2, NORM_ROWS, d), f32), pltpu.SemaphoreType.DMA((2,))],
        ),
        out_shape=jax.ShapeDtypeStruct((nb * s, d), f32),
        compiler_params=_params(("arbitrary", "arbitrary")),
        name="combine",
    )(idx_flat, ye, x1, g)


def _channel_dft_table():
    k = jnp.arange(GROUP, dtype=jnp.int32)
    ang = ((k[:, None] * k[None, :]) % GROUP).astype(f32) * (2.0 * math.pi / GROUP)
    return jnp.concatenate([jnp.cos(ang), jnp.sin(ang)], axis=1).astype(bf16)


def _sequence_dft_tables(s, ts, scale):
    half = s // 2
    inner = 64
    k = jnp.arange(half, dtype=jnp.int32)[None, :]
    n1 = jnp.arange(half // inner, dtype=jnp.int32)[:, None] * inner
    n2 = jnp.arange(inner, dtype=jnp.int32)[:, None]
    w = 2.0 * math.pi / s
    a1 = ((n1 * k) % s).astype(f32) * w
    a2 = ((n2 * k) % s).astype(f32) * w
    c1, s1 = jnp.cos(a1)[:, None, :], jnp.sin(a1)[:, None, :]
    c2, s2 = jnp.cos(a2)[None, :, :] * scale, jnp.sin(a2)[None, :, :] * scale
    cp = (c1 * c2 - s1 * s2).reshape(half, half).astype(bf16)
    sm = (-(s1 * c2 + c1 * s2)).reshape(half, half).astype(bf16)
    rows = jnp.arange(ts, dtype=jnp.int32)[:, None]
    cols = jnp.arange(ts, dtype=jnp.int32)[None, :]
    flip = (cols == ts - rows).astype(bf16)
    kk = jnp.arange(half, dtype=jnp.int32)[None, :]
    alt = jnp.where(jnp.arange(8)[:, None] == 0, (1 - 2 * (kk & 1)).astype(f32) * scale, 0.0).astype(bf16)
    fold = min(FOLD_ROWS, half)
    r = jnp.arange(fold, dtype=jnp.int32)
    j1 = ((r[None, :] == fold - r[:, None]) & (r[:, None] >= 1)).astype(bf16)
    return cp, sm, flip, alt, j1


def _pick(n, pref):
    return pref if n % pref == 0 else n


def kernel(x, norm1_g, w_in, sgu_ln_g, sgu_ln_b, w_spatial, b_spatial, w_fourier_out, w_sgu_out, w_out,
           norm2_g, w_router, w_gate_e, w_up_e, w_down_e, final_g):
    nb, s, d = x.shape
    depth = norm1_g.shape[0]
    ne = w_router.shape[2]
    ff = w_gate_e.shape[3]
    cap = CAPACITY_FACTOR * s // ne
    t = nb * s
    assert s % NORM_ROWS == 0 and s & (s - 1) == 0 and cap % GATHER_ROWS == 0 and d % GROUP == 0
    assert (nb * cap) % GROUP == 0

    tm_front = _pick(s, 1024)
    ts_dft = _pick(s // 2, 512)
    tm_post = _pick(s // 2, 1024)
    dft_scale = 1.0 / math.sqrt(s * GROUP)
    cs_tab = _channel_dft_table()
    seq_tables = _sequence_dft_tables(s, ts_dft, dft_scale)
    tmo = _pick(nb * cap, 2048)
    tf = _pick(ff, 512)
    sub = _pick(tmo, 1024)

    x2 = x.reshape(t, d)
    for l in range(depth):
        a, b, ysg, sgf = _front(
            x2, norm1_g[l][None, :], w_in[l].astype(bf16), cs_tab,
            sgu_ln_g[l][None, :], sgu_ln_b[l][None, :], w_spatial[l].astype(bf16),
            b_spatial[l].T, w_sgu_out[l].astype(bf16), tm_front)
        fm_first, fm_second = _seqdft(seq_tables, a, b, nb, s, ts_dft, dft_scale)
        wr = jnp.pad(w_router[l], ((0, 0), (0, ROUTER_LANES - ne)))
        wr_hi = wr.astype(bf16)
        wr_lo = (wr - wr_hi.astype(f32)).astype(bf16)
        wr_split = jnp.concatenate([jnp.concatenate([wr_hi, wr_lo], axis=1),
                                    jnp.concatenate([wr_hi, jnp.zeros_like(wr_hi)], axis=1)], axis=0)
        x1, h2, aff = _post(fm_first, fm_second, sgf, ysg, x2, w_fourier_out[l].astype(bf16),
                            w_out[l].astype(bf16), norm2_g[l][None, :], wr_split, ne, s, tm_post)
        idx, gate = _select(aff, nb, s, cap)
        idx_flat = idx.reshape(-1)
        xe = _gather(idx_flat, h2, nb, s, ne, cap)
        gate_row = jnp.transpose(gate, (1, 0, 2)).reshape(ne, 1, nb * cap)
        ye = _experts(xe, w_gate_e[l], w_up_e[l], w_down_e[l], gate_row, tmo, tf, sub)
        x2 = _combine(idx_flat, ye, x1, final_g[None, :], nb, s, ne, cap, l == depth - 1)
    return x2.reshape(nb, s, d)
```

```python
import functools
import math

import jax
import jax.numpy as jnp
from jax import lax
from jax.experimental import pallas as pl
from jax.experimental.pallas import tpu as pltpu

EPS = 1e-6
GROUP = 128
N_GROUPS = 4
F_WIDTH = N_GROUPS * GROUP
S_WIDTH = N_GROUPS * GROUP
CAPACITY_FACTOR = 2
VMEM_LIMIT_V7X = 56 * 1024 * 1024

f32 = jnp.float32
bf16 = jnp.bfloat16


def _dot(a, b):
    return jnp.dot(a, b, preferred_element_type=f32)


def _params(sem, vmem=VMEM_LIMIT_V7X):
    return pltpu.CompilerParams(dimension_semantics=sem, vmem_limit_bytes=vmem)


FRONT_ROWS = 1024


def _front_kernel(x_ref, g1_ref, win_ref, cs_ref, lng_ref, lnb_ref, ws_ref, bs_ref, wso_ref,
                  a_ref, b_ref, ysg_ref, sgf_ref, mix_ref):
    tm, d = x_ref.shape
    c0 = F_WIDTH + 2 * S_WIDTH
    chunk = min(FRONT_ROWS, tm)
    for r0 in range(0, tm, chunk):
        rows = slice(r0, r0 + chunk)
        x = x_ref[rows, :]
        ms = jnp.mean(x * x, axis=-1, keepdims=True)
        h = (x * lax.rsqrt(ms + EPS) * g1_ref[...]).astype(bf16)

        def proj(lo, hi):
            return _dot(h, win_ref[:, lo:hi])

        zf = proj(0, F_WIDTH)
        for g in range(N_GROUPS):
            cols = slice(g * GROUP, (g + 1) * GROUP)
            ab = _dot(zf[:, cols].astype(bf16), cs_ref[...])
            a_ref[rows, cols] = ab[:, :GROUP].astype(bf16)
            b_ref[rows, cols] = ab[:, GROUP:].astype(bf16)

        u_pre = proj(F_WIDTH, F_WIDTH + S_WIDTH)
        v_pre = proj(F_WIDTH + S_WIDTH, F_WIDTH + 2 * S_WIDTH)
        sgf_ref[rows, :] = jax.nn.sigmoid(proj(c0, c0 + d)).astype(bf16)
        gs_pre = proj(c0 + d, c0 + 2 * d)
        u = jax.nn.gelu(u_pre)
        v = jax.nn.gelu(v_pre)
        for g in range(N_GROUPS):
            cols = slice(g * GROUP, (g + 1) * GROUP)
            vg = v[:, cols]
            mu = jnp.mean(vg, axis=-1, keepdims=True)
            dv = vg - mu
            var = jnp.mean(dv * dv, axis=-1, keepdims=True)
            vnb = (dv * lax.rsqrt(var + EPS) * lng_ref[:, cols] + lnb_ref[:, cols]).astype(bf16)
            for c in range(chunk // GROUP):
                m = _dot(ws_ref[g], vnb[c * GROUP:(c + 1) * GROUP, :]) + bs_ref[:, g:g + 1]
                mix_ref[r0 + c * GROUP:r0 + (c + 1) * GROUP, cols] = m
        sgu = (u * mix_ref[rows, :]).astype(bf16)
        ys = _dot(sgu, wso_ref[...])
        ysg_ref[rows, :] = (jax.nn.sigmoid(gs_pre) * ys).astype(bf16)


def _front(x2, g1, win_b, cs_b, lng, lnb, ws_b, bs_t, wso_b, tm):
    t, d = x2.shape
    kin = win_b.shape[1]
    const = lambda *shape: pl.BlockSpec(shape, lambda i: (0,) * len(shape))
    return pl.pallas_call(
        _front_kernel,
        grid=(t // tm,),
        in_specs=[
            pl.BlockSpec((tm, d), lambda i: (i, 0)),
            const(1, d),
            const(d, kin),
            const(GROUP, 2 * GROUP),
            const(1, S_WIDTH),
            const(1, S_WIDTH),
            const(N_GROUPS, GROUP, GROUP),
            const(GROUP, N_GROUPS),
            const(S_WIDTH, d),
        ],
        out_specs=[
            pl.BlockSpec((tm, F_WIDTH), lambda i: (i, 0)),
            pl.BlockSpec((tm, F_WIDTH), lambda i: (i, 0)),
            pl.BlockSpec((tm, d), lambda i: (i, 0)),
            pl.BlockSpec((tm, d), lambda i: (i, 0)),
        ],
        out_shape=[
            jax.ShapeDtypeStruct((t, F_WIDTH), bf16),
            jax.ShapeDtypeStruct((t, F_WIDTH), bf16),
            jax.ShapeDtypeStruct((t, d), bf16),
            jax.ShapeDtypeStruct((t, d), bf16),
        ],
        scratch_shapes=[pltpu.VMEM((tm, S_WIDTH), f32)],
        compiler_params=_params(("parallel",)),
        name="front",
    )(x2, g1, win_b, cs_b, lng, lnb, ws_b, bs_t, wso_b)


FOLD_ROWS = 256


def _seqdft_kernel(cp_ref, sm_ref, flip_ref, altrow_ref, j1_ref, a_ref, b_ref, fmd_ref, fmm_ref,
                   ap_ref, bm_ref, edge_ref, *, scale):
    i = pl.program_id(1)
    ts = fmd_ref.shape[0]
    s = a_ref.shape[0]
    half = s // 2
    a_mid = scale * a_ref[half:half + 1, :].astype(f32)

    @pl.when(i == 0)
    def _fold_inputs():
        fold = j1_ref.shape[0]
        first_row = lax.broadcasted_iota(jnp.int32, (fold, 1), 0) == 0
        for j in range(half // fold):
            lo = slice(j * fold, (j + 1) * fold)
            mir = slice(s - (j + 1) * fold, s - j * fold)
            a_m = _dot(j1_ref[...], a_ref[mir, :])
            b_m = _dot(j1_ref[...], b_ref[mir, :])
            if j > 0:
                a_m = jnp.where(first_row, a_ref[mir.stop:mir.stop + 1, :].astype(f32), a_m)
                b_m = jnp.where(first_row, b_ref[mir.stop:mir.stop + 1, :].astype(f32), b_m)
            ap_ref[lo, :] = (a_ref[lo, :].astype(f32) + a_m).astype(bf16)
            bm_ref[lo, :] = (b_ref[lo, :].astype(f32) - b_m).astype(bf16)
        first = lax.broadcasted_iota(jnp.int32, (8, 1), 0) == 0
        edge_ref[...] = _dot(altrow_ref[...], ap_ref[...]) + jnp.where(first, a_mid, 0.0)

    p = _dot(cp_ref[...], ap_ref[...])
    q = _dot(sm_ref[...], bm_ref[...])
    row = lax.broadcasted_iota(jnp.int32, (ts, 1), 0)
    alt = (1 - 2 * (row & 1)).astype(f32) * a_mid
    fmd_ref[...] = (p + q + alt).astype(bf16)
    g = p - q + alt
    flipped = _dot(flip_ref[...], g.astype(bf16))
    fmm_ref[...] = jnp.where(row == 0, edge_ref[0:1, :], flipped).astype(bf16)
    edge_ref[...] = g[0:8, :]


def _seqdft(tables, a, b, nb, s, ts, scale):
    half = s // 2
    nh = half // ts
    cp, sm, flip, altrow, j1 = tables
    const = lambda *shape: pl.BlockSpec(shape, lambda bi, i: (0,) * len(shape))
    table_tile = pl.BlockSpec((ts, half), lambda bi, i: (nh - 1 - i, 0))
    return pl.pallas_call(
        functools.partial(_seqdft_kernel, scale=scale),
        grid=(nb, nh),
        in_specs=[
            table_tile,
            table_tile,
            const(ts, ts),
            const(8, half),
            const(*j1.shape),
            pl.BlockSpec((s, F_WIDTH), lambda bi, i: (bi, 0)),
            pl.BlockSpec((s, F_WIDTH), lambda bi, i: (bi, 0)),
        ],
        out_specs=[
            pl.BlockSpec((ts, F_WIDTH), lambda bi, i: (bi * nh + nh - 1 - i, 0)),
            pl.BlockSpec((ts, F_WIDTH), lambda bi, i: (bi * nh + i, 0)),
        ],
        out_shape=[
            jax.ShapeDtypeStruct((nb * half, F_WIDTH), bf16),
            jax.ShapeDtypeStruct((nb * half, F_WIDTH), bf16),
        ],
        scratch_shapes=[
            pltpu.VMEM((half, F_WIDTH), bf16),
            pltpu.VMEM((half, F_WIDTH), bf16),
            pltpu.VMEM((8, F_WIDTH), f32),
        ],
        compiler_params=_params(("parallel", "arbitrary")),
        name="seqdft",
    )(cp, sm, flip, altrow, j1, a, b)


ROUTER_LANES = 128
POST_ROWS = 256


def _post_kernel(fmd_ref, fmm_ref, sgf_ref, ysg_ref, x_ref, wfo_ref, wout_ref, g2_ref, wr_ref,
                 x1_ref, h2_ref, aff_ref, *, tiles_per_half):
    ne = aff_ref.shape[0]
    tm = x_ref.shape[0]
    in_first_half = (pl.program_id(0) % (2 * tiles_per_half)) < tiles_per_half
    chunk = min(POST_ROWS, tm)
    for r0 in range(0, tm, chunk):
        rows = slice(r0, r0 + chunk)
        fm = jnp.where(in_first_half, fmd_ref[rows, :], fmm_ref[rows, :])
        yf = _dot(fm, wfo_ref[...])
        merged = sgf_ref[rows, :].astype(f32) * yf + ysg_ref[rows, :].astype(f32)
        x1 = x_ref[rows, :] + _dot(merged.astype(bf16), wout_ref[...])
        x1_ref[rows, :] = x1
        ms = jnp.mean(x1 * x1, axis=-1, keepdims=True)
        h2 = x1 * lax.rsqrt(ms + EPS) * g2_ref[...]
        h2_ref[rows, :] = h2
        h_hi = h2.astype(bf16)
        h_lo = (h2 - h_hi.astype(f32)).astype(bf16)
        both = _dot(jnp.concatenate([h_hi, h_lo], axis=1), wr_ref[...])
        logits = both[:, :ROUTER_LANES] + both[:, ROUTER_LANES:]
        logits = logits.T[:ne, :]
        mx = jnp.max(logits, axis=0, keepdims=True)
        ex = jnp.exp(logits - mx)
        aff_ref[:, rows] = ex / jnp.sum(ex, axis=0, keepdims=True)


def _post(fm_first, fm_second, sgf, ysg, x2, wfo_b, wout_b, g2, wr_split, ne, s, tm):
    t, d = x2.shape
    th = s // 2 // tm
    const = lambda *shape: pl.BlockSpec(shape, lambda i: (0,) * len(shape))
    tile = lambda w: pl.BlockSpec((tm, w), lambda i: (i, 0))
    first = pl.BlockSpec((tm, F_WIDTH), lambda i: (i // (2 * th) * th + jnp.minimum(i % (2 * th), th - 1), 0))
    second = pl.BlockSpec((tm, F_WIDTH), lambda i: (i // (2 * th) * th + jnp.maximum(i % (2 * th) - th, 0), 0))
    return pl.pallas_call(
        functools.partial(_post_kernel, tiles_per_half=th),
        grid=(t // tm,),
        in_specs=[
            first, second, tile(d), tile(d), tile(d),
            const(F_WIDTH, d),
            const(d, d),
            const(1, d),
            const(2 * d, 2 * ROUTER_LANES),
        ],
        out_specs=[tile(d), tile(d), pl.BlockSpec((ne, tm), lambda i: (0, i))],
        out_shape=[
            jax.ShapeDtypeStruct((t, d), f32),
            jax.ShapeDtypeStruct((t, d), f32),
            jax.ShapeDtypeStruct((ne, t), f32),
        ],
        compiler_params=_params(("parallel",)),
        name="post",
    )(fm_first, fm_second, sgf, ysg, x2, wfo_b, wout_b, g2, wr_split)


def _select_kernel(aff_ref, idx_ref, gate_ref, *, cap):
    ne, s = aff_ref.shape
    nbits = s.bit_length() - 1
    aff = aff_ref[...]
    lane = lax.broadcasted_iota(jnp.int32, (ne, s), 1)

    def enough(cand):
        cnt = jnp.sum((aff >= pltpu.bitcast(cand, f32)).astype(jnp.int32), axis=1, keepdims=True)
        return cnt >= cap

    def two_bit_step(i, thr):
        hi = jnp.left_shift(jnp.int32(1), 29 - 2 * i)
        lo = jnp.left_shift(jnp.int32(1), 28 - 2 * i)
        c1, c2, c3 = thr | lo, thr | hi, thr | hi | lo
        return jnp.where(enough(c3), c3, jnp.where(enough(c2), c2, jnp.where(enough(c1), c1, thr)))

    thr = lax.fori_loop(0, 15, two_bit_step, jnp.zeros((ne, 1), jnp.int32))
    gt = aff >= pltpu.bitcast(thr + 1, f32)
    eq = jnp.logical_and(aff >= pltpu.bitcast(thr, f32), jnp.logical_not(gt))
    need = cap - jnp.sum(gt.astype(jnp.int32), axis=1, keepdims=True)

    def cumsum_tokens(v):
        k = 1
        while k < s:
            v = v + jnp.where(lane >= k, pltpu.roll(v, k, axis=1), 0)
            k *= 2
        return v

    eq_i = eq.astype(jnp.int32)
    surplus = jnp.max(jnp.sum(eq_i, axis=1, keepdims=True) - need) > 0
    keep = lax.cond(surplus, lambda: (cumsum_tokens(eq_i) <= need).astype(jnp.int32), lambda: eq_i)
    sel = jnp.logical_or(gt, jnp.logical_and(eq, keep == 1))
    slot = cumsum_tokens(sel.astype(jnp.int32)) - 1
    word = jnp.where(sel, lane | jnp.left_shift(lane - slot, nbits) | (1 << (2 * nbits)), 0)
    gate = aff
    for k in range(nbits):
        step = 1 << k
        inc_word = pltpu.roll(word, s - step, axis=1)
        inc_gate = pltpu.roll(gate, s - step, axis=1)
        take = (jnp.right_shift(inc_word, nbits + k) & 1) == 1
        leave = (jnp.right_shift(word, nbits + k) & 1) == 1
        word = jnp.where(take, inc_word, jnp.where(leave, 0, word))
        gate = jnp.where(take, inc_gate, gate)
    idx_ref[0] = word[:, :cap] & (s - 1)
    gate_ref[0] = gate[:, :cap]


def _select(aff_t, nb, s, cap):
    ne = aff_t.shape[0]
    return pl.pallas_call(
        functools.partial(_select_kernel, cap=cap),
        grid=(nb,),
        in_specs=[pl.BlockSpec((ne, s), lambda bi: (0, bi))],
        out_specs=[
            pl.BlockSpec((1, ne, cap), lambda bi: (bi, 0, 0)),
            pl.BlockSpec((1, ne, cap), lambda bi: (bi, 0, 0)),
        ],
        out_shape=[
            jax.ShapeDtypeStruct((nb, ne, cap), jnp.int32),
            jax.ShapeDtypeStruct((nb, ne, cap), f32),
        ],
        compiler_params=_params(("parallel",)),
        name="select",
    )(aff_t)


GATHER_ROWS = 64
SCATTER_ROWS = 16


def _gather_kernel(idx_ref, h2_hbm, xe_ref, buf_ref, sem, *, ne, cap):
    b = pl.program_id(0)
    e = pl.program_id(1)
    s = buf_ref.shape[1]
    slot = b % 2
    base = (b * ne + e) * cap

    def batch_copy(batch, into):
        rows = pl.ds(pl.multiple_of(batch * s, s), s)
        return pltpu.make_async_copy(h2_hbm.at[rows, :], buf_ref.at[into], sem.at[into])

    @pl.when(jnp.logical_and(b == 0, e == 0))
    def _():
        batch_copy(0, 0).start()

    @pl.when(e == 0)
    def _():
        @pl.when(b + 1 < pl.num_programs(0))
        def _():
            batch_copy(b + 1, 1 - slot).start()

        batch_copy(b, slot).wait()

    h2_ref = buf_ref.at[slot]

    def body(i, _):
        r0 = pl.multiple_of(i * GATHER_ROWS, GATHER_ROWS)
        rows = [h2_ref[pl.ds(idx_ref[base + r0 + j], 1), :] for j in range(GATHER_ROWS)]
        xe_ref[0, pl.ds(r0, GATHER_ROWS), :] = jnp.concatenate(rows, axis=0).astype(bf16)
        return 0

    lax.fori_loop(0, cap // GATHER_ROWS, body, 0)


def _gather(idx_flat, h2, nb, s, ne, cap):
    d = h2.shape[1]
    return pl.pallas_call(
        functools.partial(_gather_kernel, ne=ne, cap=cap),
        grid_spec=pltpu.PrefetchScalarGridSpec(
            num_scalar_prefetch=1,
            grid=(nb, ne),
            in_specs=[pl.BlockSpec(memory_space=pl.ANY)],
            out_specs=pl.BlockSpec((1, cap, d), lambda bi, e, idx: (e, bi, 0)),
            scratch_shapes=[pltpu.VMEM((2, s, d), f32), pltpu.SemaphoreType.DMA((2,))],
        ),
        out_shape=jax.ShapeDtypeStruct((ne, nb * cap, d), bf16),
        compiler_params=_params(("arbitrary", "arbitrary")),
        name="gather",
    )(idx_flat, h2)


def _expert_kernel(x_ref, wg_ref, wu_ref, wd_ref, gate_ref, y_ref, acc_ref, *, sub, nf):
    f = pl.program_id(2)
    tmo, d = acc_ref.shape

    def partial_out(rows, w_gate, w_up, w_down):
        xs = x_ref[0, rows, :]
        a = _dot(xs, w_gate)
        g = _dot(xs, w_up)
        hm = (a * jax.nn.sigmoid(a) * g).astype(bf16)
        return _dot(hm, w_down)

    def store_gated(rows, y):
        for c in range(rows.start // GROUP, rows.stop // GROUP):
            chunk = slice(c * GROUP, (c + 1) * GROUP)
            g_col = jnp.broadcast_to(gate_ref[0, :, chunk], (GROUP, GROUP)).T
            y_c = y[chunk.start - rows.start:chunk.stop - rows.start, :]
            y_ref[0, chunk, :] = (y_c * jnp.tile(g_col, (1, d // GROUP))).astype(bf16)

    def run(first, last):
        weights = [w_ref[0].astype(bf16) for w_ref in (wg_ref, wu_ref, wd_ref)]
        for r0 in range(0, tmo, sub):
            rows = slice(r0, r0 + sub)
            y = partial_out(rows, *weights)
            if not first:
                y = acc_ref[rows, :] + y
            if last:
                store_gated(rows, y)
            else:
                acc_ref[rows, :] = y

    if nf == 1:
        run(True, True)
    else:
        pl.when(f == 0)(lambda: run(True, False))
        pl.when(f == nf - 1)(lambda: run(False, True))
        if nf > 2:
            pl.when(jnp.logical_and(f > 0, f < nf - 1))(lambda: run(False, False))


def _experts(xe, wg, wu, wd, gate_row, tmo, tf, sub):
    ne, m, d = xe.shape
    ff = wg.shape[2]
    return pl.pallas_call(
        functools.partial(_expert_kernel, sub=sub, nf=ff // tf),
        grid=(ne, m // tmo, ff // tf),
        in_specs=[
            pl.BlockSpec((1, tmo, d), lambda e, mi, fi: (e, mi, 0)),
            pl.BlockSpec((1, d, tf), lambda e, mi, fi: (e, 0, fi)),
            pl.BlockSpec((1, d, tf), lambda e, mi, fi: (e, 0, fi)),
            pl.BlockSpec((1, tf, d), lambda e, mi, fi: (e, fi, 0)),
            pl.BlockSpec((1, 1, tmo), lambda e, mi, fi: (e, 0, mi)),
        ],
        out_specs=pl.BlockSpec((1, tmo, d), lambda e, mi, fi: (e, mi, 0)),
        out_shape=jax.ShapeDtypeStruct((ne, m, d), bf16),
        scratch_shapes=[pltpu.VMEM((tmo, d), f32)],
        compiler_params=_params(("parallel", "parallel", "arbitrary")),
        name="experts",
    )(xe, wg, wu, wd, gate_row)


NORM_ROWS = 256


def _combine_kernel(idx_ref, ye_ref, x1_hbm, g_ref, out_ref, x1_ref, sem, *, ne, cap, do_norm):
    b = pl.program_id(0)
    e = pl.program_id(1)
    s = out_ref.shape[0]
    base = (b * ne + e) * cap
    x1_copy = pltpu.make_async_copy(x1_hbm.at[pl.ds(pl.multiple_of(b * s, s), s), :], x1_ref, sem)

    @pl.when(e == 0)
    def _():
        x1_copy.start()
        out_ref[...] = jnp.zeros_like(out_ref)

    def body(i, _):
        r0 = pl.multiple_of(i * SCATTER_ROWS, SCATTER_ROWS)
        blk = ye_ref[0, pl.ds(r0, SCATTER_ROWS), :].astype(f32)
        toks = [idx_ref[base + r0 + j] for j in range(SCATTER_ROWS)]
        rows = [out_ref[pl.ds(tok, 1), :] for tok in toks]
        for j in range(SCATTER_ROWS):
            out_ref[pl.ds(toks[j], 1), :] = rows[j] + blk[j:j + 1, :]
        return 0

    lax.fori_loop(0, cap // SCATTER_ROWS, body, 0)

    @pl.when(e == ne - 1)
    def _():
        x1_copy.wait()

        def norm_rows(c, _):
            rows = pl.ds(pl.multiple_of(c * NORM_ROWS, NORM_ROWS), NORM_ROWS)
            y = x1_ref[rows, :] + out_ref[rows, :]
            if do_norm:
                ms = jnp.mean(y * y, axis=-1, keepdims=True)
                y = y * lax.rsqrt(ms + EPS) * g_ref[...]
            out_ref[rows, :] = y
            return 0

        lax.fori_loop(0, s // NORM_ROWS, norm_rows, 0)


def _combine(idx_flat, ye, x1, g, nb, s, ne, cap, do_norm):
    d = ye.shape[2]
    return pl.pallas_call(
        functools.partial(_combine_kernel, ne=ne, cap=cap, do_norm=do_norm),
        grid_spec=pltpu.PrefetchScalarGridSpec(
            num_scalar_prefetch=1,
            grid=(nb, ne),
            in_specs=[
                pl.BlockSpec((1, cap, d), lambda bi, e, idx: (e, bi, 0)),
                pl.BlockSpec(memory_space=pl.ANY),
                pl.BlockSpec((1, d), lambda bi, e, idx: (0, 0)),
            ],
            out_specs=pl.BlockSpec((s, d), lambda bi, e, idx: (bi, 0)),
            scratch_shapes=[pltpu.VMEM((s, d), f32), pltpu.SemaphoreType.DMA],
        ),
        out_shape=jax.ShapeDtypeStruct((nb * s, d), f32),
        compiler_params=_params(("arbitrary", "arbitrary")),
        name="combine",
    )(idx_flat, ye, x1, g)


def _channel_dft_table():
    k = jnp.arange(GROUP, dtype=jnp.int32)
    ang = ((k[:, None] * k[None, :]) % GROUP).astype(f32) * (2.0 * math.pi / GROUP)
    return jnp.concatenate([jnp.cos(ang), jnp.sin(ang)], axis=1).astype(bf16)


def _sequence_dft_tables(s, ts, scale):
    half = s // 2
    inner = 64
    k = jnp.arange(half, dtype=jnp.int32)[None, :]
    n1 = jnp.arange(half // inner, dtype=jnp.int32)[:, None] * inner
    n2 = jnp.arange(inner, dtype=jnp.int32)[:, None]
    w = 2.0 * math.pi / s
    a1 = ((n1 * k) % s).astype(f32) * w
    a2 = ((n2 * k) % s).astype(f32) * w
    c1, s1 = jnp.cos(a1)[:, None, :], jnp.sin(a1)[:, None, :]
    c2, s2 = jnp.cos(a2)[None, :, :] * scale, jnp.sin(a2)[None, :, :] * scale
    cp = (c1 * c2 - s1 * s2).reshape(half, half).astype(bf16)
    sm = (-(s1 * c2 + c1 * s2)).reshape(half, half).astype(bf16)
    rows = jnp.arange(ts, dtype=jnp.int32)[:, None]
    cols = jnp.arange(ts, dtype=jnp.int32)[None, :]
    flip = (cols == ts - rows).astype(bf16)
    kk = jnp.arange(half, dtype=jnp.int32)[None, :]
    alt = jnp.where(jnp.arange(8)[:, None] == 0, (1 - 2 * (kk & 1)).astype(f32) * scale, 0.0).astype(bf16)
    fold = min(FOLD_ROWS, half)
    r = jnp.arange(fold, dtype=jnp.int32)
    j1 = ((r[None, :] == fold - r[:, None]) & (r[:, None] >= 1)).astype(bf16)
    return cp, sm, flip, alt, j1


def _pick(n, pref):
    return pref if n % pref == 0 else n


def kernel(x, norm1_g, w_in, sgu_ln_g, sgu_ln_b, w_spatial, b_spatial, w_fourier_out, w_sgu_out, w_out,
           norm2_g, w_router, w_gate_e, w_up_e, w_down_e, final_g):
    nb, s, d = x.shape
    depth = norm1_g.shape[0]
    ne = w_router.shape[2]
    ff = w_gate_e.shape[3]
    cap = CAPACITY_FACTOR * s // ne
    t = nb * s
    assert s % NORM_ROWS == 0 and s & (s - 1) == 0 and cap % GATHER_ROWS == 0 and d % GROUP == 0
    assert (nb * cap) % GROUP == 0

    tm_front = _pick(s, 1024)
    ts_dft = _pick(s // 2, 512)
    tm_post = _pick(s // 2, 1024)
    dft_scale = 1.0 / math.sqrt(s * GROUP)
    cs_tab = _channel_dft_table()
    seq_tables = _sequence_dft_tables(s, ts_dft, dft_scale)
    tmo = _pick(nb * cap, 2048)
    tf = _pick(ff, 512)
    sub = _pick(tmo, 1024)

    x2 = x.reshape(t, d)
    for l in range(depth):
        a, b, ysg, sgf = _front(
            x2, norm1_g[l][None, :], w_in[l].astype(bf16), cs_tab,
            sgu_ln_g[l][None, :], sgu_ln_b[l][None, :], w_spatial[l].astype(bf16),
            b_spatial[l].T, w_sgu_out[l].astype(bf16), tm_front)
        fm_first, fm_second = _seqdft(seq_tables, a, b, nb, s, ts_dft, dft_scale)
        wr = jnp.pad(w_router[l], ((0, 0), (0, ROUTER_LANES - ne)))
        wr_hi = wr.astype(bf16)
        wr_lo = (wr - wr_hi.astype(f32)).astype(bf16)
        wr_split = jnp.concatenate([jnp.concatenate([wr_hi, wr_lo], axis=1),
                                    jnp.concatenate([wr_hi, jnp.zeros_like(wr_hi)], axis=1)], axis=0)
        x1, h2, aff = _post(fm_first, fm_second, sgf, ysg, x2, w_fourier_out[l].astype(bf16),
                            w_out[l].astype(bf16), norm2_g[l][None, :], wr_split, ne, s, tm_post)
        idx, gate = _select(aff, nb, s, cap)
        idx_flat = idx.reshape(-1)
        xe = _gather(idx_flat, h2, nb, s, ne, cap)
        gate_row = jnp.transpose(gate, (1, 0, 2)).reshape(ne, 1, nb * cap)
        ye = _experts(xe, w_gate_e[l], w_up_e[l], w_down_e[l], gate_row, tmo, tf, sub)
        x2 = _combine(idx_flat, ye, x1, final_g[None, :], nb, s, ne, cap, l == depth - 1)
    return x2.reshape(nb, s, d)
```

```python
import functools
import math

import jax
import jax.numpy as jnp
from jax import lax
from jax.experimental import pallas as pl
from jax.experimental.pallas import tpu as pltpu

EPS = 1e-6
GROUP = 128
N_GROUPS = 4
F_WIDTH = N_GROUPS * GROUP
S_WIDTH = N_GROUPS * GROUP
CAPACITY_FACTOR = 2
VMEM_LIMIT_V7X = 56 * 1024 * 1024

f32 = jnp.float32
bf16 = jnp.bfloat16


def _dot(a, b):
    return jnp.dot(a, b, preferred_element_type=f32)


def _params(sem, vmem=VMEM_LIMIT_V7X):
    return pltpu.CompilerParams(dimension_semantics=sem, vmem_limit_bytes=vmem)


FRONT_ROWS = 1024


def _front_kernel(x_ref, g1_ref, win_ref, cs_ref, lng_ref, lnb_ref, ws_ref, bs_ref, wso_ref,
                  a_ref, b_ref, ysg_ref, sgf_ref, mix_ref):
    tm, d = x_ref.shape
    c0 = F_WIDTH + 2 * S_WIDTH
    chunk = min(FRONT_ROWS, tm)
    for r0 in range(0, tm, chunk):
        rows = slice(r0, r0 + chunk)
        x = x_ref[rows, :]
        ms = jnp.mean(x * x, axis=-1, keepdims=True)
        h = (x * lax.rsqrt(ms + EPS) * g1_ref[...]).astype(bf16)

        def proj(lo, hi):
            return _dot(h, win_ref[:, lo:hi])

        zf = proj(0, F_WIDTH)
        for g in range(N_GROUPS):
            cols = slice(g * GROUP, (g + 1) * GROUP)
            ab = _dot(zf[:, cols].astype(bf16), cs_ref[...])
            a_ref[rows, cols] = ab[:, :GROUP].astype(bf16)
            b_ref[rows, cols] = ab[:, GROUP:].astype(bf16)

        u_pre = proj(F_WIDTH, F_WIDTH + S_WIDTH)
        v_pre = proj(F_WIDTH + S_WIDTH, F_WIDTH + 2 * S_WIDTH)
        sgf_ref[rows, :] = jax.nn.sigmoid(proj(c0, c0 + d)).astype(bf16)
        gs_pre = proj(c0 + d, c0 + 2 * d)
        u = jax.nn.gelu(u_pre)
        v = jax.nn.gelu(v_pre)
        for g in range(N_GROUPS):
            cols = slice(g * GROUP, (g + 1) * GROUP)
            vg = v[:, cols]
            mu = jnp.mean(vg, axis=-1, keepdims=True)
            dv = vg - mu
            var = jnp.mean(dv * dv, axis=-1, keepdims=True)
            vnb = (dv * lax.rsqrt(var + EPS) * lng_ref[:, cols] + lnb_ref[:, cols]).astype(bf16)
            for c in range(chunk // GROUP):
                m = _dot(ws_ref[g], vnb[c * GROUP:(c + 1) * GROUP, :]) + bs_ref[:, g:g + 1]
                mix_ref[r0 + c * GROUP:r0 + (c + 1) * GROUP, cols] = m
        sgu = (u * mix_ref[rows, :]).astype(bf16)
        ys = _dot(sgu, wso_ref[...])
        ysg_ref[rows, :] = (jax.nn.sigmoid(gs_pre) * ys).astype(bf16)


def _front(x2, g1, win_b, cs_b, lng, lnb, ws_b, bs_t, wso_b, tm):
    t, d = x2.shape
    kin = win_b.shape[1]
    const = lambda *shape: pl.BlockSpec(shape, lambda i: (0,) * len(shape))
    return pl.pallas_call(
        _front_kernel,
        grid=(t // tm,),
        in_specs=[
            pl.BlockSpec((tm, d), lambda i: (i, 0)),
            const(1, d),
            const(d, kin),
            const(GROUP, 2 * GROUP),
            const(1, S_WIDTH),
            const(1, S_WIDTH),
            const(N_GROUPS, GROUP, GROUP),
            const(GROUP, N_GROUPS),
            const(S_WIDTH, d),
        ],
        out_specs=[
            pl.BlockSpec((tm, F_WIDTH), lambda i: (i, 0)),
            pl.BlockSpec((tm, F_WIDTH), lambda i: (i, 0)),
            pl.BlockSpec((tm, d), lambda i: (i, 0)),
            pl.BlockSpec((tm, d), lambda i: (i, 0)),
        ],
        out_shape=[
            jax.ShapeDtypeStruct((t, F_WIDTH), bf16),
            jax.ShapeDtypeStruct((t, F_WIDTH), bf16),
            jax.ShapeDtypeStruct((t, d), bf16),
            jax.ShapeDtypeStruct((t, d), bf16),
        ],
        scratch_shapes=[pltpu.VMEM((tm, S_WIDTH), f32)],
        compiler_params=_params(("parallel",)),
        name="front",
    )(x2, g1, win_b, cs_b, lng, lnb, ws_b, bs_t, wso_b)


FOLD_ROWS = 256


def _seqdft_kernel(cp_ref, sm_ref, flip_ref, altrow_ref, j1_ref, a_ref, b_ref, fmd_ref, fmm_ref,
                   ap_ref, bm_ref, edge_ref, *, scale):
    i = pl.program_id(1)
    ts = fmd_ref.shape[0]
    s = a_ref.shape[0]
    half = s // 2
    a_mid = scale * a_ref[half:half + 1, :].astype(f32)

    @pl.when(i == 0)
    def _fold_inputs():
        fold = j1_ref.shape[0]
        first_row = lax.broadcasted_iota(jnp.int32, (fold, 1), 0) == 0
        for j in range(half // fold):
            lo = slice(j * fold, (j + 1) * fold)
            mir = slice(s - (j + 1) * fold, s - j * fold)
            a_m = _dot(j1_ref[...], a_ref[mir, :])
            b_m = _dot(j1_ref[...], b_ref[mir, :])
            if j > 0:
                a_m = jnp.where(first_row, a_ref[mir.stop:mir.stop + 1, :].astype(f32), a_m)
                b_m = jnp.where(first_row, b_ref[mir.stop:mir.stop + 1, :].astype(f32), b_m)
            ap_ref[lo, :] = (a_ref[lo, :].astype(f32) + a_m).astype(bf16)
            bm_ref[lo, :] = (b_ref[lo, :].astype(f32) - b_m).astype(bf16)
        first = lax.broadcasted_iota(jnp.int32, (8, 1), 0) == 0
        edge_ref[...] = _dot(altrow_ref[...], ap_ref[...]) + jnp.where(first, a_mid, 0.0)

    p = _dot(cp_ref[...], ap_ref[...])
    q = _dot(sm_ref[...], bm_ref[...])
    row = lax.broadcasted_iota(jnp.int32, (ts, 1), 0)
    alt = (1 - 2 * (row & 1)).astype(f32) * a_mid
    fmd_ref[...] = (p + q + alt).astype(bf16)
    g = p - q + alt
    flipped = _dot(flip_ref[...], g.astype(bf16))
    fmm_ref[...] = jnp.where(row == 0, edge_ref[0:1, :], flipped).astype(bf16)
    edge_ref[...] = g[0:8, :]


def _seqdft(tables, a, b, nb, s, ts, scale):
    half = s // 2
    nh = half // ts
    cp, sm, flip, altrow, j1 = tables
    const = lambda *shape: pl.BlockSpec(shape, lambda bi, i: (0,) * len(shape))
    table_tile = pl.BlockSpec((ts, half), lambda bi, i: (nh - 1 - i, 0))
    return pl.pallas_call(
        functools.partial(_seqdft_kernel, scale=scale),
        grid=(nb, nh),
        in_specs=[
            table_tile,
            table_tile,
            const(ts, ts),
            const(8, half),
            const(*j1.shape),
            pl.BlockSpec((s, F_WIDTH), lambda bi, i: (bi, 0)),
            pl.BlockSpec((s, F_WIDTH), lambda bi, i: (bi, 0)),
        ],
        out_specs=[
            pl.BlockSpec((ts, F_WIDTH), lambda bi, i: (bi * nh + nh - 1 - i, 0)),
            pl.BlockSpec((ts, F_WIDTH), lambda bi, i: (bi * nh + i, 0)),
        ],
        out_shape=[
            jax.ShapeDtypeStruct((nb * half, F_WIDTH), bf16),
            jax.ShapeDtypeStruct((nb * half, F_WIDTH), bf16),
        ],
        scratch_shapes=[
            pltpu.VMEM((half, F_WIDTH), bf16),
            pltpu.VMEM((half, F_WIDTH), bf16),
            pltpu.VMEM((8, F_WIDTH), f32),
        ],
        compiler_params=_params(("parallel", "arbitrary")),
        name="seqdft",
    )(cp, sm, flip, altrow, j1, a, b)


ROUTER_LANES = 128
POST_ROWS = 256


def _post_kernel(fmd_ref, fmm_ref, sgf_ref, ysg_ref, x_ref, wfo_ref, wout_ref, g2_ref, wr_ref,
                 x1_ref, h2_ref, aff_ref, *, tiles_per_half):
    ne = aff_ref.shape[0]
    tm = x_ref.shape[0]
    in_first_half = (pl.program_id(0) % (2 * tiles_per_half)) < tiles_per_half
    chunk = min(POST_ROWS, tm)
    for r0 in range(0, tm, chunk):
        rows = slice(r0, r0 + chunk)
        fm = jnp.where(in_first_half, fmd_ref[rows, :], fmm_ref[rows, :])
        yf = _dot(fm, wfo_ref[...])
        merged = sgf_ref[rows, :].astype(f32) * yf + ysg_ref[rows, :].astype(f32)
        x1 = x_ref[rows, :] + _dot(merged.astype(bf16), wout_ref[...])
        x1_ref[rows, :] = x1
        ms = jnp.mean(x1 * x1, axis=-1, keepdims=True)
        h2 = x1 * lax.rsqrt(ms + EPS) * g2_ref[...]
        h2_ref[rows, :] = h2
        h_hi = h2.astype(bf16)
        h_lo = (h2 - h_hi.astype(f32)).astype(bf16)
        both = _dot(jnp.concatenate([h_hi, h_lo], axis=1), wr_ref[...])
        logits = both[:, :ROUTER_LANES] + both[:, ROUTER_LANES:]
        logits = logits.T[:ne, :]
        mx = jnp.max(logits, axis=0, keepdims=True)
        ex = jnp.exp(logits - mx)
        aff_ref[:, rows] = ex / jnp.sum(ex, axis=0, keepdims=True)


def _post(fm_first, fm_second, sgf, ysg, x2, wfo_b, wout_b, g2, wr_split, ne, s, tm):
    t, d = x2.shape
    th = s // 2 // tm
    const = lambda *shape: pl.BlockSpec(shape, lambda i: (0,) * len(shape))
    tile = lambda w: pl.BlockSpec((tm, w), lambda i: (i, 0))
    first = pl.BlockSpec((tm, F_WIDTH), lambda i: (i // (2 * th) * th + jnp.minimum(i % (2 * th), th - 1), 0))
    second = pl.BlockSpec((tm, F_WIDTH), lambda i: (i // (2 * th) * th + jnp.maximum(i % (2 * th) - th, 0), 0))
    return pl.pallas_call(
        functools.partial(_post_kernel, tiles_per_half=th),
        grid=(t // tm,),
        in_specs=[
            first, second, tile(d), tile(d), tile(d),
            const(F_WIDTH, d),
            const(d, d),
            const(1, d),
            const(2 * d, 2 * ROUTER_LANES),
        ],
        out_specs=[tile(d), tile(d), pl.BlockSpec((ne, tm), lambda i: (0, i))],
        out_shape=[
            jax.ShapeDtypeStruct((t, d), f32),
            jax.ShapeDtypeStruct((t, d), f32),
            jax.ShapeDtypeStruct((ne, t), f32),
        ],
        compiler_params=_params(("parallel",)),
        name="post",
    )(fm_first, fm_second, sgf, ysg, x2, wfo_b, wout_b, g2, wr_split)


def _select_kernel(aff_ref, idx_ref, gate_ref, *, cap):
    ne, s = aff_ref.shape
    nbits = s.bit_length() - 1
    aff = aff_ref[...]
    lane = lax.broadcasted_iota(jnp.int32, (ne, s), 1)

    def enough(cand):
        cnt = jnp.sum((aff >= pltpu.bitcast(cand, f32)).astype(jnp.int32), axis=1, keepdims=True)
        return cnt >= cap

    def two_bit_step(i, thr):
        hi = jnp.left_shift(jnp.int32(1), 29 - 2 * i)
        lo = jnp.left_shift(jnp.int32(1), 28 - 2 * i)
        c1, c2, c3 = thr | lo, thr | hi, thr | hi | lo
        return jnp.where(enough(c3), c3, jnp.where(enough(c2), c2, jnp.where(enough(c1), c1, thr)))

    thr = lax.fori_loop(0, 15, two_bit_step, jnp.zeros((ne, 1), jnp.int32))
    gt = aff >= pltpu.bitcast(thr + 1, f32)
    eq = jnp.logical_and(aff >= pltpu.bitcast(thr, f32), jnp.logical_not(gt))
    need = cap - jnp.sum(gt.astype(jnp.int32), axis=1, keepdims=True)

    def cumsum_tokens(v):
        k = 1
        while k < s:
            v = v + jnp.where(lane >= k, pltpu.roll(v, k, axis=1), 0)
            k *= 2
        return v

    eq_i = eq.astype(jnp.int32)
    surplus = jnp.max(jnp.sum(eq_i, axis=1, keepdims=True) - need) > 0
    keep = lax.cond(surplus, lambda: (cumsum_tokens(eq_i) <= need).astype(jnp.int32), lambda: eq_i)
    sel = jnp.logical_or(gt, jnp.logical_and(eq, keep == 1))
    slot = cumsum_tokens(sel.astype(jnp.int32)) - 1
    word = jnp.where(sel, lane | jnp.left_shift(lane - slot, nbits) | (1 << (2 * nbits)), 0)
    gate = aff
    for k in range(nbits):
        step = 1 << k
        inc_word = pltpu.roll(word, s - step, axis=1)
        inc_gate = pltpu.roll(gate, s - step, axis=1)
        take = (jnp.right_shift(inc_word, nbits + k) & 1) == 1
        leave = (jnp.right_shift(word, nbits + k) & 1) == 1
        word = jnp.where(take, inc_word, jnp.where(leave, 0, word))
        gate = jnp.where(take, inc_gate, gate)
    idx_ref[0] = word[:, :cap] & (s - 1)
    gate_ref[0] = gate[:, :cap]


def _select(aff_t, nb, s, cap):
    ne = aff_t.shape[0]
    return pl.pallas_call(
        functools.partial(_select_kernel, cap=cap),
        grid=(nb,),
        in_specs=[pl.BlockSpec((ne, s), lambda bi: (0, bi))],
        out_specs=[
            pl.BlockSpec((1, ne, cap), lambda bi: (bi, 0, 0)),
            pl.BlockSpec((1, ne, cap), lambda bi: (bi, 0, 0)),
        ],
        out_shape=[
            jax.ShapeDtypeStruct((nb, ne, cap), jnp.int32),
            jax.ShapeDtypeStruct((nb, ne, cap), f32),
        ],
        compiler_params=_params(("parallel",)),
        name="select",
    )(aff_t)


GATHER_ROWS = 64
SCATTER_ROWS = 16


def _slab_pitch(rows):
    pitch = rows + (-rows) % 4
    return pitch if (pitch // 4) % 2 else pitch + 4


def _gather_kernel(idx_ref, h2_hbm, xe_ref, buf_ref, stage_ref, sem, *, ne, cap, s):
    b = pl.program_id(0)
    e = pl.program_id(1)
    nslab = xe_ref.shape[2] // GROUP
    pitch = buf_ref.shape[1] // nslab
    spitch = stage_ref.shape[0] // nslab
    slot = b % 2
    base = (b * ne + e) * cap

    def slab_copy(batch, into, c):
        rows = pl.ds(pl.multiple_of(batch * s, s), s)
        return pltpu.make_async_copy(h2_hbm.at[rows, c * GROUP:(c + 1) * GROUP],
                                     buf_ref.at[into, c * pitch:c * pitch + s, :], sem.at[into])

    @pl.when(jnp.logical_and(b == 0, e == 0))
    def _():
        for c in range(nslab):
            slab_copy(0, 0, c).start()

    @pl.when(e == 0)
    def _():
        @pl.when(b + 1 < pl.num_programs(0))
        def _():
            for c in range(nslab):
                slab_copy(b + 1, 1 - slot, c).start()

        for c in range(nslab):
            slab_copy(b, slot, c).wait()

    def body(i, _):
        r0 = pl.multiple_of(i * GATHER_ROWS, GATHER_ROWS)
        for j in range(GATHER_ROWS):
            tok = idx_ref[base + r0 + j]
            stage_ref[pl.ds(j, nslab, stride=spitch), :] = buf_ref[slot, pl.ds(tok, nslab, stride=pitch), :]
        for c in range(nslab):
            xe_ref[0, pl.ds(r0, GATHER_ROWS), c * GROUP:(c + 1) * GROUP] = \
                stage_ref[c * spitch:c * spitch + GATHER_ROWS, :].astype(bf16)
        return 0

    lax.fori_loop(0, cap // GATHER_ROWS, body, 0)


def _gather(idx_flat, h2, nb, s, ne, cap):
    d = h2.shape[1]
    nslab = d // GROUP
    return pl.pallas_call(
        functools.partial(_gather_kernel, ne=ne, cap=cap, s=s),
        grid_spec=pltpu.PrefetchScalarGridSpec(
            num_scalar_prefetch=1,
            grid=(nb, ne),
            in_specs=[pl.BlockSpec(memory_space=pl.ANY)],
            out_specs=pl.BlockSpec((1, cap, d), lambda bi, e, idx: (e, bi, 0)),
            scratch_shapes=[
                pltpu.VMEM((2, nslab * _slab_pitch(s), GROUP), f32),
                pltpu.VMEM((nslab * _slab_pitch(GATHER_ROWS), GROUP), f32),
                pltpu.SemaphoreType.DMA((2,)),
            ],
        ),
        out_shape=jax.ShapeDtypeStruct((ne, nb * cap, d), bf16),
        compiler_params=_params(("arbitrary", "arbitrary")),
        name="gather",
    )(idx_flat, h2)


def _expert_kernel(x_ref, wg_ref, wu_ref, wd_ref, gate_ref, y_ref, acc_ref, *, sub, nf):
    f = pl.program_id(2)
    tmo, d = acc_ref.shape

    def partial_out(rows, w_gate, w_up, w_down):
        xs = x_ref[0, rows, :]
        a = _dot(xs, w_gate)
        g = _dot(xs, w_up)
        hm = (a * jax.nn.sigmoid(a) * g).astype(bf16)
        return _dot(hm, w_down)

    def store_gated(rows, y):
        for c in range(rows.start // GROUP, rows.stop // GROUP):
            chunk = slice(c * GROUP, (c + 1) * GROUP)
            g_col = jnp.broadcast_to(gate_ref[0, :, chunk], (GROUP, GROUP)).T
            y_c = y[chunk.start - rows.start:chunk.stop - rows.start, :]
            y_ref[0, chunk, :] = (y_c * jnp.tile(g_col, (1, d // GROUP))).astype(bf16)

    def run(first, last):
        weights = [w_ref[0].astype(bf16) for w_ref in (wg_ref, wu_ref, wd_ref)]
        for r0 in range(0, tmo, sub):
            rows = slice(r0, r0 + sub)
            y = partial_out(rows, *weights)
            if not first:
                y = acc_ref[rows, :] + y
            if last:
                store_gated(rows, y)
            else:
                acc_ref[rows, :] = y

    if nf == 1:
        run(True, True)
    else:
        pl.when(f == 0)(lambda: run(True, False))
        pl.when(f == nf - 1)(lambda: run(False, True))
        if nf > 2:
            pl.when(jnp.logical_and(f > 0, f < nf - 1))(lambda: run(False, False))


def _experts(xe, wg, wu, wd, gate_row, tmo, tf, sub):
    ne, m, d = xe.shape
    ff = wg.shape[2]
    return pl.pallas_call(
        functools.partial(_expert_kernel, sub=sub, nf=ff // tf),
        grid=(ne, m // tmo, ff // tf),
        in_specs=[
            pl.BlockSpec((1, tmo, d), lambda e, mi, fi: (e, mi, 0)),
            pl.BlockSpec((1, d, tf), lambda e, mi, fi: (e, 0, fi)),
            pl.BlockSpec((1, d, tf), lambda e, mi, fi: (e, 0, fi)),
            pl.BlockSpec((1, tf, d), lambda e, mi, fi: (e, fi, 0)),
            pl.BlockSpec((1, 1, tmo), lambda e, mi, fi: (e, 0, mi)),
        ],
        out_specs=pl.BlockSpec((1, tmo, d), lambda e, mi, fi: (e, mi, 0)),
        out_shape=jax.ShapeDtypeStruct((ne, m, d), bf16),
        scratch_shapes=[pltpu.VMEM((tmo, d), f32)],
        compiler_params=_params(("parallel", "parallel", "arbitrary")),
        name="experts",
    )(xe, wg, wu, wd, gate_row)


NORM_ROWS = 256


def _combine_kernel(idx_ref, ye_ref, x1_hbm, g_ref, out_hbm, acc_ref, ys_ref, x1_ref, stage_ref, sem_x1, sem_out,
                    *, ne, cap, s, do_norm):
    b = pl.program_id(0)
    e = pl.program_id(1)
    d = x1_ref.shape[1]
    nslab = d // GROUP
    pitch = acc_ref.shape[0] // nslab
    ypitch = ys_ref.shape[0] // nslab
    base = (b * ne + e) * cap
    batch_row0 = pl.multiple_of(b * s, s)
    x1_copy = pltpu.make_async_copy(x1_hbm.at[pl.ds(batch_row0, s), :], x1_ref, sem_x1)

    @pl.when(e == 0)
    def _():
        x1_copy.start()
        acc_ref[...] = jnp.zeros_like(acc_ref)

    for c in range(nslab):
        ys_ref[c * ypitch:c * ypitch + cap, :] = ye_ref[0, :, c * GROUP:(c + 1) * GROUP].astype(f32)

    def body(i, _):
        r0 = pl.multiple_of(i * SCATTER_ROWS, SCATTER_ROWS)
        toks = [idx_ref[base + r0 + j] for j in range(SCATTER_ROWS)]
        rows = [acc_ref[pl.ds(tok, nslab, stride=pitch), :] for tok in toks]
        for j in range(SCATTER_ROWS):
            acc_ref[pl.ds(toks[j], nslab, stride=pitch), :] = \
                rows[j] + ys_ref[pl.ds(r0 + j, nslab, stride=ypitch), :]
        return 0

    lax.fori_loop(0, cap // SCATTER_ROWS, body, 0)

    @pl.when(e == ne - 1)
    def _():
        x1_copy.wait()
        nsteps = s // NORM_ROWS

        def out_copy(step):
            rows = pl.ds(batch_row0 + step * NORM_ROWS, NORM_ROWS)
            return pltpu.make_async_copy(stage_ref.at[step % 2], out_hbm.at[rows, :], sem_out.at[step % 2])

        for step in range(nsteps):
            lo = step * NORM_ROWS
            moe = jnp.concatenate([acc_ref[c * pitch + lo:c * pitch + lo + NORM_ROWS, :] for c in range(nslab)],
                                  axis=1)
            y = x1_ref[lo:lo + NORM_ROWS, :] + moe
            if do_norm:
                ms = jnp.mean(y * y, axis=-1, keepdims=True)
                y = y * lax.rsqrt(ms + EPS) * g_ref[...]
            if step >= 2:
                out_copy(step - 2).wait()
            stage_ref[step % 2] = y
            out_copy(step).start()
        for step in range(max(nsteps - 2, 0), nsteps):
            out_copy(step).wait()


def _combine(idx_flat, ye, x1, g, nb, s, ne, cap, do_norm):
    d = ye.shape[2]
    nslab = d // GROUP
    return pl.pallas_call(
        functools.partial(_combine_kernel, ne=ne, cap=cap, s=s, do_norm=do_norm),
        grid_spec=pltpu.PrefetchScalarGridSpec(
            num_scalar_prefetch=1,
            grid=(nb, ne),
            in_specs=[
                pl.BlockSpec((1, cap, d), lambda bi, e, idx: (e, bi, 0)),
                pl.BlockSpec(memory_space=pl.ANY),
                pl.BlockSpec((1, d), lambda bi, e, idx: (0, 0)),
            ],
            out_specs=pl.BlockSpec(memory_space=pl.ANY),
            scratch_shapes=[
                pltpu.VMEM((nslab * _slab_pitch(s), GROUP), f32),
                pltpu.VMEM((nslab * _slab_pitch(cap), GROUP), f32),
                pltpu.VMEM((s, d), f32),
                pltpu.VMEM((2, NORM_ROWS, d), f32),
                pltpu.SemaphoreType.DMA,
                pltpu.SemaphoreType.DMA((2,)),
            ],
        ),
        out_shape=jax.ShapeDtypeStruct((nb * s, d), f32),
        compiler_params=_params(("arbitrary", "arbitrary")),
        name="combine",
    )(idx_flat, ye, x1, g)


def _channel_dft_table():
    k = jnp.arange(GROUP, dtype=jnp.int32)
    ang = ((k[:, None] * k[None, :]) % GROUP).astype(f32) * (2.0 * math.pi / GROUP)
    return jnp.concatenate([jnp.cos(ang), jnp.sin(ang)], axis=1).astype(bf16)


def _sequence_dft_tables(s, ts, scale):
    half = s // 2
    inner = 64
    k = jnp.arange(half, dtype=jnp.int32)[None, :]
    n1 = jnp.arange(half // inner, dtype=jnp.int32)[:, None] * inner
    n2 = jnp.arange(inner, dtype=jnp.int32)[:, None]
    w = 2.0 * math.pi / s
    a1 = ((n1 * k) % s).astype(f32) * w
    a2 = ((n2 * k) % s).astype(f32) * w
    c1, s1 = jnp.cos(a1)[:, None, :], jnp.sin(a1)[:, None, :]
    c2, s2 = jnp.cos(a2)[None, :, :] * scale, jnp.sin(a2)[None, :, :] * scale
    cp = (c1 * c2 - s1 * s2).reshape(half, half).astype(bf16)
    sm = (-(s1 * c2 + c1 * s2)).reshape(half, half).astype(bf16)
    rows = jnp.arange(ts, dtype=jnp.int32)[:, None]
    cols = jnp.arange(ts, dtype=jnp.int32)[None, :]
    flip = (cols == ts - rows).astype(bf16)
    kk = jnp.arange(half, dtype=jnp.int32)[None, :]
    alt = jnp.where(jnp.arange(8)[:, None] == 0, (1 - 2 * (kk & 1)).astype(f32) * scale, 0.0).astype(bf16)
    fold = min(FOLD_ROWS, half)
    r = jnp.arange(fold, dtype=jnp.int32)
    j1 = ((r[None, :] == fold - r[:, None]) & (r[:, None] >= 1)).astype(bf16)
    return cp, sm, flip, alt, j1


def _pick(n, pref):
    return pref if n % pref == 0 else n


def kernel(x, norm1_g, w_in, sgu_ln_g, sgu_ln_b, w_spatial, b_spatial, w_fourier_out, w_sgu_out, w_out,
           norm2_g, w_router, w_gate_e, w_up_e, w_down_e, final_g):
    nb, s, d = x.shape
    depth = norm1_g.shape[0]
    ne = w_router.shape[2]
    ff = w_gate_e.shape[3]
    cap = CAPACITY_FACTOR * s // ne
    t = nb * s
    assert s % NORM_ROWS == 0 and s & (s - 1) == 0 and cap % GATHER_ROWS == 0 and d % GROUP == 0
    assert (nb * cap) % GROUP == 0

    tm_front = _pick(s, 1024)
    ts_dft = _pick(s // 2, 512)
    tm_post = _pick(s // 2, 1024)
    dft_scale = 1.0 / math.sqrt(s * GROUP)
    cs_tab = _channel_dft_table()
    seq_tables = _sequence_dft_tables(s, ts_dft, dft_scale)
    tmo = _pick(nb * cap, 2048)
    tf = _pick(ff, 512)
    sub = _pick(tmo, 1024)

    x2 = x.reshape(t, d)
    for l in range(depth):
        a, b, ysg, sgf = _front(
            x2, norm1_g[l][None, :], w_in[l].astype(bf16), cs_tab,
            sgu_ln_g[l][None, :], sgu_ln_b[l][None, :], w_spatial[l].astype(bf16),
            b_spatial[l].T, w_sgu_out[l].astype(bf16), tm_front)
        fm_first, fm_second = _seqdft(seq_tables, a, b, nb, s, ts_dft, dft_scale)
        wr = jnp.pad(w_router[l], ((0, 0), (0, ROUTER_LANES - ne)))
        wr_hi = wr.astype(bf16)
        wr_lo = (wr - wr_hi.astype(f32)).astype(bf16)
        wr_split = jnp.concatenate([jnp.concatenate([wr_hi, wr_lo], axis=1),
                                    jnp.concatenate([wr_hi, jnp.zeros_like(wr_hi)], axis=1)], axis=0)
        x1, h2, aff = _post(fm_first, fm_second, sgf, ysg, x2, w_fourier_out[l].astype(bf16),
                            w_out[l].astype(bf16), norm2_g[l][None, :], wr_split, ne, s, tm_post)
        idx, gate = _select(aff, nb, s, cap)
        idx_flat = idx.reshape(-1)
        xe = _gather(idx_flat, h2, nb, s, ne, cap)
        gate_row = jnp.transpose(gate, (1, 0, 2)).reshape(ne, 1, nb * cap)
        ye = _experts(xe, w_gate_e[l], w_up_e[l], w_down_e[l], gate_row, tmo, tf, sub)
        x2 = _combine(idx_flat, ye, x1, final_g[None, :], nb, s, ne, cap, l == depth - 1)
    return x2.reshape(nb, s, d)
```

```python
import functools
import math

import jax
import jax.numpy as jnp
from jax import lax
from jax.experimental import pallas as pl
from jax.experimental.pallas import tpu as pltpu

EPS = 1e-6
GROUP = 128
N_GROUPS = 4
F_WIDTH = N_GROUPS * GROUP
S_WIDTH = N_GROUPS * GROUP
CAPACITY_FACTOR = 2
VMEM_LIMIT_V7X = 56 * 1024 * 1024

f32 = jnp.float32
bf16 = jnp.bfloat16


def _dot(a, b):
    return jnp.dot(a, b, preferred_element_type=f32)


def _params(sem, vmem=VMEM_LIMIT_V7X):
    return pltpu.CompilerParams(dimension_semantics=sem, vmem_limit_bytes=vmem)


FRONT_ROWS = 1024


def _front_kernel(x_ref, g1_ref, win_ref, cs_ref, lng_ref, lnb_ref, ws_ref, bs_ref, wso_ref,
                  a_ref, b_ref, ysg_ref, sgf_ref, mix_ref):
    tm, d = x_ref.shape
    c0 = F_WIDTH + 2 * S_WIDTH
    chunk = min(FRONT_ROWS, tm)
    for r0 in range(0, tm, chunk):
        rows = slice(r0, r0 + chunk)
        x = x_ref[rows, :]
        ms = jnp.mean(x * x, axis=-1, keepdims=True)
        h = (x * lax.rsqrt(ms + EPS) * g1_ref[...]).astype(bf16)

        def proj(lo, hi):
            return _dot(h, win_ref[:, lo:hi])

        zf = proj(0, F_WIDTH)
        for g in range(N_GROUPS):
            cols = slice(g * GROUP, (g + 1) * GROUP)
            ab = _dot(zf[:, cols].astype(bf16), cs_ref[...])
            a_ref[rows, cols] = ab[:, :GROUP].astype(bf16)
            b_ref[rows, cols] = ab[:, GROUP:].astype(bf16)

        u_pre = proj(F_WIDTH, F_WIDTH + S_WIDTH)
        v_pre = proj(F_WIDTH + S_WIDTH, F_WIDTH + 2 * S_WIDTH)
        sgf_ref[rows, :] = jax.nn.sigmoid(proj(c0, c0 + d)).astype(bf16)
        gs_pre = proj(c0 + d, c0 + 2 * d)
        u = jax.nn.gelu(u_pre)
        v = jax.nn.gelu(v_pre)
        for g in range(N_GROUPS):
            cols = slice(g * GROUP, (g + 1) * GROUP)
            vg = v[:, cols]
            mu = jnp.mean(vg, axis=-1, keepdims=True)
            dv = vg - mu
            var = jnp.mean(dv * dv, axis=-1, keepdims=True)
            vnb = (dv * lax.rsqrt(var + EPS) * lng_ref[:, cols] + lnb_ref[:, cols]).astype(bf16)
            for c in range(chunk // GROUP):
                m = _dot(ws_ref[g], vnb[c * GROUP:(c + 1) * GROUP, :]) + bs_ref[:, g:g + 1]
                mix_ref[r0 + c * GROUP:r0 + (c + 1) * GROUP, cols] = m
        sgu = (u * mix_ref[rows, :]).astype(bf16)
        ys = _dot(sgu, wso_ref[...])
        ysg_ref[rows, :] = (jax.nn.sigmoid(gs_pre) * ys).astype(bf16)


def _front(x2, g1, win_b, cs_b, lng, lnb, ws_b, bs_t, wso_b, tm):
    t, d = x2.shape
    kin = win_b.shape[1]
    const = lambda *shape: pl.BlockSpec(shape, lambda i: (0,) * len(shape))
    return pl.pallas_call(
        _front_kernel,
        grid=(t // tm,),
        in_specs=[
            pl.BlockSpec((tm, d), lambda i: (i, 0)),
            const(1, d),
            const(d, kin),
            const(GROUP, 2 * GROUP),
            const(1, S_WIDTH),
            const(1, S_WIDTH),
            const(N_GROUPS, GROUP, GROUP),
            const(GROUP, N_GROUPS),
            const(S_WIDTH, d),
        ],
        out_specs=[
            pl.BlockSpec((tm, F_WIDTH), lambda i: (i, 0)),
            pl.BlockSpec((tm, F_WIDTH), lambda i: (i, 0)),
            pl.BlockSpec((tm, d), lambda i: (i, 0)),
            pl.BlockSpec((tm, d), lambda i: (i, 0)),
        ],
        out_shape=[
            jax.ShapeDtypeStruct((t, F_WIDTH), bf16),
            jax.ShapeDtypeStruct((t, F_WIDTH), bf16),
            jax.ShapeDtypeStruct((t, d), bf16),
            jax.ShapeDtypeStruct((t, d), bf16),
        ],
        scratch_shapes=[pltpu.VMEM((tm, S_WIDTH), f32)],
        compiler_params=_params(("parallel",)),
        name="front",
    )(x2, g1, win_b, cs_b, lng, lnb, ws_b, bs_t, wso_b)


FOLD_ROWS = 256


def _seqdft_kernel(cp_ref, sm_ref, flip_ref, altrow_ref, j1_ref, a_ref, b_ref, fmd_ref, fmm_ref,
                   ap_ref, bm_ref, edge_ref, *, scale):
    i = pl.program_id(1)
    ts = fmd_ref.shape[0]
    s = a_ref.shape[0]
    half = s // 2
    a_mid = scale * a_ref[half:half + 1, :].astype(f32)

    @pl.when(i == 0)
    def _fold_inputs():
        fold = j1_ref.shape[0]
        first_row = lax.broadcasted_iota(jnp.int32, (fold, 1), 0) == 0
        for j in range(half // fold):
            lo = slice(j * fold, (j + 1) * fold)
            mir = slice(s - (j + 1) * fold, s - j * fold)
            a_m = _dot(j1_ref[...], a_ref[mir, :])
            b_m = _dot(j1_ref[...], b_ref[mir, :])
            if j > 0:
                a_m = jnp.where(first_row, a_ref[mir.stop:mir.stop + 1, :].astype(f32), a_m)
                b_m = jnp.where(first_row, b_ref[mir.stop:mir.stop + 1, :].astype(f32), b_m)
            ap_ref[lo, :] = (a_ref[lo, :].astype(f32) + a_m).astype(bf16)
            bm_ref[lo, :] = (b_ref[lo, :].astype(f32) - b_m).astype(bf16)
        first = lax.broadcasted_iota(jnp.int32, (8, 1), 0) == 0
        edge_ref[...] = _dot(altrow_ref[...], ap_ref[...]) + jnp.where(first, a_mid, 0.0)

    p = _dot(cp_ref[...], ap_ref[...])
    q = _dot(sm_ref[...], bm_ref[...])
    row = lax.broadcasted_iota(jnp.int32, (ts, 1), 0)
    alt = (1 - 2 * (row & 1)).astype(f32) * a_mid
    fmd_ref[...] = (p + q + alt).astype(bf16)
    g = p - q + alt
    flipped = _dot(flip_ref[...], g.astype(bf16))
    fmm_ref[...] = jnp.where(row == 0, edge_ref[0:1, :], flipped).astype(bf16)
    edge_ref[...] = g[0:8, :]


def _seqdft(tables, a, b, nb, s, ts, scale):
    half = s // 2
    nh = half // ts
    cp, sm, flip, altrow, j1 = tables
    const = lambda *shape: pl.BlockSpec(shape, lambda bi, i: (0,) * len(shape))
    table_tile = pl.BlockSpec((ts, half), lambda bi, i: (nh - 1 - i, 0))
    return pl.pallas_call(
        functools.partial(_seqdft_kernel, scale=scale),
        grid=(nb, nh),
        in_specs=[
            table_tile,
            table_tile,
            const(ts, ts),
            const(8, half),
            const(*j1.shape),
            pl.BlockSpec((s, F_WIDTH), lambda bi, i: (bi, 0)),
            pl.BlockSpec((s, F_WIDTH), lambda bi, i: (bi, 0)),
        ],
        out_specs=[
            pl.BlockSpec((ts, F_WIDTH), lambda bi, i: (bi * nh + nh - 1 - i, 0)),
            pl.BlockSpec((ts, F_WIDTH), lambda bi, i: (bi * nh + i, 0)),
        ],
        out_shape=[
            jax.ShapeDtypeStruct((nb * half, F_WIDTH), bf16),
            jax.ShapeDtypeStruct((nb * half, F_WIDTH), bf16),
        ],
        scratch_shapes=[
            pltpu.VMEM((half, F_WIDTH), bf16),
            pltpu.VMEM((half, F_WIDTH), bf16),
            pltpu.VMEM((8, F_WIDTH), f32),
        ],
        compiler_params=_params(("parallel", "arbitrary")),
        name="seqdft",
    )(cp, sm, flip, altrow, j1, a, b)


ROUTER_LANES = 128
POST_ROWS = 256


def _post_kernel(fmd_ref, fmm_ref, sgf_ref, ysg_ref, x_ref, wfo_ref, wout_ref, g2_ref, wr_ref,
                 x1_ref, h2_ref, aff_ref, *, tiles_per_half):
    ne = aff_ref.shape[0]
    tm = x_ref.shape[0]
    in_first_half = (pl.program_id(0) % (2 * tiles_per_half)) < tiles_per_half
    chunk = min(POST_ROWS, tm)
    for r0 in range(0, tm, chunk):
        rows = slice(r0, r0 + chunk)
        fm = jnp.where(in_first_half, fmd_ref[rows, :], fmm_ref[rows, :])
        yf = _dot(fm, wfo_ref[...])
        merged = sgf_ref[rows, :].astype(f32) * yf + ysg_ref[rows, :].astype(f32)
        x1 = x_ref[rows, :] + _dot(merged.astype(bf16), wout_ref[...])
        x1_ref[rows, :] = x1
        ms = jnp.mean(x1 * x1, axis=-1, keepdims=True)
        h2 = x1 * lax.rsqrt(ms + EPS) * g2_ref[...]
        for c in range(h2_ref.shape[0]):
            h2_ref[c, rows, :] = h2[:, c * GROUP:(c + 1) * GROUP]
        h_hi = h2.astype(bf16)
        h_lo = (h2 - h_hi.astype(f32)).astype(bf16)
        both = _dot(jnp.concatenate([h_hi, h_lo], axis=1), wr_ref[...])
        logits = both[:, :ROUTER_LANES] + both[:, ROUTER_LANES:]
        logits = logits.T[:ne, :]
        mx = jnp.max(logits, axis=0, keepdims=True)
        ex = jnp.exp(logits - mx)
        aff_ref[:, rows] = ex / jnp.sum(ex, axis=0, keepdims=True)


def _post(fm_first, fm_second, sgf, ysg, x2, wfo_b, wout_b, g2, wr_split, ne, s, tm):
    t, d = x2.shape
    th = s // 2 // tm
    const = lambda *shape: pl.BlockSpec(shape, lambda i: (0,) * len(shape))
    tile = lambda w: pl.BlockSpec((tm, w), lambda i: (i, 0))
    first = pl.BlockSpec((tm, F_WIDTH), lambda i: (i // (2 * th) * th + jnp.minimum(i % (2 * th), th - 1), 0))
    second = pl.BlockSpec((tm, F_WIDTH), lambda i: (i // (2 * th) * th + jnp.maximum(i % (2 * th) - th, 0), 0))
    return pl.pallas_call(
        functools.partial(_post_kernel, tiles_per_half=th),
        grid=(t // tm,),
        in_specs=[
            first, second, tile(d), tile(d), tile(d),
            const(F_WIDTH, d),
            const(d, d),
            const(1, d),
            const(2 * d, 2 * ROUTER_LANES),
        ],
        out_specs=[tile(d), pl.BlockSpec((d // GROUP, tm, GROUP), lambda i: (0, i, 0)),
                   pl.BlockSpec((ne, tm), lambda i: (0, i))],
        out_shape=[
            jax.ShapeDtypeStruct((t, d), f32),
            jax.ShapeDtypeStruct((d // GROUP, t, GROUP), f32),
            jax.ShapeDtypeStruct((ne, t), f32),
        ],
        compiler_params=_params(("parallel",)),
        name="post",
    )(fm_first, fm_second, sgf, ysg, x2, wfo_b, wout_b, g2, wr_split)


def _select_kernel(aff_ref, idx_ref, gate_ref, *, cap):
    ne, s = aff_ref.shape
    nbits = s.bit_length() - 1
    aff = aff_ref[...]
    lane = lax.broadcasted_iota(jnp.int32, (ne, s), 1)

    def enough(cand):
        cnt = jnp.sum((aff >= pltpu.bitcast(cand, f32)).astype(jnp.int32), axis=1, keepdims=True)
        return cnt >= cap

    def two_bit_step(i, thr):
        hi = jnp.left_shift(jnp.int32(1), 29 - 2 * i)
        lo = jnp.left_shift(jnp.int32(1), 28 - 2 * i)
        c1, c2, c3 = thr | lo, thr | hi, thr | hi | lo
        return jnp.where(enough(c3), c3, jnp.where(enough(c2), c2, jnp.where(enough(c1), c1, thr)))

    thr = lax.fori_loop(0, 15, two_bit_step, jnp.zeros((ne, 1), jnp.int32))
    gt = aff >= pltpu.bitcast(thr + 1, f32)
    eq = jnp.logical_and(aff >= pltpu.bitcast(thr, f32), jnp.logical_not(gt))
    need = cap - jnp.sum(gt.astype(jnp.int32), axis=1, keepdims=True)

    def cumsum_tokens(v):
        k = 1
        while k < s:
            v = v + jnp.where(lane >= k, pltpu.roll(v, k, axis=1), 0)
            k *= 2
        return v

    eq_i = eq.astype(jnp.int32)
    surplus = jnp.max(jnp.sum(eq_i, axis=1, keepdims=True) - need) > 0
    keep = lax.cond(surplus, lambda: (cumsum_tokens(eq_i) <= need).astype(jnp.int32), lambda: eq_i)
    sel = jnp.logical_or(gt, jnp.logical_and(eq, keep == 1))
    slot = cumsum_tokens(sel.astype(jnp.int32)) - 1
    word = jnp.where(sel, lane | jnp.left_shift(lane - slot, nbits) | (1 << (2 * nbits)), 0)
    gate = aff
    for k in range(nbits):
        step = 1 << k
        inc_word = pltpu.roll(word, s - step, axis=1)
        inc_gate = pltpu.roll(gate, s - step, axis=1)
        take = (jnp.right_shift(inc_word, nbits + k) & 1) == 1
        leave = (jnp.right_shift(word, nbits + k) & 1) == 1
        word = jnp.where(take, inc_word, jnp.where(leave, 0, word))
        gate = jnp.where(take, inc_gate, gate)
    idx_ref[0] = word[:, :cap] & (s - 1)
    gate_ref[0] = gate[:, :cap]


def _select(aff_t, nb, s, cap):
    ne = aff_t.shape[0]
    return pl.pallas_call(
        functools.partial(_select_kernel, cap=cap),
        grid=(nb,),
        in_specs=[pl.BlockSpec((ne, s), lambda bi: (0, bi))],
        out_specs=[
            pl.BlockSpec((1, ne, cap), lambda bi: (bi, 0, 0)),
            pl.BlockSpec((1, ne, cap), lambda bi: (bi, 0, 0)),
        ],
        out_shape=[
            jax.ShapeDtypeStruct((nb, ne, cap), jnp.int32),
            jax.ShapeDtypeStruct((nb, ne, cap), f32),
        ],
        compiler_params=_params(("parallel",)),
        name="select",
    )(aff_t)


GATHER_ROWS = 64
SCATTER_ROWS = 16


def _slab_pitch(rows):
    pitch = rows + (-rows) % 4
    return pitch if (pitch // 4) % 2 else pitch + 4


def _gather_kernel(idx_ref, h2_hbm, xe_ref, buf_ref, stage_ref, sem, *, ne, cap, s):
    b = pl.program_id(0)
    e = pl.program_id(1)
    nslab = xe_ref.shape[2] // GROUP
    pitch = buf_ref.shape[1] // nslab
    spitch = stage_ref.shape[0] // nslab
    slot = b % 2
    base = (b * ne + e) * cap

    def slab_copy(batch, into, c):
        rows = pl.ds(pl.multiple_of(batch * s, s), s)
        return pltpu.make_async_copy(h2_hbm.at[c, rows, :],
                                     buf_ref.at[into, c * pitch:c * pitch + s, :], sem.at[into])

    @pl.when(jnp.logical_and(b == 0, e == 0))
    def _():
        for c in range(nslab):
            slab_copy(0, 0, c).start()

    @pl.when(e == 0)
    def _():
        @pl.when(b + 1 < pl.num_programs(0))
        def _():
            for c in range(nslab):
                slab_copy(b + 1, 1 - slot, c).start()

        for c in range(nslab):
            slab_copy(b, slot, c).wait()

    def body(i, _):
        r0 = pl.multiple_of(i * GATHER_ROWS, GATHER_ROWS)
        for j in range(GATHER_ROWS):
            tok = idx_ref[base + r0 + j]
            stage_ref[pl.ds(j, nslab, stride=spitch), :] = buf_ref[slot, pl.ds(tok, nslab, stride=pitch), :]
        for c in range(nslab):
            xe_ref[0, pl.ds(r0, GATHER_ROWS), c * GROUP:(c + 1) * GROUP] = \
                stage_ref[c * spitch:c * spitch + GATHER_ROWS, :].astype(bf16)
        return 0

    lax.fori_loop(0, cap // GATHER_ROWS, body, 0)


def _gather(idx_flat, h2_slabs, nb, s, ne, cap):
    nslab = h2_slabs.shape[0]
    d = nslab * GROUP
    return pl.pallas_call(
        functools.partial(_gather_kernel, ne=ne, cap=cap, s=s),
        grid_spec=pltpu.PrefetchScalarGridSpec(
            num_scalar_prefetch=1,
            grid=(nb, ne),
            in_specs=[pl.BlockSpec(memory_space=pl.ANY)],
            out_specs=pl.BlockSpec((1, cap, d), lambda bi, e, idx: (e, bi, 0)),
            scratch_shapes=[
                pltpu.VMEM((2, nslab * _slab_pitch(s), GROUP), f32),
                pltpu.VMEM((nslab * _slab_pitch(GATHER_ROWS), GROUP), f32),
                pltpu.SemaphoreType.DMA((2,)),
            ],
        ),
        out_shape=jax.ShapeDtypeStruct((ne, nb * cap, d), bf16),
        compiler_params=_params(("arbitrary", "arbitrary")),
        name="gather",
    )(idx_flat, h2_slabs)


def _expert_kernel(x_ref, wg_ref, wu_ref, wd_ref, gate_ref, y_ref, acc_ref, *, sub, nf):
    f = pl.program_id(2)
    tmo, d = acc_ref.shape

    def partial_out(rows, w_gate, w_up, w_down):
        xs = x_ref[0, rows, :]
        a = _dot(xs, w_gate)
        g = _dot(xs, w_up)
        hm = (a * jax.nn.sigmoid(a) * g).astype(bf16)
        return _dot(hm, w_down)

    def store_gated(rows, y):
        for c in range(rows.start // GROUP, rows.stop // GROUP):
            chunk = slice(c * GROUP, (c + 1) * GROUP)
            g_col = jnp.broadcast_to(gate_ref[0, :, chunk], (GROUP, GROUP)).T
            y_c = y[chunk.start - rows.start:chunk.stop - rows.start, :]
            y_ref[0, chunk, :] = (y_c * jnp.tile(g_col, (1, d // GROUP))).astype(bf16)

    def run(first, last):
        weights = [w_ref[0].astype(bf16) for w_ref in (wg_ref, wu_ref, wd_ref)]
        for r0 in range(0, tmo, sub):
            rows = slice(r0, r0 + sub)
            y = partial_out(rows, *weights)
            if not first:
                y = acc_ref[rows, :] + y
            if last:
                store_gated(rows, y)
            else:
                acc_ref[rows, :] = y

    if nf == 1:
        run(True, True)
    else:
        pl.when(f == 0)(lambda: run(True, False))
        pl.when(f == nf - 1)(lambda: run(False, True))
        if nf > 2:
            pl.when(jnp.logical_and(f > 0, f < nf - 1))(lambda: run(False, False))


def _experts(xe, wg, wu, wd, gate_row, tmo, tf, sub):
    ne, m, d = xe.shape
    ff = wg.shape[2]
    return pl.pallas_call(
        functools.partial(_expert_kernel, sub=sub, nf=ff // tf),
        grid=(ne, m // tmo, ff // tf),
        in_specs=[
            pl.BlockSpec((1, tmo, d), lambda e, mi, fi: (e, mi, 0)),
            pl.BlockSpec((1, d, tf), lambda e, mi, fi: (e, 0, fi)),
            pl.BlockSpec((1, d, tf), lambda e, mi, fi: (e, 0, fi)),
            pl.BlockSpec((1, tf, d), lambda e, mi, fi: (e, fi, 0)),
            pl.BlockSpec((1, 1, tmo), lambda e, mi, fi: (e, 0, mi)),
        ],
        out_specs=pl.BlockSpec((1, tmo, d), lambda e, mi, fi: (e, mi, 0)),
        out_shape=jax.ShapeDtypeStruct((ne, m, d), bf16),
        scratch_shapes=[pltpu.VMEM((tmo, d), f32)],
        compiler_params=_params(("parallel", "parallel", "arbitrary")),
        name="experts",
    )(xe, wg, wu, wd, gate_row)


NORM_ROWS = 256


def _combine_kernel(idx_ref, ye_ref, x1_hbm, g_ref, out_hbm, acc_ref, ys_ref, x1_ref, stage_ref, sem_x1, sem_out,
                    *, ne, cap, s, do_norm):
    b = pl.program_id(0)
    e = pl.program_id(1)
    d = x1_ref.shape[1]
    nslab = d // GROUP
    pitch = acc_ref.shape[0] // nslab
    ypitch = ys_ref.shape[0] // nslab
    base = (b * ne + e) * cap
    batch_row0 = pl.multiple_of(b * s, s)
    x1_copy = pltpu.make_async_copy(x1_hbm.at[pl.ds(batch_row0, s), :], x1_ref, sem_x1)

    @pl.when(e == 0)
    def _():
        x1_copy.start()
        acc_ref[...] = jnp.zeros_like(acc_ref)

    for c in range(nslab):
        ys_ref[c * ypitch:c * ypitch + cap, :] = ye_ref[0, :, c * GROUP:(c + 1) * GROUP].astype(f32)

    def body(i, _):
        r0 = pl.multiple_of(i * SCATTER_ROWS, SCATTER_ROWS)
        toks = [idx_ref[base + r0 + j] for j in range(SCATTER_ROWS)]
        rows = [acc_ref[pl.ds(tok, nslab, stride=pitch), :] for tok in toks]
        for j in range(SCATTER_ROWS):
            acc_ref[pl.ds(toks[j], nslab, stride=pitch), :] = \
                rows[j] + ys_ref[pl.ds(r0 + j, nslab, stride=ypitch), :]
        return 0

    lax.fori_loop(0, cap // SCATTER_ROWS, body, 0)

    @pl.when(e == ne - 1)
    def _():
        x1_copy.wait()
        nsteps = s // NORM_ROWS

        def out_copy(step):
            rows = pl.ds(batch_row0 + step * NORM_ROWS, NORM_ROWS)
            return pltpu.make_async_copy(stage_ref.at[step % 2], out_hbm.at[rows, :], sem_out.at[step % 2])

        for step in range(nsteps):
            lo = step * NORM_ROWS
            moe = jnp.concatenate([acc_ref[c * pitch + lo:c * pitch + lo + NORM_ROWS, :] for c in range(nslab)],
                                  axis=1)
            y = x1_ref[lo:lo + NORM_ROWS, :] + moe
            if do_norm:
                ms = jnp.mean(y * y, axis=-1, keepdims=True)
                y = y * lax.rsqrt(ms + EPS) * g_ref[...]
            if step >= 2:
                out_copy(step - 2).wait()
            stage_ref[step % 2] = y
            out_copy(step).start()
        for step in range(max(nsteps - 2, 0), nsteps):
            out_copy(step).wait()


def _combine(idx_flat, ye, x1, g, nb, s, ne, cap, do_norm):
    d = ye.shape[2]
    nslab = d // GROUP
    return pl.pallas_call(
        functools.partial(_combine_kernel, ne=ne, cap=cap, s=s, do_norm=do_norm),
        grid_spec=pltpu.PrefetchScalarGridSpec(
            num_scalar_prefetch=1,
            grid=(nb, ne),
            in_specs=[
                pl.BlockSpec((1, cap, d), lambda bi, e, idx: (e, bi, 0)),
                pl.BlockSpec(memory_space=pl.ANY),
                pl.BlockSpec((1, d), lambda bi, e, idx: (0, 0)),
            ],
            out_specs=pl.BlockSpec(memory_space=pl.ANY),
            scratch_shapes=[
                pltpu.VMEM((nslab * _slab_pitch(s), GROUP), f32),
                pltpu.VMEM((nslab * _slab_pitch(cap), GROUP), f32),
                pltpu.VMEM((s, d), f32),
                pltpu.VMEM((2, NORM_ROWS, d), f32),
                pltpu.SemaphoreType.DMA,
                pltpu.SemaphoreType.DMA((2,)),
            ],
        ),
        out_shape=jax.ShapeDtypeStruct((nb * s, d), f32),
        compiler_params=_params(("arbitrary", "arbitrary")),
        name="combine",
    )(idx_flat, ye, x1, g)


def _channel_dft_table():
    k = jnp.arange(GROUP, dtype=jnp.int32)
    ang = ((k[:, None] * k[None, :]) % GROUP).astype(f32) * (2.0 * math.pi / GROUP)
    return jnp.concatenate([jnp.cos(ang), jnp.sin(ang)], axis=1).astype(bf16)


def _sequence_dft_tables(s, ts, scale):
    half = s // 2
    inner = 64
    k = jnp.arange(half, dtype=jnp.int32)[None, :]
    n1 = jnp.arange(half // inner, dtype=jnp.int32)[:, None] * inner
    n2 = jnp.arange(inner, dtype=jnp.int32)[:, None]
    w = 2.0 * math.pi / s
    a1 = ((n1 * k) % s).astype(f32) * w
    a2 = ((n2 * k) % s).astype(f32) * w
    c1, s1 = jnp.cos(a1)[:, None, :], jnp.sin(a1)[:, None, :]
    c2, s2 = jnp.cos(a2)[None, :, :] * scale, jnp.sin(a2)[None, :, :] * scale
    cp = (c1 * c2 - s1 * s2).reshape(half, half).astype(bf16)
    sm = (-(s1 * c2 + c1 * s2)).reshape(half, half).astype(bf16)
    rows = jnp.arange(ts, dtype=jnp.int32)[:, None]
    cols = jnp.arange(ts, dtype=jnp.int32)[None, :]
    flip = (cols == ts - rows).astype(bf16)
    kk = jnp.arange(half, dtype=jnp.int32)[None, :]
    alt = jnp.where(jnp.arange(8)[:, None] == 0, (1 - 2 * (kk & 1)).astype(f32) * scale, 0.0).astype(bf16)
    fold = min(FOLD_ROWS, half)
    r = jnp.arange(fold, dtype=jnp.int32)
    j1 = ((r[None, :] == fold - r[:, None]) & (r[:, None] >= 1)).astype(bf16)
    return cp, sm, flip, alt, j1


def _pick(n, pref):
    return pref if n % pref == 0 else n


def kernel(x, norm1_g, w_in, sgu_ln_g, sgu_ln_b, w_spatial, b_spatial, w_fourier_out, w_sgu_out, w_out,
           norm2_g, w_router, w_gate_e, w_up_e, w_down_e, final_g):
    nb, s, d = x.shape
    depth = norm1_g.shape[0]
    ne = w_router.shape[2]
    ff = w_gate_e.shape[3]
    cap = CAPACITY_FACTOR * s // ne
    t = nb * s
    assert s % NORM_ROWS == 0 and s & (s - 1) == 0 and cap % GATHER_ROWS == 0 and d % GROUP == 0
    assert (nb * cap) % GROUP == 0

    tm_front = _pick(s, 1024)
    ts_dft = _pick(s // 2, 512)
    tm_post = _pick(s // 2, 1024)
    dft_scale = 1.0 / math.sqrt(s * GROUP)
    cs_tab = _channel_dft_table()
    seq_tables = _sequence_dft_tables(s, ts_dft, dft_scale)
    tmo = _pick(nb * cap, 2048)
    tf = _pick(ff, 512)
    sub = _pick(tmo, 1024)

    x2 = x.reshape(t, d)
    for l in range(depth):
        a, b, ysg, sgf = _front(
            x2, norm1_g[l][None, :], w_in[l].astype(bf16), cs_tab,
            sgu_ln_g[l][None, :], sgu_ln_b[l][None, :], w_spatial[l].astype(bf16),
            b_spatial[l].T, w_sgu_out[l].astype(bf16), tm_front)
        fm_first, fm_second = _seqdft(seq_tables, a, b, nb, s, ts_dft, dft_scale)
        wr = jnp.pad(w_router[l], ((0, 0), (0, ROUTER_LANES - ne)))
        wr_hi = wr.astype(bf16)
        wr_lo = (wr - wr_hi.astype(f32)).astype(bf16)
        wr_split = jnp.concatenate([jnp.concatenate([wr_hi, wr_lo], axis=1),
                                    jnp.concatenate([wr_hi, jnp.zeros_like(wr_hi)], axis=1)], axis=0)
        x1, h2, aff = _post(fm_first, fm_second, sgf, ysg, x2, w_fourier_out[l].astype(bf16),
                            w_out[l].astype(bf16), norm2_g[l][None, :], wr_split, ne, s, tm_post)
        idx, gate = _select(aff, nb, s, cap)
        idx_flat = idx.reshape(-1)
        xe = _gather(idx_flat, h2, nb, s, ne, cap)
        gate_row = jnp.transpose(gate, (1, 0, 2)).reshape(ne, 1, nb * cap)
        ye = _experts(xe, w_gate_e[l], w_up_e[l], w_down_e[l], gate_row, tmo, tf, sub)
        x2 = _combine(idx_flat, ye, x1, final_g[None, :], nb, s, ne, cap, l == depth - 1)
    return x2.reshape(nb, s, d)
```

```python
import functools
import math

import jax
import jax.numpy as jnp
from jax import lax
from jax.experimental import pallas as pl
from jax.experimental.pallas import tpu as pltpu

EPS = 1e-6
GROUP = 128
N_GROUPS = 4
F_WIDTH = N_GROUPS * GROUP
S_WIDTH = N_GROUPS * GROUP
CAPACITY_FACTOR = 2
VMEM_LIMIT_V7X = 56 * 1024 * 1024

f32 = jnp.float32
bf16 = jnp.bfloat16


def _dot(a, b):
    return jnp.dot(a, b, preferred_element_type=f32)


def _params(sem, vmem=VMEM_LIMIT_V7X):
    return pltpu.CompilerParams(dimension_semantics=sem, vmem_limit_bytes=vmem)


FRONT_ROWS = 1024


def _front_kernel(x_ref, g1_ref, win_ref, cs_ref, lng_ref, lnb_ref, ws_ref, bs_ref, wso_ref,
                  a_ref, b_ref, ysg_ref, sgf_ref, mix_ref):
    tm, d = x_ref.shape
    c0 = F_WIDTH + 2 * S_WIDTH
    chunk = min(FRONT_ROWS, tm)
    for r0 in range(0, tm, chunk):
        rows = slice(r0, r0 + chunk)
        x = x_ref[rows, :]
        ms = jnp.mean(x * x, axis=-1, keepdims=True)
        h = (x * lax.rsqrt(ms + EPS) * g1_ref[...]).astype(bf16)

        def proj(lo, hi):
            return _dot(h, win_ref[:, lo:hi])

        zf = proj(0, F_WIDTH)
        for g in range(N_GROUPS):
            cols = slice(g * GROUP, (g + 1) * GROUP)
            ab = _dot(zf[:, cols].astype(bf16), cs_ref[...])
            a_ref[rows, cols] = ab[:, :GROUP].astype(bf16)
            b_ref[rows, cols] = ab[:, GROUP:].astype(bf16)

        u_pre = proj(F_WIDTH, F_WIDTH + S_WIDTH)
        v_pre = proj(F_WIDTH + S_WIDTH, F_WIDTH + 2 * S_WIDTH)
        sgf_ref[rows, :] = jax.nn.sigmoid(proj(c0, c0 + d)).astype(bf16)
        gs_pre = proj(c0 + d, c0 + 2 * d)
        u = jax.nn.gelu(u_pre)
        v = jax.nn.gelu(v_pre)
        for g in range(N_GROUPS):
            cols = slice(g * GROUP, (g + 1) * GROUP)
            vg = v[:, cols]
            mu = jnp.mean(vg, axis=-1, keepdims=True)
            dv = vg - mu
            var = jnp.mean(dv * dv, axis=-1, keepdims=True)
            vnb = (dv * lax.rsqrt(var + EPS) * lng_ref[:, cols] + lnb_ref[:, cols]).astype(bf16)
            for c in range(chunk // GROUP):
                m = _dot(ws_ref[g], vnb[c * GROUP:(c + 1) * GROUP, :]) + bs_ref[:, g:g + 1]
                mix_ref[r0 + c * GROUP:r0 + (c + 1) * GROUP, cols] = m
        sgu = (u * mix_ref[rows, :]).astype(bf16)
        ys = _dot(sgu, wso_ref[...])
        ysg_ref[rows, :] = (jax.nn.sigmoid(gs_pre) * ys).astype(bf16)


def _front(x2, g1, win_b, cs_b, lng, lnb, ws_b, bs_t, wso_b, tm):
    t, d = x2.shape
    kin = win_b.shape[1]
    const = lambda *shape: pl.BlockSpec(shape, lambda i: (0,) * len(shape))
    return pl.pallas_call(
        _front_kernel,
        grid=(t // tm,),
        in_specs=[
            pl.BlockSpec((tm, d), lambda i: (i, 0)),
            const(1, d),
            const(d, kin),
            const(GROUP, 2 * GROUP),
            const(1, S_WIDTH),
            const(1, S_WIDTH),
            const(N_GROUPS, GROUP, GROUP),
            const(GROUP, N_GROUPS),
            const(S_WIDTH, d),
        ],
        out_specs=[
            pl.BlockSpec((tm, F_WIDTH), lambda i: (i, 0)),
            pl.BlockSpec((tm, F_WIDTH), lambda i: (i, 0)),
            pl.BlockSpec((tm, d), lambda i: (i, 0)),
            pl.BlockSpec((tm, d), lambda i: (i, 0)),
        ],
        out_shape=[
            jax.ShapeDtypeStruct((t, F_WIDTH), bf16),
            jax.ShapeDtypeStruct((t, F_WIDTH), bf16),
            jax.ShapeDtypeStruct((t, d), bf16),
            jax.ShapeDtypeStruct((t, d), bf16),
        ],
        scratch_shapes=[pltpu.VMEM((tm, S_WIDTH), f32)],
        compiler_params=_params(("parallel",)),
        name="front",
    )(x2, g1, win_b, cs_b, lng, lnb, ws_b, bs_t, wso_b)


FOLD_ROWS = 256


def _seqdft_kernel(cp_ref, sm_ref, flip_ref, altrow_ref, j1_ref, a_ref, b_ref, fmd_ref, fmm_ref,
                   ap_ref, bm_ref, edge_ref, *, scale):
    i = pl.program_id(1)
    ts = fmd_ref.shape[0]
    s = a_ref.shape[0]
    half = s // 2
    a_mid = scale * a_ref[half:half + 1, :].astype(f32)

    @pl.when(i == 0)
    def _fold_inputs():
        fold = j1_ref.shape[0]
        first_row = lax.broadcasted_iota(jnp.int32, (fold, 1), 0) == 0
        for j in range(half // fold):
            lo = slice(j * fold, (j + 1) * fold)
            mir = slice(s - (j + 1) * fold, s - j * fold)
            a_m = _dot(j1_ref[...], a_ref[mir, :])
            b_m = _dot(j1_ref[...], b_ref[mir, :])
            if j > 0:
                a_m = jnp.where(first_row, a_ref[mir.stop:mir.stop + 1, :].astype(f32), a_m)
                b_m = jnp.where(first_row, b_ref[mir.stop:mir.stop + 1, :].astype(f32), b_m)
            ap_ref[lo, :] = (a_ref[lo, :].astype(f32) + a_m).astype(bf16)
            bm_ref[lo, :] = (b_ref[lo, :].astype(f32) - b_m).astype(bf16)
        first = lax.broadcasted_iota(jnp.int32, (8, 1), 0) == 0
        edge_ref[...] = _dot(altrow_ref[...], ap_ref[...]) + jnp.where(first, a_mid, 0.0)

    p = _dot(cp_ref[...], ap_ref[...])
    q = _dot(sm_ref[...], bm_ref[...])
    row = lax.broadcasted_iota(jnp.int32, (ts, 1), 0)
    alt = (1 - 2 * (row & 1)).astype(f32) * a_mid
    fmd_ref[...] = (p + q + alt).astype(bf16)
    g = p - q + alt
    flipped = _dot(flip_ref[...], g.astype(bf16))
    fmm_ref[...] = jnp.where(row == 0, edge_ref[0:1, :], flipped).astype(bf16)
    edge_ref[...] = g[0:8, :]


def _seqdft(tables, a, b, nb, s, ts, scale):
    half = s // 2
    nh = half // ts
    cp, sm, flip, altrow, j1 = tables
    const = lambda *shape: pl.BlockSpec(shape, lambda bi, i: (0,) * len(shape))
    table_tile = pl.BlockSpec((ts, half), lambda bi, i: (nh - 1 - i, 0))
    return pl.pallas_call(
        functools.partial(_seqdft_kernel, scale=scale),
        grid=(nb, nh),
        in_specs=[
            table_tile,
            table_tile,
            const(ts, ts),
            const(8, half),
            const(*j1.shape),
            pl.BlockSpec((s, F_WIDTH), lambda bi, i: (bi, 0)),
            pl.BlockSpec((s, F_WIDTH), lambda bi, i: (bi, 0)),
        ],
        out_specs=[
            pl.BlockSpec((ts, F_WIDTH), lambda bi, i: (bi * nh + nh - 1 - i, 0)),
            pl.BlockSpec((ts, F_WIDTH), lambda bi, i: (bi * nh + i, 0)),
        ],
        out_shape=[
            jax.ShapeDtypeStruct((nb * half, F_WIDTH), bf16),
            jax.ShapeDtypeStruct((nb * half, F_WIDTH), bf16),
        ],
        scratch_shapes=[
            pltpu.VMEM((half, F_WIDTH), bf16),
            pltpu.VMEM((half, F_WIDTH), bf16),
            pltpu.VMEM((8, F_WIDTH), f32),
        ],
        compiler_params=_params(("parallel", "arbitrary")),
        name="seqdft",
    )(cp, sm, flip, altrow, j1, a, b)


ROUTER_LANES = 128
POST_ROWS = 256


def _post_kernel(fmd_ref, fmm_ref, sgf_ref, ysg_ref, x_ref, wfo_ref, wout_ref, g2_ref, wr_ref,
                 x1_ref, h2_ref, aff_ref, *, tiles_per_half):
    ne = aff_ref.shape[0]
    tm = x_ref.shape[0]
    in_first_half = (pl.program_id(0) % (2 * tiles_per_half)) < tiles_per_half
    chunk = min(POST_ROWS, tm)
    for r0 in range(0, tm, chunk):
        rows = slice(r0, r0 + chunk)
        fm = jnp.where(in_first_half, fmd_ref[rows, :], fmm_ref[rows, :])
        yf = _dot(fm, wfo_ref[...])
        merged = sgf_ref[rows, :].astype(f32) * yf + ysg_ref[rows, :].astype(f32)
        x1 = x_ref[rows, :] + _dot(merged.astype(bf16), wout_ref[...])
        x1_ref[rows, :] = x1
        ms = jnp.mean(x1 * x1, axis=-1, keepdims=True)
        h2 = x1 * lax.rsqrt(ms + EPS) * g2_ref[...]
        h2_ref[rows, :] = h2
        h_hi = h2.astype(bf16)
        h_lo = (h2 - h_hi.astype(f32)).astype(bf16)
        both = _dot(jnp.concatenate([h_hi, h_lo], axis=1), wr_ref[...])
        logits = both[:, :ROUTER_LANES] + both[:, ROUTER_LANES:]
        logits = logits.T[:ne, :]
        mx = jnp.max(logits, axis=0, keepdims=True)
        ex = jnp.exp(logits - mx)
        aff_ref[:, rows] = ex / jnp.sum(ex, axis=0, keepdims=True)


def _post(fm_first, fm_second, sgf, ysg, x2, wfo_b, wout_b, g2, wr_split, ne, s, tm):
    t, d = x2.shape
    th = s // 2 // tm
    const = lambda *shape: pl.BlockSpec(shape, lambda i: (0,) * len(shape))
    tile = lambda w: pl.BlockSpec((tm, w), lambda i: (i, 0))
    first = pl.BlockSpec((tm, F_WIDTH), lambda i: (i // (2 * th) * th + jnp.minimum(i % (2 * th), th - 1), 0))
    second = pl.BlockSpec((tm, F_WIDTH), lambda i: (i // (2 * th) * th + jnp.maximum(i % (2 * th) - th, 0), 0))
    return pl.pallas_call(
        functools.partial(_post_kernel, tiles_per_half=th),
        grid=(t // tm,),
        in_specs=[
            first, second, tile(d), tile(d), tile(d),
            const(F_WIDTH, d),
            const(d, d),
            const(1, d),
            const(2 * d, 2 * ROUTER_LANES),
        ],
        out_specs=[tile(d), tile(d), pl.BlockSpec((ne, tm), lambda i: (0, i))],
        out_shape=[
            jax.ShapeDtypeStruct((t, d), f32),
            jax.ShapeDtypeStruct((t, d), f32),
            jax.ShapeDtypeStruct((ne, t), f32),
        ],
        compiler_params=_params(("parallel",)),
        name="post",
    )(fm_first, fm_second, sgf, ysg, x2, wfo_b, wout_b, g2, wr_split)


def _select_kernel(aff_ref, idx_ref, gate_ref, *, cap):
    ne, s = aff_ref.shape
    nbits = s.bit_length() - 1
    aff = aff_ref[...]
    lane = lax.broadcasted_iota(jnp.int32, (ne, s), 1)

    def enough(cand):
        cnt = jnp.sum((aff >= pltpu.bitcast(cand, f32)).astype(jnp.int32), axis=1, keepdims=True)
        return cnt >= cap

    def two_bit_step(i, thr):
        hi = jnp.left_shift(jnp.int32(1), 29 - 2 * i)
        lo = jnp.left_shift(jnp.int32(1), 28 - 2 * i)
        c1, c2, c3 = thr | lo, thr | hi, thr | hi | lo
        return jnp.where(enough(c3), c3, jnp.where(enough(c2), c2, jnp.where(enough(c1), c1, thr)))

    thr = lax.fori_loop(0, 15, two_bit_step, jnp.zeros((ne, 1), jnp.int32))
    gt = aff >= pltpu.bitcast(thr + 1, f32)
    eq = jnp.logical_and(aff >= pltpu.bitcast(thr, f32), jnp.logical_not(gt))
    need = cap - jnp.sum(gt.astype(jnp.int32), axis=1, keepdims=True)

    def cumsum_tokens(v):
        k = 1
        while k < s:
            v = v + jnp.where(lane >= k, pltpu.roll(v, k, axis=1), 0)
            k *= 2
        return v

    eq_i = eq.astype(jnp.int32)
    surplus = jnp.max(jnp.sum(eq_i, axis=1, keepdims=True) - need) > 0
    keep = lax.cond(surplus, lambda: (cumsum_tokens(eq_i) <= need).astype(jnp.int32), lambda: eq_i)
    sel = jnp.logical_or(gt, jnp.logical_and(eq, keep == 1))
    slot = cumsum_tokens(sel.astype(jnp.int32)) - 1
    word = jnp.where(sel, lane | jnp.left_shift(lane - slot, nbits) | (1 << (2 * nbits)), 0)
    gate = aff
    for k in range(nbits):
        step = 1 << k
        inc_word = pltpu.roll(word, s - step, axis=1)
        inc_gate = pltpu.roll(gate, s - step, axis=1)
        take = (jnp.right_shift(inc_word, nbits + k) & 1) == 1
        leave = (jnp.right_shift(word, nbits + k) & 1) == 1
        word = jnp.where(take, inc_word, jnp.where(leave, 0, word))
        gate = jnp.where(take, inc_gate, gate)
    idx_ref[0] = word[:, :cap] & (s - 1)
    gate_ref[0] = gate[:, :cap]


def _select(aff_t, nb, s, cap):
    ne = aff_t.shape[0]
    return pl.pallas_call(
        functools.partial(_select_kernel, cap=cap),
        grid=(nb,),
        in_specs=[pl.BlockSpec((ne, s), lambda bi: (0, bi))],
        out_specs=[
            pl.BlockSpec((1, ne, cap), lambda bi: (bi, 0, 0)),
            pl.BlockSpec((1, ne, cap), lambda bi: (bi, 0, 0)),
        ],
        out_shape=[
            jax.ShapeDtypeStruct((nb, ne, cap), jnp.int32),
            jax.ShapeDtypeStruct((nb, ne, cap), f32),
        ],
        compiler_params=_params(("parallel",)),
        name="select",
    )(aff_t)


GATHER_ROWS = 64
SCATTER_ROWS = 16


def _slab_pitch(rows):
    pitch = rows + (-rows) % 4
    return pitch if (pitch // 4) % 2 else pitch + 4


def _gather_kernel(idx_ref, h2_hbm, xe_ref, buf_ref, stage_ref, sem, *, ne, cap, s):
    b = pl.program_id(0)
    e = pl.program_id(1)
    nslab = xe_ref.shape[2] // GROUP
    pitch = buf_ref.shape[1] // nslab
    spitch = stage_ref.shape[0] // nslab
    slot = b % 2
    base = (b * ne + e) * cap

    def slab_copy(batch, into, c):
        rows = pl.ds(pl.multiple_of(batch * s, s), s)
        return pltpu.make_async_copy(h2_hbm.at[rows, c * GROUP:(c + 1) * GROUP],
                                     buf_ref.at[into, c * pitch:c * pitch + s, :], sem.at[into])

    @pl.when(jnp.logical_and(b == 0, e == 0))
    def _():
        for c in range(nslab):
            slab_copy(0, 0, c).start()

    @pl.when(e == 0)
    def _():
        @pl.when(b + 1 < pl.num_programs(0))
        def _():
            for c in range(nslab):
                slab_copy(b + 1, 1 - slot, c).start()

        for c in range(nslab):
            slab_copy(b, slot, c).wait()

    def body(i, _):
        r0 = pl.multiple_of(i * GATHER_ROWS, GATHER_ROWS)
        for j in range(GATHER_ROWS):
            tok = idx_ref[base + r0 + j]
            stage_ref[pl.ds(j, nslab, stride=spitch), :] = buf_ref[slot, pl.ds(tok, nslab, stride=pitch), :]
        for c in range(nslab):
            xe_ref[0, pl.ds(r0, GATHER_ROWS), c * GROUP:(c + 1) * GROUP] = \
                stage_ref[c * spitch:c * spitch + GATHER_ROWS, :].astype(bf16)
        return 0

    lax.fori_loop(0, cap // GATHER_ROWS, body, 0)


def _gather(idx_flat, h2, nb, s, ne, cap):
    d = h2.shape[1]
    nslab = d // GROUP
    return pl.pallas_call(
        functools.partial(_gather_kernel, ne=ne, cap=cap, s=s),
        grid_spec=pltpu.PrefetchScalarGridSpec(
            num_scalar_prefetch=1,
            grid=(nb, ne),
            in_specs=[pl.BlockSpec(memory_space=pl.ANY)],
            out_specs=pl.BlockSpec((1, cap, d), lambda bi, e, idx: (e, bi, 0)),
            scratch_shapes=[
                pltpu.VMEM((2, nslab * _slab_pitch(s), GROUP), f32),
                pltpu.VMEM((nslab * _slab_pitch(GATHER_ROWS), GROUP), f32),
                pltpu.SemaphoreType.DMA((2,)),
            ],
        ),
        out_shape=jax.ShapeDtypeStruct((ne, nb * cap, d), bf16),
        compiler_params=_params(("arbitrary", "arbitrary")),
        name="gather",
    )(idx_flat, h2)


def _expert_kernel(x_ref, wg_ref, wu_ref, wd_ref, gate_ref, y_ref, acc_ref, *, sub, nf):
    f = pl.program_id(2)
    tmo, d = acc_ref.shape

    def partial_out(rows, w_gate, w_up, w_down):
        xs = x_ref[0, rows, :]
        a = _dot(xs, w_gate)
        g = _dot(xs, w_up)
        hm = (a * jax.nn.sigmoid(a) * g).astype(bf16)
        return _dot(hm, w_down)

    def store_gated(rows, y):
        for c in range(rows.start // GROUP, rows.stop // GROUP):
            chunk = slice(c * GROUP, (c + 1) * GROUP)
            g_col = jnp.broadcast_to(gate_ref[0, :, chunk], (GROUP, GROUP)).T
            y_c = y[chunk.start - rows.start:chunk.stop - rows.start, :]
            y_ref[0, chunk, :] = (y_c * jnp.tile(g_col, (1, d // GROUP))).astype(bf16)

    def run(first, last):
        weights = [w_ref[0].astype(bf16) for w_ref in (wg_ref, wu_ref, wd_ref)]
        for r0 in range(0, tmo, sub):
            rows = slice(r0, r0 + sub)
            y = partial_out(rows, *weights)
            if not first:
                y = acc_ref[rows, :] + y
            if last:
                store_gated(rows, y)
            else:
                acc_ref[rows, :] = y

    if nf == 1:
        run(True, True)
    else:
        pl.when(f == 0)(lambda: run(True, False))
        pl.when(f == nf - 1)(lambda: run(False, True))
        if nf > 2:
            pl.when(jnp.logical_and(f > 0, f < nf - 1))(lambda: run(False, False))


def _experts(xe, wg, wu, wd, gate_row, tmo, tf, sub):
    ne, m, d = xe.shape
    ff = wg.shape[2]
    return pl.pallas_call(
        functools.partial(_expert_kernel, sub=sub, nf=ff // tf),
        grid=(ne, m // tmo, ff // tf),
        in_specs=[
            pl.BlockSpec((1, tmo, d), lambda e, mi, fi: (e, mi, 0)),
            pl.BlockSpec((1, d, tf), lambda e, mi, fi: (e, 0, fi)),
            pl.BlockSpec((1, d, tf), lambda e, mi, fi: (e, 0, fi)),
            pl.BlockSpec((1, tf, d), lambda e, mi, fi: (e, fi, 0)),
            pl.BlockSpec((1, 1, tmo), lambda e, mi, fi: (e, 0, mi)),
        ],
        out_specs=pl.BlockSpec((1, tmo, d), lambda e, mi, fi: (e, mi, 0)),
        out_shape=jax.ShapeDtypeStruct((ne, m, d), bf16),
        scratch_shapes=[pltpu.VMEM((tmo, d), f32)],
        compiler_params=_params(("parallel", "parallel", "arbitrary")),
        name="experts",
    )(xe, wg, wu, wd, gate_row)


NORM_ROWS = 256


def _combine_kernel(idx_ref, ye_ref, x1_hbm, g_ref, out_hbm, acc_ref, ys_ref, stage_ref, sem_x1, sem_out,
                    *, ne, cap, s, do_norm):
    b = pl.program_id(0)
    e = pl.program_id(1)
    d = stage_ref.shape[2]
    nslab = d // GROUP
    pitch = acc_ref.shape[1] // nslab
    ypitch = ys_ref.shape[0] // nslab
    slot = b % 2
    base = (b * ne + e) * cap

    def x1_slab_copy(batch, into, c):
        rows = pl.ds(pl.multiple_of(batch * s, s), s)
        return pltpu.make_async_copy(x1_hbm.at[rows, c * GROUP:(c + 1) * GROUP],
                                     acc_ref.at[into, c * pitch:c * pitch + s, :], sem_x1.at[into])

    @pl.when(jnp.logical_and(b == 0, e == 0))
    def _():
        for c in range(nslab):
            x1_slab_copy(0, 0, c).start()

    @pl.when(e == 0)
    def _():
        @pl.when(b + 1 < pl.num_programs(0))
        def _():
            for c in range(nslab):
                x1_slab_copy(b + 1, 1 - slot, c).start()

        for c in range(nslab):
            x1_slab_copy(b, slot, c).wait()

    for c in range(nslab):
        ys_ref[c * ypitch:c * ypitch + cap, :] = ye_ref[0, :, c * GROUP:(c + 1) * GROUP].astype(f32)

    def accumulate(acc):
        def body(i, _):
            r0 = pl.multiple_of(i * SCATTER_ROWS, SCATTER_ROWS)
            toks = [idx_ref[base + r0 + j] for j in range(SCATTER_ROWS)]
            rows = [acc[pl.ds(tok, nslab, stride=pitch), :] for tok in toks]
            for j in range(SCATTER_ROWS):
                acc[pl.ds(toks[j], nslab, stride=pitch), :] = \
                    rows[j] + ys_ref[pl.ds(r0 + j, nslab, stride=ypitch), :]
            return 0

        lax.fori_loop(0, cap // SCATTER_ROWS, body, 0)

        @pl.when(e == ne - 1)
        def _():
            nsteps = s // NORM_ROWS
            batch_row0 = pl.multiple_of(b * s, s)

            def out_copy(step):
                rows = pl.ds(batch_row0 + step * NORM_ROWS, NORM_ROWS)
                return pltpu.make_async_copy(stage_ref.at[step % 2], out_hbm.at[rows, :], sem_out.at[step % 2])

            for step in range(nsteps):
                lo = step * NORM_ROWS
                y = jnp.concatenate([acc[c * pitch + lo:c * pitch + lo + NORM_ROWS, :] for c in range(nslab)],
                                    axis=1)
                if do_norm:
                    ms = jnp.mean(y * y, axis=-1, keepdims=True)
                    y = y * lax.rsqrt(ms + EPS) * g_ref[...]
                if step >= 2:
                    out_copy(step - 2).wait()
                stage_ref[step % 2] = y
                out_copy(step).start()
            for step in range(max(nsteps - 2, 0), nsteps):
                out_copy(step).wait()

    for k in range(2):
        pl.when(slot == k)(functools.partial(accumulate, acc_ref.at[k]))


def _combine(idx_flat, ye, x1, g, nb, s, ne, cap, do_norm):
    d = ye.shape[2]
    nslab = d // GROUP
    return pl.pallas_call(
        functools.partial(_combine_kernel, ne=ne, cap=cap, s=s, do_norm=do_norm),
        grid_spec=pltpu.PrefetchScalarGridSpec(
            num_scalar_prefetch=1,
            grid=(nb, ne),
            in_specs=[
                pl.BlockSpec((1, cap, d), lambda bi, e, idx: (e, bi, 0)),
                pl.BlockSpec(memory_space=pl.ANY),
                pl.BlockSpec((1, d), lambda bi, e, idx: (0, 0)),
            ],
            out_specs=pl.BlockSpec(memory_space=pl.ANY),
            scratch_shapes=[
                pltpu.VMEM((2, nslab * _slab_pitch(s), GROUP), f32),
                pltpu.VMEM((nslab * _slab_pitch(cap), GROUP), f32),
                pltpu.VMEM((2, NORM_ROWS, d), f32),
                pltpu.SemaphoreType.DMA((2,)),
                pltpu.SemaphoreType.DMA((2,)),
            ],
        ),
        out_shape=jax.ShapeDtypeStruct((nb * s, d), f32),
        compiler_params=_params(("arbitrary", "arbitrary")),
        name="combine",
    )(idx_flat, ye, x1, g)


def _channel_dft_table():
    k = jnp.arange(GROUP, dtype=jnp.int32)
    ang = ((k[:, None] * k[None, :]) % GROUP).astype(f32) * (2.0 * math.pi / GROUP)
    return jnp.concatenate([jnp.cos(ang), jnp.sin(ang)], axis=1).astype(bf16)


def _sequence_dft_tables(s, ts, scale):
    half = s // 2
    inner = 64
    k = jnp.arange(half, dtype=jnp.int32)[None, :]
    n1 = jnp.arange(half // inner, dtype=jnp.int32)[:, None] * inner
    n2 = jnp.arange(inner, dtype=jnp.int32)[:, None]
    w = 2.0 * math.pi / s
    a1 = ((n1 * k) % s).astype(f32) * w
    a2 = ((n2 * k) % s).astype(f32) * w
    c1, s1 = jnp.cos(a1)[:, None, :], jnp.sin(a1)[:, None, :]
    c2, s2 = jnp.cos(a2)[None, :, :] * scale, jnp.sin(a2)[None, :, :] * scale
    cp = (c1 * c2 - s1 * s2).reshape(half, half).astype(bf16)
    sm = (-(s1 * c2 + c1 * s2)).reshape(half, half).astype(bf16)
    rows = jnp.arange(ts, dtype=jnp.int32)[:, None]
    cols = jnp.arange(ts, dtype=jnp.int32)[None, :]
    flip = (cols == ts - rows).astype(bf16)
    kk = jnp.arange(half, dtype=jnp.int32)[None, :]
    alt = jnp.where(jnp.arange(8)[:, None] == 0, (1 - 2 * (kk & 1)).astype(f32) * scale, 0.0).astype(bf16)
    fold = min(FOLD_ROWS, half)
    r = jnp.arange(fold, dtype=jnp.int32)
    j1 = ((r[None, :] == fold - r[:, None]) & (r[:, None] >= 1)).astype(bf16)
    return cp, sm, flip, alt, j1


def _pick(n, pref):
    return pref if n % pref == 0 else n


def kernel(x, norm1_g, w_in, sgu_ln_g, sgu_ln_b, w_spatial, b_spatial, w_fourier_out, w_sgu_out, w_out,
           norm2_g, w_router, w_gate_e, w_up_e, w_down_e, final_g):
    nb, s, d = x.shape
    depth = norm1_g.shape[0]
    ne = w_router.shape[2]
    ff = w_gate_e.shape[3]
    cap = CAPACITY_FACTOR * s // ne
    t = nb * s
    assert s % NORM_ROWS == 0 and s & (s - 1) == 0 and cap % GATHER_ROWS == 0 and d % GROUP == 0
    assert (nb * cap) % GROUP == 0

    tm_front = _pick(s, 1024)
    ts_dft = _pick(s // 2, 512)
    tm_post = _pick(s // 2, 1024)
    dft_scale = 1.0 / math.sqrt(s * GROUP)
    cs_tab = _channel_dft_table()
    seq_tables = _sequence_dft_tables(s, ts_dft, dft_scale)
    tmo = _pick(nb * cap, 2048)
    tf = _pick(ff, 512)
    sub = _pick(tmo, 1024)

    x2 = x.reshape(t, d)
    for l in range(depth):
        a, b, ysg, sgf = _front(
            x2, norm1_g[l][None, :], w_in[l].astype(bf16), cs_tab,
            sgu_ln_g[l][None, :], sgu_ln_b[l][None, :], w_spatial[l].astype(bf16),
            b_spatial[l].T, w_sgu_out[l].astype(bf16), tm_front)
        fm_first, fm_second = _seqdft(seq_tables, a, b, nb, s, ts_dft, dft_scale)
        wr = jnp.pad(w_router[l], ((0, 0), (0, ROUTER_LANES - ne)))
        wr_hi = wr.astype(bf16)
        wr_lo = (wr - wr_hi.astype(f32)).astype(bf16)
        wr_split = jnp.concatenate([jnp.concatenate([wr_hi, wr_lo], axis=1),
                                    jnp.concatenate([wr_hi, jnp.zeros_like(wr_hi)], axis=1)], axis=0)
        x1, h2, aff = _post(fm_first, fm_second, sgf, ysg, x2, w_fourier_out[l].astype(bf16),
                            w_out[l].astype(bf16), norm2_g[l][None, :], wr_split, ne, s, tm_post)
        idx, gate = _select(aff, nb, s, cap)
        idx_flat = idx.reshape(-1)
        xe = _gather(idx_flat, h2, nb, s, ne, cap)
        gate_row = jnp.transpose(gate, (1, 0, 2)).reshape(ne, 1, nb * cap)
        ye = _experts(xe, w_gate_e[l], w_up_e[l], w_down_e[l], gate_row, tmo, tf, sub)
        x2 = _combine(idx_flat, ye, x1, final_g[None, :], nb, s, ne, cap, l == depth - 1)
    return x2.reshape(nb, s, d)
```

```python
import functools
import math

import jax
import jax.numpy as jnp
from jax import lax
from jax.experimental import pallas as pl
from jax.experimental.pallas import tpu as pltpu

EPS = 1e-6
GROUP = 128
N_GROUPS = 4
F_WIDTH = N_GROUPS * GROUP
S_WIDTH = N_GROUPS * GROUP
CAPACITY_FACTOR = 2
VMEM_LIMIT_V7X = 56 * 1024 * 1024

f32 = jnp.float32
bf16 = jnp.bfloat16


def _dot(a, b):
    return jnp.dot(a, b, preferred_element_type=f32)


def _params(sem, vmem=VMEM_LIMIT_V7X):
    return pltpu.CompilerParams(dimension_semantics=sem, vmem_limit_bytes=vmem)


FRONT_ROWS = 1024


def _front_kernel(x_ref, g1_ref, win_ref, cs_ref, lng_ref, lnb_ref, ws_ref, bs_ref, wso_ref,
                  a_ref, b_ref, ysg_ref, sgf_ref, mix_ref):
    tm, d = x_ref.shape
    c0 = F_WIDTH + 2 * S_WIDTH
    chunk = min(FRONT_ROWS, tm)
    for r0 in range(0, tm, chunk):
        rows = slice(r0, r0 + chunk)
        x = x_ref[rows, :]
        ms = jnp.mean(x * x, axis=-1, keepdims=True)
        h = (x * lax.rsqrt(ms + EPS) * g1_ref[...]).astype(bf16)

        def proj(lo, hi):
            return _dot(h, win_ref[:, lo:hi])

        zf = proj(0, F_WIDTH)
        for g in range(N_GROUPS):
            cols = slice(g * GROUP, (g + 1) * GROUP)
            ab = _dot(zf[:, cols].astype(bf16), cs_ref[...])
            a_ref[rows, cols] = ab[:, :GROUP].astype(bf16)
            b_ref[rows, cols] = ab[:, GROUP:].astype(bf16)

        u_pre = proj(F_WIDTH, F_WIDTH + S_WIDTH)
        v_pre = proj(F_WIDTH + S_WIDTH, F_WIDTH + 2 * S_WIDTH)
        sgf_ref[rows, :] = jax.nn.sigmoid(proj(c0, c0 + d)).astype(bf16)
        gs_pre = proj(c0 + d, c0 + 2 * d)
        u = jax.nn.gelu(u_pre)
        v = jax.nn.gelu(v_pre)
        for g in range(N_GROUPS):
            cols = slice(g * GROUP, (g + 1) * GROUP)
            vg = v[:, cols]
            mu = jnp.mean(vg, axis=-1, keepdims=True)
            dv = vg - mu
            var = jnp.mean(dv * dv, axis=-1, keepdims=True)
            vnb = (dv * lax.rsqrt(var + EPS) * lng_ref[:, cols] + lnb_ref[:, cols]).astype(bf16)
            for c in range(chunk // GROUP):
                m = _dot(ws_ref[g], vnb[c * GROUP:(c + 1) * GROUP, :]) + bs_ref[:, g:g + 1]
                mix_ref[r0 + c * GROUP:r0 + (c + 1) * GROUP, cols] = m
        sgu = (u * mix_ref[rows, :]).astype(bf16)
        ys = _dot(sgu, wso_ref[...])
        ysg_ref[rows, :] = (jax.nn.sigmoid(gs_pre) * ys).astype(bf16)


def _front(x2, g1, win_b, cs_b, lng, lnb, ws_b, bs_t, wso_b, tm):
    t, d = x2.shape
    kin = win_b.shape[1]
    const = lambda *shape: pl.BlockSpec(shape, lambda i: (0,) * len(shape))
    return pl.pallas_call(
        _front_kernel,
        grid=(t // tm,),
        in_specs=[
            pl.BlockSpec((tm, d), lambda i: (i, 0)),
            const(1, d),
            const(d, kin),
            const(GROUP, 2 * GROUP),
            const(1, S_WIDTH),
            const(1, S_WIDTH),
            const(N_GROUPS, GROUP, GROUP),
            const(GROUP, N_GROUPS),
            const(S_WIDTH, d),
        ],
        out_specs=[
            pl.BlockSpec((tm, F_WIDTH), lambda i: (i, 0)),
            pl.BlockSpec((tm, F_WIDTH), lambda i: (i, 0)),
            pl.BlockSpec((tm, d), lambda i: (i, 0)),
            pl.BlockSpec((tm, d), lambda i: (i, 0)),
        ],
        out_shape=[
            jax.ShapeDtypeStruct((t, F_WIDTH), bf16),
            jax.ShapeDtypeStruct((t, F_WIDTH), bf16),
            jax.ShapeDtypeStruct((t, d), bf16),
            jax.ShapeDtypeStruct((t, d), bf16),
        ],
        scratch_shapes=[pltpu.VMEM((tm, S_WIDTH), f32)],
        compiler_params=_params(("parallel",)),
        name="front",
    )(x2, g1, win_b, cs_b, lng, lnb, ws_b, bs_t, wso_b)


FOLD_ROWS = 256


def _seqdft_kernel(cp_ref, sm_ref, flip_ref, altrow_ref, j1_ref, a_ref, b_ref, fmd_ref, fmm_ref,
                   ap_ref, bm_ref, edge_ref, *, scale):
    i = pl.program_id(1)
    ts = fmd_ref.shape[0]
    s = a_ref.shape[0]
    half = s // 2
    a_mid = scale * a_ref[half:half + 1, :].astype(f32)

    @pl.when(i == 0)
    def _fold_inputs():
        fold = j1_ref.shape[0]
        first_row = lax.broadcasted_iota(jnp.int32, (fold, 1), 0) == 0
        for j in range(half // fold):
            lo = slice(j * fold, (j + 1) * fold)
            mir = slice(s - (j + 1) * fold, s - j * fold)
            a_m = _dot(j1_ref[...], a_ref[mir, :])
            b_m = _dot(j1_ref[...], b_ref[mir, :])
            if j > 0:
                a_m = jnp.where(first_row, a_ref[mir.stop:mir.stop + 1, :].astype(f32), a_m)
                b_m = jnp.where(first_row, b_ref[mir.stop:mir.stop + 1, :].astype(f32), b_m)
            ap_ref[lo, :] = (a_ref[lo, :].astype(f32) + a_m).astype(bf16)
            bm_ref[lo, :] = (b_ref[lo, :].astype(f32) - b_m).astype(bf16)
        first = lax.broadcasted_iota(jnp.int32, (8, 1), 0) == 0
        edge_ref[...] = _dot(altrow_ref[...], ap_ref[...]) + jnp.where(first, a_mid, 0.0)

    p = _dot(cp_ref[...], ap_ref[...])
    q = _dot(sm_ref[...], bm_ref[...])
    row = lax.broadcasted_iota(jnp.int32, (ts, 1), 0)
    alt = (1 - 2 * (row & 1)).astype(f32) * a_mid
    fmd_ref[...] = (p + q + alt).astype(bf16)
    g = p - q + alt
    flipped = _dot(flip_ref[...], g.astype(bf16))
    fmm_ref[...] = jnp.where(row == 0, edge_ref[0:1, :], flipped).astype(bf16)
    edge_ref[...] = g[0:8, :]


def _seqdft(tables, a, b, nb, s, ts, scale):
    half = s // 2
    nh = half // ts
    cp, sm, flip, altrow, j1 = tables
    const = lambda *shape: pl.BlockSpec(shape, lambda bi, i: (0,) * len(shape))
    table_tile = pl.BlockSpec((ts, half), lambda bi, i: (nh - 1 - i, 0))
    return pl.pallas_call(
        functools.partial(_seqdft_kernel, scale=scale),
        grid=(nb, nh),
        in_specs=[
            table_tile,
            table_tile,
            const(ts, ts),
            const(8, half),
            const(*j1.shape),
            pl.BlockSpec((s, F_WIDTH), lambda bi, i: (bi, 0)),
            pl.BlockSpec((s, F_WIDTH), lambda bi, i: (bi, 0)),
        ],
        out_specs=[
            pl.BlockSpec((ts, F_WIDTH), lambda bi, i: (bi * nh + nh - 1 - i, 0)),
            pl.BlockSpec((ts, F_WIDTH), lambda bi, i: (bi * nh + i, 0)),
        ],
        out_shape=[
            jax.ShapeDtypeStruct((nb * half, F_WIDTH), bf16),
            jax.ShapeDtypeStruct((nb * half, F_WIDTH), bf16),
        ],
        scratch_shapes=[
            pltpu.VMEM((half, F_WIDTH), bf16),
            pltpu.VMEM((half, F_WIDTH), bf16),
            pltpu.VMEM((8, F_WIDTH), f32),
        ],
        compiler_params=_params(("parallel", "arbitrary")),
        name="seqdft",
    )(cp, sm, flip, altrow, j1, a, b)


ROUTER_LANES = 128
POST_ROWS = 256


def _post_kernel(fmd_ref, fmm_ref, sgf_ref, ysg_ref, x_ref, wfo_ref, wout_ref, g2_ref, wr_ref,
                 x1_ref, h2_ref, aff_ref, *, tiles_per_half):
    ne = aff_ref.shape[0]
    tm = x_ref.shape[0]
    in_first_half = (pl.program_id(0) % (2 * tiles_per_half)) < tiles_per_half
    chunk = min(POST_ROWS, tm)
    for r0 in range(0, tm, chunk):
        rows = slice(r0, r0 + chunk)
        fm = jnp.where(in_first_half, fmd_ref[rows, :], fmm_ref[rows, :])
        yf = _dot(fm, wfo_ref[...])
        merged = sgf_ref[rows, :].astype(f32) * yf + ysg_ref[rows, :].astype(f32)
        x1 = x_ref[rows, :] + _dot(merged.astype(bf16), wout_ref[...])
        x1_ref[rows, :] = x1
        ms = jnp.mean(x1 * x1, axis=-1, keepdims=True)
        h2 = x1 * lax.rsqrt(ms + EPS) * g2_ref[...]
        h2_ref[rows, :] = h2
        h_hi = h2.astype(bf16)
        h_lo = (h2 - h_hi.astype(f32)).astype(bf16)
        both = _dot(jnp.concatenate([h_hi, h_lo], axis=1), wr_ref[...])
        logits = both[:, :ROUTER_LANES] + both[:, ROUTER_LANES:]
        logits = logits.T[:ne, :]
        mx = jnp.max(logits, axis=0, keepdims=True)
        ex = jnp.exp(logits - mx)
        aff_ref[:, rows] = ex / jnp.sum(ex, axis=0, keepdims=True)


def _post(fm_first, fm_second, sgf, ysg, x2, wfo_b, wout_b, g2, wr_split, ne, s, tm):
    t, d = x2.shape
    th = s // 2 // tm
    const = lambda *shape: pl.BlockSpec(shape, lambda i: (0,) * len(shape))
    tile = lambda w: pl.BlockSpec((tm, w), lambda i: (i, 0))
    first = pl.BlockSpec((tm, F_WIDTH), lambda i: (i // (2 * th) * th + jnp.minimum(i % (2 * th), th - 1), 0))
    second = pl.BlockSpec((tm, F_WIDTH), lambda i: (i // (2 * th) * th + jnp.maximum(i % (2 * th) - th, 0), 0))
    return pl.pallas_call(
        functools.partial(_post_kernel, tiles_per_half=th),
        grid=(t // tm,),
        in_specs=[
            first, second, tile(d), tile(d), tile(d),
            const(F_WIDTH, d),
            const(d, d),
            const(1, d),
            const(2 * d, 2 * ROUTER_LANES),
        ],
        out_specs=[tile(d), tile(d), pl.BlockSpec((ne, tm), lambda i: (0, i))],
        out_shape=[
            jax.ShapeDtypeStruct((t, d), f32),
            jax.ShapeDtypeStruct((t, d), f32),
            jax.ShapeDtypeStruct((ne, t), f32),
        ],
        compiler_params=_params(("parallel",)),
        name="post",
    )(fm_first, fm_second, sgf, ysg, x2, wfo_b, wout_b, g2, wr_split)


def _select_kernel(aff_ref, idx_ref, gate_ref, *, cap):
    ne, s = aff_ref.shape
    nbits = s.bit_length() - 1
    aff = aff_ref[...]
    lane = lax.broadcasted_iota(jnp.int32, (ne, s), 1)

    def enough(cand):
        cnt = jnp.sum((aff >= pltpu.bitcast(cand, f32)).astype(jnp.int32), axis=1, keepdims=True)
        return cnt >= cap

    def two_bit_step(i, thr):
        hi = jnp.left_shift(jnp.int32(1), 29 - 2 * i)
        lo = jnp.left_shift(jnp.int32(1), 28 - 2 * i)
        c1, c2, c3 = thr | lo, thr | hi, thr | hi | lo
        return jnp.where(enough(c3), c3, jnp.where(enough(c2), c2, jnp.where(enough(c1), c1, thr)))

    thr = lax.fori_loop(0, 15, two_bit_step, jnp.zeros((ne, 1), jnp.int32))
    gt = aff >= pltpu.bitcast(thr + 1, f32)
    eq = jnp.logical_and(aff >= pltpu.bitcast(thr, f32), jnp.logical_not(gt))
    need = cap - jnp.sum(gt.astype(jnp.int32), axis=1, keepdims=True)

    def cumsum_tokens(v):
        k = 1
        while k < s:
            v = v + jnp.where(lane >= k, pltpu.roll(v, k, axis=1), 0)
            k *= 2
        return v

    eq_i = eq.astype(jnp.int32)
    surplus = jnp.max(jnp.sum(eq_i, axis=1, keepdims=True) - need) > 0
    keep = lax.cond(surplus, lambda: (cumsum_tokens(eq_i) <= need).astype(jnp.int32), lambda: eq_i)
    sel = jnp.logical_or(gt, jnp.logical_and(eq, keep == 1))
    slot = cumsum_tokens(sel.astype(jnp.int32)) - 1
    word = jnp.where(sel, lane | jnp.left_shift(lane - slot, nbits) | (1 << (2 * nbits)), 0)
    gate = aff
    for k in range(nbits):
        step = 1 << k
        inc_word = pltpu.roll(word, s - step, axis=1)
        inc_gate = pltpu.roll(gate, s - step, axis=1)
        take = (jnp.right_shift(inc_word, nbits + k) & 1) == 1
        leave = (jnp.right_shift(word, nbits + k) & 1) == 1
        word = jnp.where(take, inc_word, jnp.where(leave, 0, word))
        gate = jnp.where(take, inc_gate, gate)
    idx_ref[0] = word[:, :cap] & (s - 1)
    gate_ref[0] = gate[:, :cap]


def _select(aff_t, nb, s, cap):
    ne = aff_t.shape[0]
    return pl.pallas_call(
        functools.partial(_select_kernel, cap=cap),
        grid=(nb,),
        in_specs=[pl.BlockSpec((ne, s), lambda bi: (0, bi))],
        out_specs=[
            pl.BlockSpec((1, ne, cap), lambda bi: (bi, 0, 0)),
            pl.BlockSpec((1, ne, cap), lambda bi: (bi, 0, 0)),
        ],
        out_shape=[
            jax.ShapeDtypeStruct((nb, ne, cap), jnp.int32),
            jax.ShapeDtypeStruct((nb, ne, cap), f32),
        ],
        compiler_params=_params(("parallel",)),
        name="select",
    )(aff_t)


GATHER_ROWS = 64
SCATTER_ROWS = 16
STEP_EXPERTS = 4


def _slab_pitch(rows):
    pitch = rows + (-rows) % 4
    return pitch if (pitch // 4) % 2 else pitch + 4


def _gather_kernel(idx_ref, h2_hbm, xe_ref, buf_ref, stage_ref, sem, *, ne, cap, s):
    b = pl.program_id(0)
    e = pl.program_id(1)
    nslab = xe_ref.shape[2] // GROUP
    pitch = buf_ref.shape[1] // nslab
    spitch = stage_ref.shape[0] // nslab
    slot = b % 2
    group = xe_ref.shape[0]

    def slab_copy(batch, into, c):
        rows = pl.ds(pl.multiple_of(batch * s, s), s)
        return pltpu.make_async_copy(h2_hbm.at[rows, c * GROUP:(c + 1) * GROUP],
                                     buf_ref.at[into, c * pitch:c * pitch + s, :], sem.at[into])

    @pl.when(jnp.logical_and(b == 0, e == 0))
    def _():
        for c in range(nslab):
            slab_copy(0, 0, c).start()

    @pl.when(e == 0)
    def _():
        @pl.when(b + 1 < pl.num_programs(0))
        def _():
            for c in range(nslab):
                slab_copy(b + 1, 1 - slot, c).start()

        for c in range(nslab):
            slab_copy(b, slot, c).wait()

    for k in range(group):
        base = (b * ne + e * group + k) * cap

        def body(i, _, k=k, base=base):
            r0 = pl.multiple_of(i * GATHER_ROWS, GATHER_ROWS)
            for j in range(GATHER_ROWS):
                tok = idx_ref[base + r0 + j]
                stage_ref[pl.ds(j, nslab, stride=spitch), :] = buf_ref[slot, pl.ds(tok, nslab, stride=pitch), :]
            for c in range(nslab):
                xe_ref[k, pl.ds(r0, GATHER_ROWS), c * GROUP:(c + 1) * GROUP] = \
                    stage_ref[c * spitch:c * spitch + GATHER_ROWS, :].astype(bf16)
            return 0

        lax.fori_loop(0, cap // GATHER_ROWS, body, 0)


def _gather(idx_flat, h2, nb, s, ne, cap):
    d = h2.shape[1]
    nslab = d // GROUP
    group = _pick(ne, STEP_EXPERTS)
    return pl.pallas_call(
        functools.partial(_gather_kernel, ne=ne, cap=cap, s=s),
        grid_spec=pltpu.PrefetchScalarGridSpec(
            num_scalar_prefetch=1,
            grid=(nb, ne // group),
            in_specs=[pl.BlockSpec(memory_space=pl.ANY)],
            out_specs=pl.BlockSpec((group, cap, d), lambda bi, e, idx: (e, bi, 0)),
            scratch_shapes=[
                pltpu.VMEM((2, nslab * _slab_pitch(s), GROUP), f32),
                pltpu.VMEM((nslab * _slab_pitch(GATHER_ROWS), GROUP), f32),
                pltpu.SemaphoreType.DMA((2,)),
            ],
        ),
        out_shape=jax.ShapeDtypeStruct((ne, nb * cap, d), bf16),
        compiler_params=_params(("arbitrary", "arbitrary")),
        name="gather",
    )(idx_flat, h2)


def _expert_kernel(x_ref, wg_ref, wu_ref, wd_ref, gate_ref, y_ref, acc_ref, *, sub, nf):
    f = pl.program_id(2)
    tmo, d = acc_ref.shape

    def partial_out(rows, w_gate, w_up, w_down):
        xs = x_ref[0, rows, :]
        a = _dot(xs, w_gate)
        g = _dot(xs, w_up)
        hm = (a * jax.nn.sigmoid(a) * g).astype(bf16)
        return _dot(hm, w_down)

    def store_gated(rows, y):
        for c in range(rows.start // GROUP, rows.stop // GROUP):
            chunk = slice(c * GROUP, (c + 1) * GROUP)
            g_col = jnp.broadcast_to(gate_ref[0, :, chunk], (GROUP, GROUP)).T
            y_c = y[chunk.start - rows.start:chunk.stop - rows.start, :]
            y_ref[0, chunk, :] = (y_c * jnp.tile(g_col, (1, d // GROUP))).astype(bf16)

    def run(first, last):
        weights = [w_ref[0].astype(bf16) for w_ref in (wg_ref, wu_ref, wd_ref)]
        for r0 in range(0, tmo, sub):
            rows = slice(r0, r0 + sub)
            y = partial_out(rows, *weights)
            if not first:
                y = acc_ref[rows, :] + y
            if last:
                store_gated(rows, y)
            else:
                acc_ref[rows, :] = y

    if nf == 1:
        run(True, True)
    else:
        pl.when(f == 0)(lambda: run(True, False))
        pl.when(f == nf - 1)(lambda: run(False, True))
        if nf > 2:
            pl.when(jnp.logical_and(f > 0, f < nf - 1))(lambda: run(False, False))


def _experts(xe, wg, wu, wd, gate_row, tmo, tf, sub):
    ne, m, d = xe.shape
    ff = wg.shape[2]
    return pl.pallas_call(
        functools.partial(_expert_kernel, sub=sub, nf=ff // tf),
        grid=(ne, m // tmo, ff // tf),
        in_specs=[
            pl.BlockSpec((1, tmo, d), lambda e, mi, fi: (e, mi, 0)),
            pl.BlockSpec((1, d, tf), lambda e, mi, fi: (e, 0, fi)),
            pl.BlockSpec((1, d, tf), lambda e, mi, fi: (e, 0, fi)),
            pl.BlockSpec((1, tf, d), lambda e, mi, fi: (e, fi, 0)),
            pl.BlockSpec((1, 1, tmo), lambda e, mi, fi: (e, 0, mi)),
        ],
        out_specs=pl.BlockSpec((1, tmo, d), lambda e, mi, fi: (e, mi, 0)),
        out_shape=jax.ShapeDtypeStruct((ne, m, d), bf16),
        scratch_shapes=[pltpu.VMEM((tmo, d), f32)],
        compiler_params=_params(("parallel", "parallel", "arbitrary")),
        name="experts",
    )(xe, wg, wu, wd, gate_row)


NORM_ROWS = 256


def _combine_kernel(idx_ref, ye_ref, x1_hbm, g_ref, out_hbm, acc_ref, ys_ref, stage_ref, sem_x1, sem_out,
                    *, ne, cap, s, do_norm):
    b = pl.program_id(0)
    e = pl.program_id(1)
    d = stage_ref.shape[2]
    nslab = d // GROUP
    pitch = acc_ref.shape[1] // nslab
    ypitch = ys_ref.shape[0] // nslab
    slot = b % 2
    group = ye_ref.shape[0]

    def x1_slab_copy(batch, into, c):
        rows = pl.ds(pl.multiple_of(batch * s, s), s)
        return pltpu.make_async_copy(x1_hbm.at[rows, c * GROUP:(c + 1) * GROUP],
                                     acc_ref.at[into, c * pitch:c * pitch + s, :], sem_x1.at[into])

    @pl.when(jnp.logical_and(b == 0, e == 0))
    def _():
        for c in range(nslab):
            x1_slab_copy(0, 0, c).start()

    @pl.when(e == 0)
    def _():
        @pl.when(b + 1 < pl.num_programs(0))
        def _():
            for c in range(nslab):
                x1_slab_copy(b + 1, 1 - slot, c).start()

        for c in range(nslab):
            x1_slab_copy(b, slot, c).wait()

    def accumulate(acc):
        for k in range(group):
            base = (b * ne + e * group + k) * cap
            for c in range(nslab):
                ys_ref[c * ypitch:c * ypitch + cap, :] = ye_ref[k, :, c * GROUP:(c + 1) * GROUP].astype(f32)

            def body(i, _, base=base):
                r0 = pl.multiple_of(i * SCATTER_ROWS, SCATTER_ROWS)
                toks = [idx_ref[base + r0 + j] for j in range(SCATTER_ROWS)]
                rows = [acc[pl.ds(tok, nslab, stride=pitch), :] for tok in toks]
                for j in range(SCATTER_ROWS):
                    acc[pl.ds(toks[j], nslab, stride=pitch), :] = \
                        rows[j] + ys_ref[pl.ds(r0 + j, nslab, stride=ypitch), :]
                return 0

            lax.fori_loop(0, cap // SCATTER_ROWS, body, 0)

        @pl.when(e == ne // group - 1)
        def _():
            nsteps = s // NORM_ROWS
            batch_row0 = pl.multiple_of(b * s, s)

            def out_copy(step):
                rows = pl.ds(batch_row0 + step * NORM_ROWS, NORM_ROWS)
                return pltpu.make_async_copy(stage_ref.at[step % 2], out_hbm.at[rows, :], sem_out.at[step % 2])

            for step in range(nsteps):
                lo = step * NORM_ROWS
                y = jnp.concatenate([acc[c * pitch + lo:c * pitch + lo + NORM_ROWS, :] for c in range(nslab)],
                                    axis=1)
                if do_norm:
                    ms = jnp.mean(y * y, axis=-1, keepdims=True)
                    y = y * lax.rsqrt(ms + EPS) * g_ref[...]
                if step >= 2:
                    out_copy(step - 2).wait()
                stage_ref[step % 2] = y
                out_copy(step).start()
            for step in range(max(nsteps - 2, 0), nsteps):
                out_copy(step).wait()

    for k in range(2):
        pl.when(slot == k)(functools.partial(accumulate, acc_ref.at[k]))


def _combine(idx_flat, ye, x1, g, nb, s, ne, cap, do_norm):
    d = ye.shape[2]
    nslab = d // GROUP
    group = _pick(ne, STEP_EXPERTS)
    return pl.pallas_call(
        functools.partial(_combine_kernel, ne=ne, cap=cap, s=s, do_norm=do_norm),
        grid_spec=pltpu.PrefetchScalarGridSpec(
            num_scalar_prefetch=1,
            grid=(nb, ne // group),
            in_specs=[
                pl.BlockSpec((group, cap, d), lambda bi, e, idx: (e, bi, 0)),
                pl.BlockSpec(memory_space=pl.ANY),
                pl.BlockSpec((1, d), lambda bi, e, idx: (0, 0)),
            ],
            out_specs=pl.BlockSpec(memory_space=pl.ANY),
            scratch_shapes=[
                pltpu.VMEM((2, nslab * _slab_pitch(s), GROUP), f32),
                pltpu.VMEM((nslab * _slab_pitch(cap), GROUP), f32),
                pltpu.VMEM((2, NORM_ROWS, d), f32),
                pltpu.SemaphoreType.DMA((2,)),
                pltpu.SemaphoreType.DMA((2,)),
            ],
        ),
        out_shape=jax.ShapeDtypeStruct((nb * s, d), f32),
        compiler_params=_params(("arbitrary", "arbitrary")),
        name="combine",
    )(idx_flat, ye, x1, g)


def _channel_dft_table():
    k = jnp.arange(GROUP, dtype=jnp.int32)
    ang = ((k[:, None] * k[None, :]) % GROUP).astype(f32) * (2.0 * math.pi / GROUP)
    return jnp.concatenate([jnp.cos(ang), jnp.sin(ang)], axis=1).astype(bf16)


def _sequence_dft_tables(s, ts, scale):
    half = s // 2
    inner = 64
    k = jnp.arange(half, dtype=jnp.int32)[None, :]
    n1 = jnp.arange(half // inner, dtype=jnp.int32)[:, None] * inner
    n2 = jnp.arange(inner, dtype=jnp.int32)[:, None]
    w = 2.0 * math.pi / s
    a1 = ((n1 * k) % s).astype(f32) * w
    a2 = ((n2 * k) % s).astype(f32) * w
    c1, s1 = jnp.cos(a1)[:, None, :], jnp.sin(a1)[:, None, :]
    c2, s2 = jnp.cos(a2)[None, :, :] * scale, jnp.sin(a2)[None, :, :] * scale
    cp = (c1 * c2 - s1 * s2).reshape(half, half).astype(bf16)
    sm = (-(s1 * c2 + c1 * s2)).reshape(half, half).astype(bf16)
    rows = jnp.arange(ts, dtype=jnp.int32)[:, None]
    cols = jnp.arange(ts, dtype=jnp.int32)[None, :]
    flip = (cols == ts - rows).astype(bf16)
    kk = jnp.arange(half, dtype=jnp.int32)[None, :]
    alt = jnp.where(jnp.arange(8)[:, None] == 0, (1 - 2 * (kk & 1)).astype(f32) * scale, 0.0).astype(bf16)
    fold = min(FOLD_ROWS, half)
    r = jnp.arange(fold, dtype=jnp.int32)
    j1 = ((r[None, :] == fold - r[:, None]) & (r[:, None] >= 1)).astype(bf16)
    return cp, sm, flip, alt, j1


def _pick(n, pref):
    return pref if n % pref == 0 else n


def kernel(x, norm1_g, w_in, sgu_ln_g, sgu_ln_b, w_spatial, b_spatial, w_fourier_out, w_sgu_out, w_out,
           norm2_g, w_router, w_gate_e, w_up_e, w_down_e, final_g):
    nb, s, d = x.shape
    depth = norm1_g.shape[0]
    ne = w_router.shape[2]
    ff = w_gate_e.shape[3]
    cap = CAPACITY_FACTOR * s // ne
    t = nb * s
    assert s % NORM_ROWS == 0 and s & (s - 1) == 0 and cap % GATHER_ROWS == 0 and d % GROUP == 0
    assert (nb * cap) % GROUP == 0

    tm_front = _pick(s, 1024)
    ts_dft = _pick(s // 2, 512)
    tm_post = _pick(s // 2, 1024)
    dft_scale = 1.0 / math.sqrt(s * GROUP)
    cs_tab = _channel_dft_table()
    seq_tables = _sequence_dft_tables(s, ts_dft, dft_scale)
    tmo = _pick(nb * cap, 2048)
    tf = _pick(ff, 512)
    sub = _pick(tmo, 1024)

    x2 = x.reshape(t, d)
    for l in range(depth):
        a, b, ysg, sgf = _front(
            x2, norm1_g[l][None, :], w_in[l].astype(bf16), cs_tab,
            sgu_ln_g[l][None, :], sgu_ln_b[l][None, :], w_spatial[l].astype(bf16),
            b_spatial[l].T, w_sgu_out[l].astype(bf16), tm_front)
        fm_first, fm_second = _seqdft(seq_tables, a, b, nb, s, ts_dft, dft_scale)
        wr = jnp.pad(w_router[l], ((0, 0), (0, ROUTER_LANES - ne)))
        wr_hi = wr.astype(bf16)
        wr_lo = (wr - wr_hi.astype(f32)).astype(bf16)
        wr_split = jnp.concatenate([jnp.concatenate([wr_hi, wr_lo], axis=1),
                                    jnp.concatenate([wr_hi, jnp.zeros_like(wr_hi)], axis=1)], axis=0)
        x1, h2, aff = _post(fm_first, fm_second, sgf, ysg, x2, w_fourier_out[l].astype(bf16),
                            w_out[l].astype(bf16), norm2_g[l][None, :], wr_split, ne, s, tm_post)
        idx, gate = _select(aff, nb, s, cap)
        idx_flat = idx.reshape(-1)
        xe = _gather(idx_flat, h2, nb, s, ne, cap)
        gate_row = jnp.transpose(gate, (1, 0, 2)).reshape(ne, 1, nb * cap)
        ye = _experts(xe, w_gate_e[l], w_up_e[l], w_down_e[l], gate_row, tmo, tf, sub)
        x2 = _combine(idx_flat, ye, x1, final_g[None, :], nb, s, ne, cap, l == depth - 1)
    return x2.reshape(nb, s, d)
```

```python
import functools
import math

import jax
import jax.numpy as jnp
from jax import lax
from jax.experimental import pallas as pl
from jax.experimental.pallas import tpu as pltpu

EPS = 1e-6
GROUP = 128
N_GROUPS = 4
F_WIDTH = N_GROUPS * GROUP
S_WIDTH = N_GROUPS * GROUP
CAPACITY_FACTOR = 2
VMEM_LIMIT_V7X = 56 * 1024 * 1024

f32 = jnp.float32
bf16 = jnp.bfloat16


def _dot(a, b):
    return jnp.dot(a, b, preferred_element_type=f32)


def _params(sem, vmem=VMEM_LIMIT_V7X):
    return pltpu.CompilerParams(dimension_semantics=sem, vmem_limit_bytes=vmem)


FRONT_ROWS = 1024


def _front_kernel(x_ref, g1_ref, win_ref, cs_ref, lng_ref, lnb_ref, ws_ref, bs_ref, wso_ref,
                  a_ref, b_ref, ysg_ref, sgf_ref, mix_ref):
    tm, d = x_ref.shape
    c0 = F_WIDTH + 2 * S_WIDTH
    chunk = min(FRONT_ROWS, tm)
    for r0 in range(0, tm, chunk):
        rows = slice(r0, r0 + chunk)
        x = x_ref[rows, :]
        ms = jnp.mean(x * x, axis=-1, keepdims=True)
        h = (x * lax.rsqrt(ms + EPS) * g1_ref[...]).astype(bf16)

        def proj(lo, hi):
            return _dot(h, win_ref[:, lo:hi])

        zf = proj(0, F_WIDTH)
        for g in range(N_GROUPS):
            cols = slice(g * GROUP, (g + 1) * GROUP)
            ab = _dot(zf[:, cols].astype(bf16), cs_ref[...])
            a_ref[rows, cols] = ab[:, :GROUP].astype(bf16)
            b_ref[rows, cols] = ab[:, GROUP:].astype(bf16)

        u_pre = proj(F_WIDTH, F_WIDTH + S_WIDTH)
        v_pre = proj(F_WIDTH + S_WIDTH, F_WIDTH + 2 * S_WIDTH)
        sgf_ref[rows, :] = jax.nn.sigmoid(proj(c0, c0 + d)).astype(bf16)
        gs_pre = proj(c0 + d, c0 + 2 * d)
        u = jax.nn.gelu(u_pre)
        v = jax.nn.gelu(v_pre)
        for g in range(N_GROUPS):
            cols = slice(g * GROUP, (g + 1) * GROUP)
            vg = v[:, cols]
            mu = jnp.mean(vg, axis=-1, keepdims=True)
            dv = vg - mu
            var = jnp.mean(dv * dv, axis=-1, keepdims=True)
            vnb = (dv * lax.rsqrt(var + EPS) * lng_ref[:, cols] + lnb_ref[:, cols]).astype(bf16)
            for c in range(chunk // GROUP):
                m = _dot(ws_ref[g], vnb[c * GROUP:(c + 1) * GROUP, :]) + bs_ref[:, g:g + 1]
                mix_ref[r0 + c * GROUP:r0 + (c + 1) * GROUP, cols] = m
        sgu = (u * mix_ref[rows, :]).astype(bf16)
        ys = _dot(sgu, wso_ref[...])
        ysg_ref[rows, :] = (jax.nn.sigmoid(gs_pre) * ys).astype(bf16)


def _front(x2, g1, win_b, cs_b, lng, lnb, ws_b, bs_t, wso_b, tm):
    t, d = x2.shape
    kin = win_b.shape[1]
    const = lambda *shape: pl.BlockSpec(shape, lambda i: (0,) * len(shape))
    return pl.pallas_call(
        _front_kernel,
        grid=(t // tm,),
        in_specs=[
            pl.BlockSpec((tm, d), lambda i: (i, 0)),
            const(1, d),
            const(d, kin),
            const(GROUP, 2 * GROUP),
            const(1, S_WIDTH),
            const(1, S_WIDTH),
            const(N_GROUPS, GROUP, GROUP),
            const(GROUP, N_GROUPS),
            const(S_WIDTH, d),
        ],
        out_specs=[
            pl.BlockSpec((tm, F_WIDTH), lambda i: (i, 0)),
            pl.BlockSpec((tm, F_WIDTH), lambda i: (i, 0)),
            pl.BlockSpec((tm, d), lambda i: (i, 0)),
            pl.BlockSpec((tm, d), lambda i: (i, 0)),
        ],
        out_shape=[
            jax.ShapeDtypeStruct((t, F_WIDTH), bf16),
            jax.ShapeDtypeStruct((t, F_WIDTH), bf16),
            jax.ShapeDtypeStruct((t, d), bf16),
            jax.ShapeDtypeStruct((t, d), bf16),
        ],
        scratch_shapes=[pltpu.VMEM((tm, S_WIDTH), f32)],
        compiler_params=_params(("parallel",)),
        name="front",
    )(x2, g1, win_b, cs_b, lng, lnb, ws_b, bs_t, wso_b)


FOLD_ROWS = 256


def _seqdft_kernel(cp_ref, sm_ref, flip_ref, altrow_ref, j1_ref, a_ref, b_ref, fmd_ref, fmm_ref,
                   ap_ref, bm_ref, edge_ref, *, scale):
    i = pl.program_id(1)
    ts = fmd_ref.shape[0]
    s = a_ref.shape[0]
    half = s // 2
    a_mid = scale * a_ref[half:half + 1, :].astype(f32)

    @pl.when(i == 0)
    def _fold_inputs():
        fold = j1_ref.shape[0]
        first_row = lax.broadcasted_iota(jnp.int32, (fold, 1), 0) == 0
        for j in range(half // fold):
            lo = slice(j * fold, (j + 1) * fold)
            mir = slice(s - (j + 1) * fold, s - j * fold)
            a_m = _dot(j1_ref[...], a_ref[mir, :])
            b_m = _dot(j1_ref[...], b_ref[mir, :])
            if j > 0:
                a_m = jnp.where(first_row, a_ref[mir.stop:mir.stop + 1, :].astype(f32), a_m)
                b_m = jnp.where(first_row, b_ref[mir.stop:mir.stop + 1, :].astype(f32), b_m)
            ap_ref[lo, :] = (a_ref[lo, :].astype(f32) + a_m).astype(bf16)
            bm_ref[lo, :] = (b_ref[lo, :].astype(f32) - b_m).astype(bf16)
        first = lax.broadcasted_iota(jnp.int32, (8, 1), 0) == 0
        edge_ref[...] = _dot(altrow_ref[...], ap_ref[...]) + jnp.where(first, a_mid, 0.0)

    p = _dot(cp_ref[...], ap_ref[...])
    q = _dot(sm_ref[...], bm_ref[...])
    row = lax.broadcasted_iota(jnp.int32, (ts, 1), 0)
    alt = (1 - 2 * (row & 1)).astype(f32) * a_mid
    fmd_ref[...] = (p + q + alt).astype(bf16)
    g = p - q + alt
    flipped = _dot(flip_ref[...], g.astype(bf16))
    fmm_ref[...] = jnp.where(row == 0, edge_ref[0:1, :], flipped).astype(bf16)
    edge_ref[...] = g[0:8, :]


def _seqdft(tables, a, b, nb, s, ts, scale):
    half = s // 2
    nh = half // ts
    cp, sm, flip, altrow, j1 = tables
    const = lambda *shape: pl.BlockSpec(shape, lambda bi, i: (0,) * len(shape))
    table_tile = pl.BlockSpec((ts, half), lambda bi, i: (nh - 1 - i, 0))
    return pl.pallas_call(
        functools.partial(_seqdft_kernel, scale=scale),
        grid=(nb, nh),
        in_specs=[
            table_tile,
            table_tile,
            const(ts, ts),
            const(8, half),
            const(*j1.shape),
            pl.BlockSpec((s, F_WIDTH), lambda bi, i: (bi, 0)),
            pl.BlockSpec((s, F_WIDTH), lambda bi, i: (bi, 0)),
        ],
        out_specs=[
            pl.BlockSpec((ts, F_WIDTH), lambda bi, i: (bi * nh + nh - 1 - i, 0)),
            pl.BlockSpec((ts, F_WIDTH), lambda bi, i: (bi * nh + i, 0)),
        ],
        out_shape=[
            jax.ShapeDtypeStruct((nb * half, F_WIDTH), bf16),
            jax.ShapeDtypeStruct((nb * half, F_WIDTH), bf16),
        ],
        scratch_shapes=[
            pltpu.VMEM((half, F_WIDTH), bf16),
            pltpu.VMEM((half, F_WIDTH), bf16),
            pltpu.VMEM((8, F_WIDTH), f32),
        ],
        compiler_params=_params(("parallel", "arbitrary")),
        name="seqdft",
    )(cp, sm, flip, altrow, j1, a, b)


ROUTER_LANES = 128
POST_ROWS = 256


def _post_kernel(fmd_ref, fmm_ref, sgf_ref, ysg_ref, x_ref, wfo_ref, wout_ref, g2_ref, wr_ref,
                 x1_ref, h2_ref, aff_ref, *, tiles_per_half):
    ne = aff_ref.shape[0]
    tm = x_ref.shape[0]
    in_first_half = (pl.program_id(0) % (2 * tiles_per_half)) < tiles_per_half
    chunk = min(POST_ROWS, tm)
    for r0 in range(0, tm, chunk):
        rows = slice(r0, r0 + chunk)
        fm = jnp.where(in_first_half, fmd_ref[rows, :], fmm_ref[rows, :])
        yf = _dot(fm, wfo_ref[...])
        merged = sgf_ref[rows, :].astype(f32) * yf + ysg_ref[rows, :].astype(f32)
        x1 = x_ref[rows, :] + _dot(merged.astype(bf16), wout_ref[...])
        x1_ref[rows, :] = x1
        ms = jnp.mean(x1 * x1, axis=-1, keepdims=True)
        h2 = x1 * lax.rsqrt(ms + EPS) * g2_ref[...]
        h2_ref[rows, :] = h2
        h_hi = h2.astype(bf16)
        h_lo = (h2 - h_hi.astype(f32)).astype(bf16)
        both = _dot(jnp.concatenate([h_hi, h_lo], axis=1), wr_ref[...])
        logits = both[:, :ROUTER_LANES] + both[:, ROUTER_LANES:]
        logits = logits.T[:ne, :]
        mx = jnp.max(logits, axis=0, keepdims=True)
        ex = jnp.exp(logits - mx)
        aff_ref[:, rows] = ex / jnp.sum(ex, axis=0, keepdims=True)


def _post(fm_first, fm_second, sgf, ysg, x2, wfo_b, wout_b, g2, wr_split, ne, s, tm):
    t, d = x2.shape
    th = s // 2 // tm
    const = lambda *shape: pl.BlockSpec(shape, lambda i: (0,) * len(shape))
    tile = lambda w: pl.BlockSpec((tm, w), lambda i: (i, 0))
    first = pl.BlockSpec((tm, F_WIDTH), lambda i: (i // (2 * th) * th + jnp.minimum(i % (2 * th), th - 1), 0))
    second = pl.BlockSpec((tm, F_WIDTH), lambda i: (i // (2 * th) * th + jnp.maximum(i % (2 * th) - th, 0), 0))
    return pl.pallas_call(
        functools.partial(_post_kernel, tiles_per_half=th),
        grid=(t // tm,),
        in_specs=[
            first, second, tile(d), tile(d), tile(d),
            const(F_WIDTH, d),
            const(d, d),
            const(1, d),
            const(2 * d, 2 * ROUTER_LANES),
        ],
        out_specs=[tile(d), tile(d), pl.BlockSpec((ne, tm), lambda i: (0, i))],
        out_shape=[
            jax.ShapeDtypeStruct((t, d), f32),
            jax.ShapeDtypeStruct((t, d), f32),
            jax.ShapeDtypeStruct((ne, t), f32),
        ],
        compiler_params=_params(("parallel",)),
        name="post",
    )(fm_first, fm_second, sgf, ysg, x2, wfo_b, wout_b, g2, wr_split)


def _select_kernel(aff_ref, idx_ref, gate_ref, *, cap):
    ne, s = aff_ref.shape
    nbits = s.bit_length() - 1
    aff = aff_ref[...]
    lane = lax.broadcasted_iota(jnp.int32, (ne, s), 1)

    def enough(cand):
        cnt = jnp.sum((aff >= pltpu.bitcast(cand, f32)).astype(jnp.int32), axis=1, keepdims=True)
        return cnt >= cap

    def two_bit_step(i, thr):
        hi = jnp.left_shift(jnp.int32(1), 29 - 2 * i)
        lo = jnp.left_shift(jnp.int32(1), 28 - 2 * i)
        c1, c2, c3 = thr | lo, thr | hi, thr | hi | lo
        return jnp.where(enough(c3), c3, jnp.where(enough(c2), c2, jnp.where(enough(c1), c1, thr)))

    thr = lax.fori_loop(0, 15, two_bit_step, jnp.zeros((ne, 1), jnp.int32))
    gt = aff >= pltpu.bitcast(thr + 1, f32)
    eq = jnp.logical_and(aff >= pltpu.bitcast(thr, f32), jnp.logical_not(gt))
    need = cap - jnp.sum(gt.astype(jnp.int32), axis=1, keepdims=True)

    def cumsum_tokens(v):
        k = 1
        while k < s:
            v = v + jnp.where(lane >= k, pltpu.roll(v, k, axis=1), 0)
            k *= 2
        return v

    eq_i = eq.astype(jnp.int32)
    surplus = jnp.max(jnp.sum(eq_i, axis=1, keepdims=True) - need) > 0
    keep = lax.cond(surplus, lambda: (cumsum_tokens(eq_i) <= need).astype(jnp.int32), lambda: eq_i)
    sel = jnp.logical_or(gt, jnp.logical_and(eq, keep == 1))
    slot = cumsum_tokens(sel.astype(jnp.int32)) - 1
    word = jnp.where(sel, lane | jnp.left_shift(lane - slot, nbits) | (1 << (2 * nbits)), 0)
    gate = aff
    for k in range(nbits):
        step = 1 << k
        inc_word = pltpu.roll(word, s - step, axis=1)
        inc_gate = pltpu.roll(gate, s - step, axis=1)
        take = (jnp.right_shift(inc_word, nbits + k) & 1) == 1
        leave = (jnp.right_shift(word, nbits + k) & 1) == 1
        word = jnp.where(take, inc_word, jnp.where(leave, 0, word))
        gate = jnp.where(take, inc_gate, gate)
    idx_ref[0] = word[:, :cap] & (s - 1)
    gate_ref[0] = gate[:, :cap]


def _select(aff_t, nb, s, cap):
    ne = aff_t.shape[0]
    return pl.pallas_call(
        functools.partial(_select_kernel, cap=cap),
        grid=(nb,),
        in_specs=[pl.BlockSpec((ne, s), lambda bi: (0, bi))],
        out_specs=[
            pl.BlockSpec((1, ne, cap), lambda bi: (bi, 0, 0)),
            pl.BlockSpec((1, ne, cap), lambda bi: (bi, 0, 0)),
        ],
        out_shape=[
            jax.ShapeDtypeStruct((nb, ne, cap), jnp.int32),
            jax.ShapeDtypeStruct((nb, ne, cap), f32),
        ],
        compiler_params=_params(("parallel",)),
        name="select",
    )(aff_t)


GATHER_ROWS = 64
SCATTER_ROWS = 16
STEP_EXPERTS = 8


def _slab_pitch(rows):
    pitch = rows + (-rows) % 4
    return pitch if (pitch // 4) % 2 else pitch + 4


def _gather_kernel(idx_ref, h2_hbm, xe_ref, buf_ref, stage_ref, sem, *, ne, cap, s):
    b = pl.program_id(0)
    e = pl.program_id(1)
    nslab = xe_ref.shape[2] // GROUP
    pitch = buf_ref.shape[1] // nslab
    spitch = stage_ref.shape[0] // nslab
    slot = b % 2
    group = xe_ref.shape[0]

    def slab_copy(batch, into, c):
        rows = pl.ds(pl.multiple_of(batch * s, s), s)
        return pltpu.make_async_copy(h2_hbm.at[rows, c * GROUP:(c + 1) * GROUP],
                                     buf_ref.at[into, c * pitch:c * pitch + s, :], sem.at[into])

    @pl.when(jnp.logical_and(b == 0, e == 0))
    def _():
        for c in range(nslab):
            slab_copy(0, 0, c).start()

    @pl.when(e == 0)
    def _():
        @pl.when(b + 1 < pl.num_programs(0))
        def _():
            for c in range(nslab):
                slab_copy(b + 1, 1 - slot, c).start()

        for c in range(nslab):
            slab_copy(b, slot, c).wait()

    for k in range(group):
        base = (b * ne + e * group + k) * cap

        def body(i, _, k=k, base=base):
            r0 = pl.multiple_of(i * GATHER_ROWS, GATHER_ROWS)
            for j in range(GATHER_ROWS):
                tok = idx_ref[base + r0 + j]
                stage_ref[pl.ds(j, nslab, stride=spitch), :] = buf_ref[slot, pl.ds(tok, nslab, stride=pitch), :]
            for c in range(nslab):
                xe_ref[k, pl.ds(r0, GATHER_ROWS), c * GROUP:(c + 1) * GROUP] = \
                    stage_ref[c * spitch:c * spitch + GATHER_ROWS, :].astype(bf16)
            return 0

        lax.fori_loop(0, cap // GATHER_ROWS, body, 0)


def _gather(idx_flat, h2, nb, s, ne, cap):
    d = h2.shape[1]
    nslab = d // GROUP
    group = _pick(ne, STEP_EXPERTS)
    return pl.pallas_call(
        functools.partial(_gather_kernel, ne=ne, cap=cap, s=s),
        grid_spec=pltpu.PrefetchScalarGridSpec(
            num_scalar_prefetch=1,
            grid=(nb, ne // group),
            in_specs=[pl.BlockSpec(memory_space=pl.ANY)],
            out_specs=pl.BlockSpec((group, cap, d), lambda bi, e, idx: (e, bi, 0)),
            scratch_shapes=[
                pltpu.VMEM((2, nslab * _slab_pitch(s), GROUP), f32),
                pltpu.VMEM((nslab * _slab_pitch(GATHER_ROWS), GROUP), f32),
                pltpu.SemaphoreType.DMA((2,)),
            ],
        ),
        out_shape=jax.ShapeDtypeStruct((ne, nb * cap, d), bf16),
        compiler_params=_params(("arbitrary", "arbitrary")),
        name="gather",
    )(idx_flat, h2)


def _expert_kernel(x_ref, wg_ref, wu_ref, wd_ref, gate_ref, y_ref, acc_ref, *, sub, nf):
    f = pl.program_id(2)
    tmo, d = acc_ref.shape

    def partial_out(rows, w_gate, w_up, w_down):
        xs = x_ref[0, rows, :]
        a = _dot(xs, w_gate)
        g = _dot(xs, w_up)
        hm = (a * jax.nn.sigmoid(a) * g).astype(bf16)
        return _dot(hm, w_down)

    def store_gated(rows, y):
        for c in range(rows.start // GROUP, rows.stop // GROUP):
            chunk = slice(c * GROUP, (c + 1) * GROUP)
            g_col = jnp.broadcast_to(gate_ref[0, :, chunk], (GROUP, GROUP)).T
            y_c = y[chunk.start - rows.start:chunk.stop - rows.start, :]
            y_ref[0, chunk, :] = (y_c * jnp.tile(g_col, (1, d // GROUP))).astype(bf16)

    def run(first, last):
        weights = [w_ref[0].astype(bf16) for w_ref in (wg_ref, wu_ref, wd_ref)]
        for r0 in range(0, tmo, sub):
            rows = slice(r0, r0 + sub)
            y = partial_out(rows, *weights)
            if not first:
                y = acc_ref[rows, :] + y
            if last:
                store_gated(rows, y)
            else:
                acc_ref[rows, :] = y

    if nf == 1:
        run(True, True)
    else:
        pl.when(f == 0)(lambda: run(True, False))
        pl.when(f == nf - 1)(lambda: run(False, True))
        if nf > 2:
            pl.when(jnp.logical_and(f > 0, f < nf - 1))(lambda: run(False, False))


def _experts(xe, wg, wu, wd, gate_row, tmo, tf, sub):
    ne, m, d = xe.shape
    ff = wg.shape[2]
    return pl.pallas_call(
        functools.partial(_expert_kernel, sub=sub, nf=ff // tf),
        grid=(ne, m // tmo, ff // tf),
        in_specs=[
            pl.BlockSpec((1, tmo, d), lambda e, mi, fi: (e, mi, 0)),
            pl.BlockSpec((1, d, tf), lambda e, mi, fi: (e, 0, fi)),
            pl.BlockSpec((1, d, tf), lambda e, mi, fi: (e, 0, fi)),
            pl.BlockSpec((1, tf, d), lambda e, mi, fi: (e, fi, 0)),
            pl.BlockSpec((1, 1, tmo), lambda e, mi, fi: (e, 0, mi)),
        ],
        out_specs=pl.BlockSpec((1, tmo, d), lambda e, mi, fi: (e, mi, 0)),
        out_shape=jax.ShapeDtypeStruct((ne, m, d), bf16),
        scratch_shapes=[pltpu.VMEM((tmo, d), f32)],
        compiler_params=_params(("parallel", "parallel", "arbitrary")),
        name="experts",
    )(xe, wg, wu, wd, gate_row)


NORM_ROWS = 256


def _combine_kernel(idx_ref, ye_ref, x1_hbm, g_ref, out_hbm, acc_ref, ys_ref, stage_ref, sem_x1, sem_out,
                    *, ne, cap, s, do_norm):
    b = pl.program_id(0)
    e = pl.program_id(1)
    d = stage_ref.shape[2]
    nslab = d // GROUP
    pitch = acc_ref.shape[1] // nslab
    ypitch = ys_ref.shape[0] // nslab
    slot = b % 2
    group = ye_ref.shape[0]

    def x1_slab_copy(batch, into, c):
        rows = pl.ds(pl.multiple_of(batch * s, s), s)
        return pltpu.make_async_copy(x1_hbm.at[rows, c * GROUP:(c + 1) * GROUP],
                                     acc_ref.at[into, c * pitch:c * pitch + s, :], sem_x1.at[into])

    @pl.when(jnp.logical_and(b == 0, e == 0))
    def _():
        for c in range(nslab):
            x1_slab_copy(0, 0, c).start()

    @pl.when(e == 0)
    def _():
        @pl.when(b + 1 < pl.num_programs(0))
        def _():
            for c in range(nslab):
                x1_slab_copy(b + 1, 1 - slot, c).start()

        for c in range(nslab):
            x1_slab_copy(b, slot, c).wait()

    def accumulate(acc):
        for k in range(group):
            base = (b * ne + e * group + k) * cap
            for c in range(nslab):
                ys_ref[c * ypitch:c * ypitch + cap, :] = ye_ref[k, :, c * GROUP:(c + 1) * GROUP].astype(f32)

            def body(i, _, base=base):
                r0 = pl.multiple_of(i * SCATTER_ROWS, SCATTER_ROWS)
                toks = [idx_ref[base + r0 + j] for j in range(SCATTER_ROWS)]
                rows = [acc[pl.ds(tok, nslab, stride=pitch), :] for tok in toks]
                for j in range(SCATTER_ROWS):
                    acc[pl.ds(toks[j], nslab, stride=pitch), :] = \
                        rows[j] + ys_ref[pl.ds(r0 + j, nslab, stride=ypitch), :]
                return 0

            lax.fori_loop(0, cap // SCATTER_ROWS, body, 0)

        @pl.when(e == ne // group - 1)
        def _():
            nsteps = s // NORM_ROWS
            batch_row0 = pl.multiple_of(b * s, s)

            def out_copy(step):
                rows = pl.ds(batch_row0 + step * NORM_ROWS, NORM_ROWS)
                return pltpu.make_async_copy(stage_ref.at[step % 2], out_hbm.at[rows, :], sem_out.at[step % 2])

            for step in range(nsteps):
                lo = step * NORM_ROWS
                y = jnp.concatenate([acc[c * pitch + lo:c * pitch + lo + NORM_ROWS, :] for c in range(nslab)],
                                    axis=1)
                if do_norm:
                    ms = jnp.mean(y * y, axis=-1, keepdims=True)
                    y = y * lax.rsqrt(ms + EPS) * g_ref[...]
                if step >= 2:
                    out_copy(step - 2).wait()
                stage_ref[step % 2] = y
                out_copy(step).start()
            for step in range(max(nsteps - 2, 0), nsteps):
                out_copy(step).wait()

    for k in range(2):
        pl.when(slot == k)(functools.partial(accumulate, acc_ref.at[k]))


def _combine(idx_flat, ye, x1, g, nb, s, ne, cap, do_norm):
    d = ye.shape[2]
    nslab = d // GROUP
    group = _pick(ne, STEP_EXPERTS)
    return pl.pallas_call(
        functools.partial(_combine_kernel, ne=ne, cap=cap, s=s, do_norm=do_norm),
        grid_spec=pltpu.PrefetchScalarGridSpec(
            num_scalar_prefetch=1,
            grid=(nb, ne // group),
            in_specs=[
                pl.BlockSpec((group, cap, d), lambda bi, e, idx: (e, bi, 0)),
                pl.BlockSpec(memory_space=pl.ANY),
                pl.BlockSpec((1, d), lambda bi, e, idx: (0, 0)),
            ],
            out_specs=pl.BlockSpec(memory_space=pl.ANY),
            scratch_shapes=[
                pltpu.VMEM((2, nslab * _slab_pitch(s), GROUP), f32),
                pltpu.VMEM((nslab * _slab_pitch(cap), GROUP), f32),
                pltpu.VMEM((2, NORM_ROWS, d), f32),
                pltpu.SemaphoreType.DMA((2,)),
                pltpu.SemaphoreType.DMA((2,)),
            ],
        ),
        out_shape=jax.ShapeDtypeStruct((nb * s, d), f32),
        compiler_params=_params(("arbitrary", "arbitrary")),
        name="combine",
    )(idx_flat, ye, x1, g)


def _channel_dft_table():
    k = jnp.arange(GROUP, dtype=jnp.int32)
    ang = ((k[:, None] * k[None, :]) % GROUP).astype(f32) * (2.0 * math.pi / GROUP)
    return jnp.concatenate([jnp.cos(ang), jnp.sin(ang)], axis=1).astype(bf16)


def _sequence_dft_tables(s, ts, scale):
    half = s // 2
    inner = 64
    k = jnp.arange(half, dtype=jnp.int32)[None, :]
    n1 = jnp.arange(half // inner, dtype=jnp.int32)[:, None] * inner
    n2 = jnp.arange(inner, dtype=jnp.int32)[:, None]
    w = 2.0 * math.pi / s
    a1 = ((n1 * k) % s).astype(f32) * w
    a2 = ((n2 * k) % s).astype(f32) * w
    c1, s1 = jnp.cos(a1)[:, None, :], jnp.sin(a1)[:, None, :]
    c2, s2 = jnp.cos(a2)[None, :, :] * scale, jnp.sin(a2)[None, :, :] * scale
    cp = (c1 * c2 - s1 * s2).reshape(half, half).astype(bf16)
    sm = (-(s1 * c2 + c1 * s2)).reshape(half, half).astype(bf16)
    rows = jnp.arange(ts, dtype=jnp.int32)[:, None]
    cols = jnp.arange(ts, dtype=jnp.int32)[None, :]
    flip = (cols == ts - rows).astype(bf16)
    kk = jnp.arange(half, dtype=jnp.int32)[None, :]
    alt = jnp.where(jnp.arange(8)[:, None] == 0, (1 - 2 * (kk & 1)).astype(f32) * scale, 0.0).astype(bf16)
    fold = min(FOLD_ROWS, half)
    r = jnp.arange(fold, dtype=jnp.int32)
    j1 = ((r[None, :] == fold - r[:, None]) & (r[:, None] >= 1)).astype(bf16)
    return cp, sm, flip, alt, j1


def _pick(n, pref):
    return pref if n % pref == 0 else n


def kernel(x, norm1_g, w_in, sgu_ln_g, sgu_ln_b, w_spatial, b_spatial, w_fourier_out, w_sgu_out, w_out,
           norm2_g, w_router, w_gate_e, w_up_e, w_down_e, final_g):
    nb, s, d = x.shape
    depth = norm1_g.shape[0]
    ne = w_router.shape[2]
    ff = w_gate_e.shape[3]
    cap = CAPACITY_FACTOR * s // ne
    t = nb * s
    assert s % NORM_ROWS == 0 and s & (s - 1) == 0 and cap % GATHER_ROWS == 0 and d % GROUP == 0
    assert (nb * cap) % GROUP == 0

    tm_front = _pick(s, 1024)
    ts_dft = _pick(s // 2, 512)
    tm_post = _pick(s // 2, 1024)
    dft_scale = 1.0 / math.sqrt(s * GROUP)
    cs_tab = _channel_dft_table()
    seq_tables = _sequence_dft_tables(s, ts_dft, dft_scale)
    tmo = _pick(nb * cap, 2048)
    tf = _pick(ff, 512)
    sub = _pick(tmo, 1024)

    x2 = x.reshape(t, d)
    for l in range(depth):
        a, b, ysg, sgf = _front(
            x2, norm1_g[l][None, :], w_in[l].astype(bf16), cs_tab,
            sgu_ln_g[l][None, :], sgu_ln_b[l][None, :], w_spatial[l].astype(bf16),
            b_spatial[l].T, w_sgu_out[l].astype(bf16), tm_front)
        fm_first, fm_second = _seqdft(seq_tables, a, b, nb, s, ts_dft, dft_scale)
        wr = jnp.pad(w_router[l], ((0, 0), (0, ROUTER_LANES - ne)))
        wr_hi = wr.astype(bf16)
        wr_lo = (wr - wr_hi.astype(f32)).astype(bf16)
        wr_split = jnp.concatenate([jnp.concatenate([wr_hi, wr_lo], axis=1),
                                    jnp.concatenate([wr_hi, jnp.zeros_like(wr_hi)], axis=1)], axis=0)
        x1, h2, aff = _post(fm_first, fm_second, sgf, ysg, x2, w_fourier_out[l].astype(bf16),
                            w_out[l].astype(bf16), norm2_g[l][None, :], wr_split, ne, s, tm_post)
        idx, gate = _select(aff, nb, s, cap)
        idx_flat = idx.reshape(-1)
        xe = _gather(idx_flat, h2, nb, s, ne, cap)
        gate_row = jnp.transpose(gate, (1, 0, 2)).reshape(ne, 1, nb * cap)
        ye = _experts(xe, w_gate_e[l], w_up_e[l], w_down_e[l], gate_row, tmo, tf, sub)
        x2 = _combine(idx_flat, ye, x1, final_g[None, :], nb, s, ne, cap, l == depth - 1)
    return x2.reshape(nb, s, d)
```

```python
import functools
import math

import jax
import jax.numpy as jnp
from jax import lax
from jax.experimental import pallas as pl
from jax.experimental.pallas import tpu as pltpu

EPS = 1e-6
GROUP = 128
N_GROUPS = 4
F_WIDTH = N_GROUPS * GROUP
S_WIDTH = N_GROUPS * GROUP
CAPACITY_FACTOR = 2
VMEM_LIMIT_V7X = 56 * 1024 * 1024

f32 = jnp.float32
bf16 = jnp.bfloat16


def _dot(a, b):
    return jnp.dot(a, b, preferred_element_type=f32)


def _params(sem, vmem=VMEM_LIMIT_V7X):
    return pltpu.CompilerParams(dimension_semantics=sem, vmem_limit_bytes=vmem)


FRONT_ROWS = 1024


def _front_kernel(x_ref, g1_ref, win_ref, cs_ref, lng_ref, lnb_ref, ws_ref, bs_ref, wso_ref,
                  a_ref, b_ref, ysg_ref, sgf_ref, mix_ref):
    tm, d = x_ref.shape
    c0 = F_WIDTH + 2 * S_WIDTH
    chunk = min(FRONT_ROWS, tm)
    for r0 in range(0, tm, chunk):
        rows = slice(r0, r0 + chunk)
        x = x_ref[rows, :]
        ms = jnp.mean(x * x, axis=-1, keepdims=True)
        h = (x * lax.rsqrt(ms + EPS) * g1_ref[...]).astype(bf16)

        def proj(lo, hi):
            return _dot(h, win_ref[:, lo:hi])

        zf = proj(0, F_WIDTH)
        for g in range(N_GROUPS):
            cols = slice(g * GROUP, (g + 1) * GROUP)
            ab = _dot(zf[:, cols].astype(bf16), cs_ref[...])
            a_ref[rows, cols] = ab[:, :GROUP].astype(bf16)
            b_ref[rows, cols] = ab[:, GROUP:].astype(bf16)

        u_pre = proj(F_WIDTH, F_WIDTH + S_WIDTH)
        v_pre = proj(F_WIDTH + S_WIDTH, F_WIDTH + 2 * S_WIDTH)
        sgf_ref[rows, :] = jax.nn.sigmoid(proj(c0, c0 + d)).astype(bf16)
        gs_pre = proj(c0 + d, c0 + 2 * d)
        u = jax.nn.gelu(u_pre)
        v = jax.nn.gelu(v_pre)
        for g in range(N_GROUPS):
            cols = slice(g * GROUP, (g + 1) * GROUP)
            vg = v[:, cols]
            mu = jnp.mean(vg, axis=-1, keepdims=True)
            dv = vg - mu
            var = jnp.mean(dv * dv, axis=-1, keepdims=True)
            vnb = (dv * lax.rsqrt(var + EPS) * lng_ref[:, cols] + lnb_ref[:, cols]).astype(bf16)
            for c in range(chunk // GROUP):
                m = _dot(ws_ref[g], vnb[c * GROUP:(c + 1) * GROUP, :]) + bs_ref[:, g:g + 1]
                mix_ref[r0 + c * GROUP:r0 + (c + 1) * GROUP, cols] = m
        sgu = (u * mix_ref[rows, :]).astype(bf16)
        ys = _dot(sgu, wso_ref[...])
        ysg_ref[rows, :] = (jax.nn.sigmoid(gs_pre) * ys).astype(bf16)


def _front(x2, g1, win_b, cs_b, lng, lnb, ws_b, bs_t, wso_b, tm):
    t, d = x2.shape
    kin = win_b.shape[1]
    const = lambda *shape: pl.BlockSpec(shape, lambda i: (0,) * len(shape))
    return pl.pallas_call(
        _front_kernel,
        grid=(t // tm,),
        in_specs=[
            pl.BlockSpec((tm, d), lambda i: (i, 0)),
            const(1, d),
            const(d, kin),
            const(GROUP, 2 * GROUP),
            const(1, S_WIDTH),
            const(1, S_WIDTH),
            const(N_GROUPS, GROUP, GROUP),
            const(GROUP, N_GROUPS),
            const(S_WIDTH, d),
        ],
        out_specs=[
            pl.BlockSpec((tm, F_WIDTH), lambda i: (i, 0)),
            pl.BlockSpec((tm, F_WIDTH), lambda i: (i, 0)),
            pl.BlockSpec((tm, d), lambda i: (i, 0)),
            pl.BlockSpec((tm, d), lambda i: (i, 0)),
        ],
        out_shape=[
            jax.ShapeDtypeStruct((t, F_WIDTH), bf16),
            jax.ShapeDtypeStruct((t, F_WIDTH), bf16),
            jax.ShapeDtypeStruct((t, d), bf16),
            jax.ShapeDtypeStruct((t, d), bf16),
        ],
        scratch_shapes=[pltpu.VMEM((tm, S_WIDTH), f32)],
        compiler_params=_params(("parallel",)),
        name="front",
    )(x2, g1, win_b, cs_b, lng, lnb, ws_b, bs_t, wso_b)


FOLD_ROWS = 256


def _seqdft_kernel(cp_ref, sm_ref, flip_ref, altrow_ref, j1_ref, a_ref, b_ref, fmd_ref, fmm_ref,
                   ap_ref, bm_ref, edge_ref, *, scale):
    i = pl.program_id(1)
    ts = fmd_ref.shape[0]
    s = a_ref.shape[0]
    half = s // 2
    a_mid = scale * a_ref[half:half + 1, :].astype(f32)

    @pl.when(i == 0)
    def _fold_inputs():
        fold = j1_ref.shape[0]
        first_row = lax.broadcasted_iota(jnp.int32, (fold, 1), 0) == 0
        for j in range(half // fold):
            lo = slice(j * fold, (j + 1) * fold)
            mir = slice(s - (j + 1) * fold, s - j * fold)
            a_m = _dot(j1_ref[...], a_ref[mir, :])
            b_m = _dot(j1_ref[...], b_ref[mir, :])
            if j > 0:
                a_m = jnp.where(first_row, a_ref[mir.stop:mir.stop + 1, :].astype(f32), a_m)
                b_m = jnp.where(first_row, b_ref[mir.stop:mir.stop + 1, :].astype(f32), b_m)
            ap_ref[lo, :] = (a_ref[lo, :].astype(f32) + a_m).astype(bf16)
            bm_ref[lo, :] = (b_ref[lo, :].astype(f32) - b_m).astype(bf16)
        first = lax.broadcasted_iota(jnp.int32, (8, 1), 0) == 0
        edge_ref[...] = _dot(altrow_ref[...], ap_ref[...]) + jnp.where(first, a_mid, 0.0)

    p = _dot(cp_ref[...], ap_ref[...])
    q = _dot(sm_ref[...], bm_ref[...])
    row = lax.broadcasted_iota(jnp.int32, (ts, 1), 0)
    alt = (1 - 2 * (row & 1)).astype(f32) * a_mid
    fmd_ref[...] = (p + q + alt).astype(bf16)
    g = p - q + alt
    flipped = _dot(flip_ref[...], g.astype(bf16))
    fmm_ref[...] = jnp.where(row == 0, edge_ref[0:1, :], flipped).astype(bf16)
    edge_ref[...] = g[0:8, :]


def _seqdft(tables, a, b, nb, s, ts, scale):
    half = s // 2
    nh = half // ts
    cp, sm, flip, altrow, j1 = tables
    const = lambda *shape: pl.BlockSpec(shape, lambda bi, i: (0,) * len(shape))
    table_tile = pl.BlockSpec((ts, half), lambda bi, i: (nh - 1 - i, 0))
    return pl.pallas_call(
        functools.partial(_seqdft_kernel, scale=scale),
        grid=(nb, nh),
        in_specs=[
            table_tile,
            table_tile,
            const(ts, ts),
            const(8, half),
            const(*j1.shape),
            pl.BlockSpec((s, F_WIDTH), lambda bi, i: (bi, 0)),
            pl.BlockSpec((s, F_WIDTH), lambda bi, i: (bi, 0)),
        ],
        out_specs=[
            pl.BlockSpec((ts, F_WIDTH), lambda bi, i: (bi * nh + nh - 1 - i, 0)),
            pl.BlockSpec((ts, F_WIDTH), lambda bi, i: (bi * nh + i, 0)),
        ],
        out_shape=[
            jax.ShapeDtypeStruct((nb * half, F_WIDTH), bf16),
            jax.ShapeDtypeStruct((nb * half, F_WIDTH), bf16),
        ],
        scratch_shapes=[
            pltpu.VMEM((half, F_WIDTH), bf16),
            pltpu.VMEM((half, F_WIDTH), bf16),
            pltpu.VMEM((8, F_WIDTH), f32),
        ],
        compiler_params=_params(("parallel", "arbitrary")),
        name="seqdft",
    )(cp, sm, flip, altrow, j1, a, b)


ROUTER_LANES = 128
POST_ROWS = 256


def _post_kernel(fmd_ref, fmm_ref, sgf_ref, ysg_ref, x_ref, wfo_ref, wout_ref, g2_ref, wr_ref,
                 x1_ref, h2_ref, aff_ref, *, tiles_per_half):
    ne = aff_ref.shape[0]
    tm = x_ref.shape[0]
    in_first_half = (pl.program_id(0) % (2 * tiles_per_half)) < tiles_per_half
    chunk = min(POST_ROWS, tm)
    for r0 in range(0, tm, chunk):
        rows = slice(r0, r0 + chunk)
        fm = jnp.where(in_first_half, fmd_ref[rows, :], fmm_ref[rows, :])
        yf = _dot(fm, wfo_ref[...])
        merged = sgf_ref[rows, :].astype(f32) * yf + ysg_ref[rows, :].astype(f32)
        x1 = x_ref[rows, :] + _dot(merged.astype(bf16), wout_ref[...])
        x1_ref[rows, :] = x1
        ms = jnp.mean(x1 * x1, axis=-1, keepdims=True)
        h2 = x1 * lax.rsqrt(ms + EPS) * g2_ref[...]
        h2_ref[rows, :] = h2
        h_hi = h2.astype(bf16)
        h_lo = (h2 - h_hi.astype(f32)).astype(bf16)
        both = _dot(jnp.concatenate([h_hi, h_lo], axis=1), wr_ref[...])
        logits = both[:, :ROUTER_LANES] + both[:, ROUTER_LANES:]
        logits = logits.T[:ne, :]
        mx = jnp.max(logits, axis=0, keepdims=True)
        ex = jnp.exp(logits - mx)
        aff_ref[:, rows] = ex / jnp.sum(ex, axis=0, keepdims=True)


def _post(fm_first, fm_second, sgf, ysg, x2, wfo_b, wout_b, g2, wr_split, ne, s, tm):
    t, d = x2.shape
    th = s // 2 // tm
    const = lambda *shape: pl.BlockSpec(shape, lambda i: (0,) * len(shape))
    tile = lambda w: pl.BlockSpec((tm, w), lambda i: (i, 0))
    first = pl.BlockSpec((tm, F_WIDTH), lambda i: (i // (2 * th) * th + jnp.minimum(i % (2 * th), th - 1), 0))
    second = pl.BlockSpec((tm, F_WIDTH), lambda i: (i // (2 * th) * th + jnp.maximum(i % (2 * th) - th, 0), 0))
    return pl.pallas_call(
        functools.partial(_post_kernel, tiles_per_half=th),
        grid=(t // tm,),
        in_specs=[
            first, second, tile(d), tile(d), tile(d),
            const(F_WIDTH, d),
            const(d, d),
            const(1, d),
            const(2 * d, 2 * ROUTER_LANES),
        ],
        out_specs=[tile(d), tile(d), pl.BlockSpec((ne, tm), lambda i: (0, i))],
        out_shape=[
            jax.ShapeDtypeStruct((t, d), f32),
            jax.ShapeDtypeStruct((t, d), f32),
            jax.ShapeDtypeStruct((ne, t), f32),
        ],
        compiler_params=_params(("parallel",)),
        name="post",
    )(fm_first, fm_second, sgf, ysg, x2, wfo_b, wout_b, g2, wr_split)


def _select_kernel(aff_ref, idx_ref, gate_ref, *, cap):
    ne, s = aff_ref.shape
    nbits = s.bit_length() - 1
    aff = aff_ref[...]
    lane = lax.broadcasted_iota(jnp.int32, (ne, s), 1)

    def enough(cand):
        cnt = jnp.sum((aff >= pltpu.bitcast(cand, f32)).astype(jnp.int32), axis=1, keepdims=True)
        return cnt >= cap

    def two_bit_step(i, thr):
        hi = jnp.left_shift(jnp.int32(1), 29 - 2 * i)
        lo = jnp.left_shift(jnp.int32(1), 28 - 2 * i)
        c1, c2, c3 = thr | lo, thr | hi, thr | hi | lo
        return jnp.where(enough(c3), c3, jnp.where(enough(c2), c2, jnp.where(enough(c1), c1, thr)))

    thr = lax.fori_loop(0, 15, two_bit_step, jnp.zeros((ne, 1), jnp.int32))
    gt = aff >= pltpu.bitcast(thr + 1, f32)
    eq = jnp.logical_and(aff >= pltpu.bitcast(thr, f32), jnp.logical_not(gt))
    need = cap - jnp.sum(gt.astype(jnp.int32), axis=1, keepdims=True)

    def cumsum_tokens(v):
        k = 1
        while k < s:
            v = v + jnp.where(lane >= k, pltpu.roll(v, k, axis=1), 0)
            k *= 2
        return v

    eq_i = eq.astype(jnp.int32)
    surplus = jnp.max(jnp.sum(eq_i, axis=1, keepdims=True) - need) > 0
    keep = lax.cond(surplus, lambda: (cumsum_tokens(eq_i) <= need).astype(jnp.int32), lambda: eq_i)
    sel = jnp.logical_or(gt, jnp.logical_and(eq, keep == 1))
    slot = cumsum_tokens(sel.astype(jnp.int32)) - 1
    word = jnp.where(sel, lane | jnp.left_shift(lane - slot, nbits) | (1 << (2 * nbits)), 0)
    gate = aff
    for k in range(nbits):
        step = 1 << k
        inc_word = pltpu.roll(word, s - step, axis=1)
        inc_gate = pltpu.roll(gate, s - step, axis=1)
        take = (jnp.right_shift(inc_word, nbits + k) & 1) == 1
        leave = (jnp.right_shift(word, nbits + k) & 1) == 1
        word = jnp.where(take, inc_word, jnp.where(leave, 0, word))
        gate = jnp.where(take, inc_gate, gate)
    idx_ref[0] = word[:, :cap] & (s - 1)
    gate_ref[0] = gate[:, :cap]


def _select(aff_t, nb, s, cap):
    ne = aff_t.shape[0]
    return pl.pallas_call(
        functools.partial(_select_kernel, cap=cap),
        grid=(nb,),
        in_specs=[pl.BlockSpec((ne, s), lambda bi: (0, bi))],
        out_specs=[
            pl.BlockSpec((1, ne, cap), lambda bi: (bi, 0, 0)),
            pl.BlockSpec((1, ne, cap), lambda bi: (bi, 0, 0)),
        ],
        out_shape=[
            jax.ShapeDtypeStruct((nb, ne, cap), jnp.int32),
            jax.ShapeDtypeStruct((nb, ne, cap), f32),
        ],
        compiler_params=_params(("parallel",)),
        name="select",
    )(aff_t)


GATHER_ROWS = 128
SCATTER_ROWS = 16
STEP_EXPERTS = 4


def _slab_pitch(rows):
    pitch = rows + (-rows) % 4
    return pitch if (pitch // 4) % 2 else pitch + 4


def _gather_kernel(idx_ref, h2_hbm, xe_ref, buf_ref, stage_ref, sem, *, ne, cap, s):
    b = pl.program_id(0)
    e = pl.program_id(1)
    nslab = xe_ref.shape[2] // GROUP
    pitch = buf_ref.shape[1] // nslab
    spitch = stage_ref.shape[0] // nslab
    slot = b % 2
    group = xe_ref.shape[0]

    def slab_copy(batch, into, c):
        rows = pl.ds(pl.multiple_of(batch * s, s), s)
        return pltpu.make_async_copy(h2_hbm.at[rows, c * GROUP:(c + 1) * GROUP],
                                     buf_ref.at[into, c * pitch:c * pitch + s, :], sem.at[into])

    @pl.when(jnp.logical_and(b == 0, e == 0))
    def _():
        for c in range(nslab):
            slab_copy(0, 0, c).start()

    @pl.when(e == 0)
    def _():
        @pl.when(b + 1 < pl.num_programs(0))
        def _():
            for c in range(nslab):
                slab_copy(b + 1, 1 - slot, c).start()

        for c in range(nslab):
            slab_copy(b, slot, c).wait()

    for k in range(group):
        base = (b * ne + e * group + k) * cap

        def body(i, _, k=k, base=base):
            r0 = pl.multiple_of(i * GATHER_ROWS, GATHER_ROWS)
            for j in range(GATHER_ROWS):
                tok = idx_ref[base + r0 + j]
                stage_ref[pl.ds(j, nslab, stride=spitch), :] = buf_ref[slot, pl.ds(tok, nslab, stride=pitch), :]
            for c in range(nslab):
                xe_ref[k, pl.ds(r0, GATHER_ROWS), c * GROUP:(c + 1) * GROUP] = \
                    stage_ref[c * spitch:c * spitch + GATHER_ROWS, :].astype(bf16)
            return 0

        lax.fori_loop(0, cap // GATHER_ROWS, body, 0)


def _gather(idx_flat, h2, nb, s, ne, cap):
    d = h2.shape[1]
    nslab = d // GROUP
    group = _pick(ne, STEP_EXPERTS)
    return pl.pallas_call(
        functools.partial(_gather_kernel, ne=ne, cap=cap, s=s),
        grid_spec=pltpu.PrefetchScalarGridSpec(
            num_scalar_prefetch=1,
            grid=(nb, ne // group),
            in_specs=[pl.BlockSpec(memory_space=pl.ANY)],
            out_specs=pl.BlockSpec((group, cap, d), lambda bi, e, idx: (e, bi, 0)),
            scratch_shapes=[
                pltpu.VMEM((2, nslab * _slab_pitch(s), GROUP), f32),
                pltpu.VMEM((nslab * _slab_pitch(GATHER_ROWS), GROUP), f32),
                pltpu.SemaphoreType.DMA((2,)),
            ],
        ),
        out_shape=jax.ShapeDtypeStruct((ne, nb * cap, d), bf16),
        compiler_params=_params(("arbitrary", "arbitrary")),
        name="gather",
    )(idx_flat, h2)


def _expert_kernel(x_ref, wg_ref, wu_ref, wd_ref, gate_ref, y_ref, acc_ref, *, sub, nf):
    f = pl.program_id(2)
    tmo, d = acc_ref.shape

    def partial_out(rows, w_gate, w_up, w_down):
        xs = x_ref[0, rows, :]
        a = _dot(xs, w_gate)
        g = _dot(xs, w_up)
        hm = (a * jax.nn.sigmoid(a) * g).astype(bf16)
        return _dot(hm, w_down)

    def store_gated(rows, y):
        for c in range(rows.start // GROUP, rows.stop // GROUP):
            chunk = slice(c * GROUP, (c + 1) * GROUP)
            g_col = jnp.broadcast_to(gate_ref[0, :, chunk], (GROUP, GROUP)).T
            y_c = y[chunk.start - rows.start:chunk.stop - rows.start, :]
            y_ref[0, chunk, :] = (y_c * jnp.tile(g_col, (1, d // GROUP))).astype(bf16)

    def run(first, last):
        weights = [w_ref[0].astype(bf16) for w_ref in (wg_ref, wu_ref, wd_ref)]
        for r0 in range(0, tmo, sub):
            rows = slice(r0, r0 + sub)
            y = partial_out(rows, *weights)
            if not first:
                y = acc_ref[rows, :] + y
            if last:
                store_gated(rows, y)
            else:
                acc_ref[rows, :] = y

    if nf == 1:
        run(True, True)
    else:
        pl.when(f == 0)(lambda: run(True, False))
        pl.when(f == nf - 1)(lambda: run(False, True))
        if nf > 2:
            pl.when(jnp.logical_and(f > 0, f < nf - 1))(lambda: run(False, False))


def _experts(xe, wg, wu, wd, gate_row, tmo, tf, sub):
    ne, m, d = xe.shape
    ff = wg.shape[2]
    return pl.pallas_call(
        functools.partial(_expert_kernel, sub=sub, nf=ff // tf),
        grid=(ne, m // tmo, ff // tf),
        in_specs=[
            pl.BlockSpec((1, tmo, d), lambda e, mi, fi: (e, mi, 0)),
            pl.BlockSpec((1, d, tf), lambda e, mi, fi: (e, 0, fi)),
            pl.BlockSpec((1, d, tf), lambda e, mi, fi: (e, 0, fi)),
            pl.BlockSpec((1, tf, d), lambda e, mi, fi: (e, fi, 0)),
            pl.BlockSpec((1, 1, tmo), lambda e, mi, fi: (e, 0, mi)),
        ],
        out_specs=pl.BlockSpec((1, tmo, d), lambda e, mi, fi: (e, mi, 0)),
        out_shape=jax.ShapeDtypeStruct((ne, m, d), bf16),
        scratch_shapes=[pltpu.VMEM((tmo, d), f32)],
        compiler_params=_params(("parallel", "parallel", "arbitrary")),
        name="experts",
    )(xe, wg, wu, wd, gate_row)


NORM_ROWS = 256


def _combine_kernel(idx_ref, ye_ref, x1_hbm, g_ref, out_hbm, acc_ref, ys_ref, stage_ref, sem_x1, sem_out,
                    *, ne, cap, s, do_norm):
    b = pl.program_id(0)
    e = pl.program_id(1)
    d = stage_ref.shape[2]
    nslab = d // GROUP
    pitch = acc_ref.shape[1] // nslab
    ypitch = ys_ref.shape[0] // nslab
    slot = b % 2
    group = ye_ref.shape[0]

    def x1_slab_copy(batch, into, c):
        rows = pl.ds(pl.multiple_of(batch * s, s), s)
        return pltpu.make_async_copy(x1_hbm.at[rows, c * GROUP:(c + 1) * GROUP],
                                     acc_ref.at[into, c * pitch:c * pitch + s, :], sem_x1.at[into])

    @pl.when(jnp.logical_and(b == 0, e == 0))
    def _():
        for c in range(nslab):
            x1_slab_copy(0, 0, c).start()

    @pl.when(e == 0)
    def _():
        @pl.when(b + 1 < pl.num_programs(0))
        def _():
            for c in range(nslab):
                x1_slab_copy(b + 1, 1 - slot, c).start()

        for c in range(nslab):
            x1_slab_copy(b, slot, c).wait()

    def accumulate(acc):
        for k in range(group):
            base = (b * ne + e * group + k) * cap
            for c in range(nslab):
                ys_ref[c * ypitch:c * ypitch + cap, :] = ye_ref[k, :, c * GROUP:(c + 1) * GROUP].astype(f32)

            def body(i, _, base=base):
                r0 = pl.multiple_of(i * SCATTER_ROWS, SCATTER_ROWS)
                toks = [idx_ref[base + r0 + j] for j in range(SCATTER_ROWS)]
                rows = [acc[pl.ds(tok, nslab, stride=pitch), :] for tok in toks]
                for j in range(SCATTER_ROWS):
                    acc[pl.ds(toks[j], nslab, stride=pitch), :] = \
                        rows[j] + ys_ref[pl.ds(r0 + j, nslab, stride=ypitch), :]
                return 0

            lax.fori_loop(0, cap // SCATTER_ROWS, body, 0)

        @pl.when(e == ne // group - 1)
        def _():
            nsteps = s // NORM_ROWS
            batch_row0 = pl.multiple_of(b * s, s)

            def out_copy(step):
                rows = pl.ds(batch_row0 + step * NORM_ROWS, NORM_ROWS)
                return pltpu.make_async_copy(stage_ref.at[step % 2], out_hbm.at[rows, :], sem_out.at[step % 2])

            for step in range(nsteps):
                lo = step * NORM_ROWS
                y = jnp.concatenate([acc[c * pitch + lo:c * pitch + lo + NORM_ROWS, :] for c in range(nslab)],
                                    axis=1)
                if do_norm:
                    ms = jnp.mean(y * y, axis=-1, keepdims=True)
                    y = y * lax.rsqrt(ms + EPS) * g_ref[...]
                if step >= 2:
                    out_copy(step - 2).wait()
                stage_ref[step % 2] = y
                out_copy(step).start()
            for step in range(max(nsteps - 2, 0), nsteps):
                out_copy(step).wait()

    for k in range(2):
        pl.when(slot == k)(functools.partial(accumulate, acc_ref.at[k]))


def _combine(idx_flat, ye, x1, g, nb, s, ne, cap, do_norm):
    d = ye.shape[2]
    nslab = d // GROUP
    group = _pick(ne, STEP_EXPERTS)
    return pl.pallas_call(
        functools.partial(_combine_kernel, ne=ne, cap=cap, s=s, do_norm=do_norm),
        grid_spec=pltpu.PrefetchScalarGridSpec(
            num_scalar_prefetch=1,
            grid=(nb, ne // group),
            in_specs=[
                pl.BlockSpec((group, cap, d), lambda bi, e, idx: (e, bi, 0)),
                pl.BlockSpec(memory_space=pl.ANY),
                pl.BlockSpec((1, d), lambda bi, e, idx: (0, 0)),
            ],
            out_specs=pl.BlockSpec(memory_space=pl.ANY),
            scratch_shapes=[
                pltpu.VMEM((2, nslab * _slab_pitch(s), GROUP), f32),
                pltpu.VMEM((nslab * _slab_pitch(cap), GROUP), f32),
                pltpu.VMEM((2, NORM_ROWS, d), f32),
                pltpu.SemaphoreType.DMA((2,)),
                pltpu.SemaphoreType.DMA((2,)),
            ],
        ),
        out_shape=jax.ShapeDtypeStruct((nb * s, d), f32),
        compiler_params=_params(("arbitrary", "arbitrary")),
        name="combine",
    )(idx_flat, ye, x1, g)


def _channel_dft_table():
    k = jnp.arange(GROUP, dtype=jnp.int32)
    ang = ((k[:, None] * k[None, :]) % GROUP).astype(f32) * (2.0 * math.pi / GROUP)
    return jnp.concatenate([jnp.cos(ang), jnp.sin(ang)], axis=1).astype(bf16)


def _sequence_dft_tables(s, ts, scale):
    half = s // 2
    inner = 64
    k = jnp.arange(half, dtype=jnp.int32)[None, :]
    n1 = jnp.arange(half // inner, dtype=jnp.int32)[:, None] * inner
    n2 = jnp.arange(inner, dtype=jnp.int32)[:, None]
    w = 2.0 * math.pi / s
    a1 = ((n1 * k) % s).astype(f32) * w
    a2 = ((n2 * k) % s).astype(f32) * w
    c1, s1 = jnp.cos(a1)[:, None, :], jnp.sin(a1)[:, None, :]
    c2, s2 = jnp.cos(a2)[None, :, :] * scale, jnp.sin(a2)[None, :, :] * scale
    cp = (c1 * c2 - s1 * s2).reshape(half, half).astype(bf16)
    sm = (-(s1 * c2 + c1 * s2)).reshape(half, half).astype(bf16)
    rows = jnp.arange(ts, dtype=jnp.int32)[:, None]
    cols = jnp.arange(ts, dtype=jnp.int32)[None, :]
    flip = (cols == ts - rows).astype(bf16)
    kk = jnp.arange(half, dtype=jnp.int32)[None, :]
    alt = jnp.where(jnp.arange(8)[:, None] == 0, (1 - 2 * (kk & 1)).astype(f32) * scale, 0.0).astype(bf16)
    fold = min(FOLD_ROWS, half)
    r = jnp.arange(fold, dtype=jnp.int32)
    j1 = ((r[None, :] == fold - r[:, None]) & (r[:, None] >= 1)).astype(bf16)
    return cp, sm, flip, alt, j1


def _pick(n, pref):
    return pref if n % pref == 0 else n


def kernel(x, norm1_g, w_in, sgu_ln_g, sgu_ln_b, w_spatial, b_spatial, w_fourier_out, w_sgu_out, w_out,
           norm2_g, w_router, w_gate_e, w_up_e, w_down_e, final_g):
    nb, s, d = x.shape
    depth = norm1_g.shape[0]
    ne = w_router.shape[2]
    ff = w_gate_e.shape[3]
    cap = CAPACITY_FACTOR * s // ne
    t = nb * s
    assert s % NORM_ROWS == 0 and s & (s - 1) == 0 and cap % GATHER_ROWS == 0 and d % GROUP == 0
    assert (nb * cap) % GROUP == 0

    tm_front = _pick(s, 1024)
    ts_dft = _pick(s // 2, 512)
    tm_post = _pick(s // 2, 1024)
    dft_scale = 1.0 / math.sqrt(s * GROUP)
    cs_tab = _channel_dft_table()
    seq_tables = _sequence_dft_tables(s, ts_dft, dft_scale)
    tmo = _pick(nb * cap, 2048)
    tf = _pick(ff, 512)
    sub = _pick(tmo, 1024)

    x2 = x.reshape(t, d)
    for l in range(depth):
        a, b, ysg, sgf = _front(
            x2, norm1_g[l][None, :], w_in[l].astype(bf16), cs_tab,
            sgu_ln_g[l][None, :], sgu_ln_b[l][None, :], w_spatial[l].astype(bf16),
            b_spatial[l].T, w_sgu_out[l].astype(bf16), tm_front)
        fm_first, fm_second = _seqdft(seq_tables, a, b, nb, s, ts_dft, dft_scale)
        wr = jnp.pad(w_router[l], ((0, 0), (0, ROUTER_LANES - ne)))
        wr_hi = wr.astype(bf16)
        wr_lo = (wr - wr_hi.astype(f32)).astype(bf16)
        wr_split = jnp.concatenate([jnp.concatenate([wr_hi, wr_lo], axis=1),
                                    jnp.concatenate([wr_hi, jnp.zeros_like(wr_hi)], axis=1)], axis=0)
        x1, h2, aff = _post(fm_first, fm_second, sgf, ysg, x2, w_fourier_out[l].astype(bf16),
                            w_out[l].astype(bf16), norm2_g[l][None, :], wr_split, ne, s, tm_post)
        idx, gate = _select(aff, nb, s, cap)
        idx_flat = idx.reshape(-1)
        xe = _gather(idx_flat, h2, nb, s, ne, cap)
        gate_row = jnp.transpose(gate, (1, 0, 2)).reshape(ne, 1, nb * cap)
        ye = _experts(xe, w_gate_e[l], w_up_e[l], w_down_e[l], gate_row, tmo, tf, sub)
        x2 = _combine(idx_flat, ye, x1, final_g[None, :], nb, s, ne, cap, l == depth - 1)
    return x2.reshape(nb, s, d)
```

```python
import functools
import math

import jax
import jax.numpy as jnp
from jax import lax
from jax.experimental import pallas as pl
from jax.experimental.pallas import tpu as pltpu

EPS = 1e-6
GROUP = 128
N_GROUPS = 4
F_WIDTH = N_GROUPS * GROUP
S_WIDTH = N_GROUPS * GROUP
CAPACITY_FACTOR = 2
VMEM_LIMIT_V7X = 56 * 1024 * 1024

f32 = jnp.float32
bf16 = jnp.bfloat16


def _dot(a, b):
    return jnp.dot(a, b, preferred_element_type=f32)


def _params(sem, vmem=VMEM_LIMIT_V7X):
    return pltpu.CompilerParams(dimension_semantics=sem, vmem_limit_bytes=vmem)


FRONT_ROWS = 1024


def _front_kernel(x_ref, g1_ref, win_ref, cs_ref, lng_ref, lnb_ref, ws_ref, bs_ref, wso_ref,
                  a_ref, b_ref, ysg_ref, sgf_ref, mix_ref):
    tm, d = x_ref.shape
    c0 = F_WIDTH + 2 * S_WIDTH
    chunk = min(FRONT_ROWS, tm)
    for r0 in range(0, tm, chunk):
        rows = slice(r0, r0 + chunk)
        x = x_ref[rows, :]
        ms = jnp.mean(x * x, axis=-1, keepdims=True)
        h = (x * lax.rsqrt(ms + EPS) * g1_ref[...]).astype(bf16)

        def proj(lo, hi):
            return _dot(h, win_ref[:, lo:hi])

        zf = proj(0, F_WIDTH)
        for g in range(N_GROUPS):
            cols = slice(g * GROUP, (g + 1) * GROUP)
            ab = _dot(zf[:, cols].astype(bf16), cs_ref[...])
            a_ref[rows, cols] = ab[:, :GROUP].astype(bf16)
            b_ref[rows, cols] = ab[:, GROUP:].astype(bf16)

        u_pre = proj(F_WIDTH, F_WIDTH + S_WIDTH)
        v_pre = proj(F_WIDTH + S_WIDTH, F_WIDTH + 2 * S_WIDTH)
        sgf_ref[rows, :] = jax.nn.sigmoid(proj(c0, c0 + d)).astype(bf16)
        gs_pre = proj(c0 + d, c0 + 2 * d)
        u = jax.nn.gelu(u_pre)
        v = jax.nn.gelu(v_pre)
        for g in range(N_GROUPS):
            cols = slice(g * GROUP, (g + 1) * GROUP)
            vg = v[:, cols]
            mu = jnp.mean(vg, axis=-1, keepdims=True)
            dv = vg - mu
            var = jnp.mean(dv * dv, axis=-1, keepdims=True)
            vnb = (dv * lax.rsqrt(var + EPS) * lng_ref[:, cols] + lnb_ref[:, cols]).astype(bf16)
            for c in range(chunk // GROUP):
                m = _dot(ws_ref[g], vnb[c * GROUP:(c + 1) * GROUP, :]) + bs_ref[:, g:g + 1]
                mix_ref[r0 + c * GROUP:r0 + (c + 1) * GROUP, cols] = m
        sgu = (u * mix_ref[rows, :]).astype(bf16)
        ys = _dot(sgu, wso_ref[...])
        ysg_ref[rows, :] = (jax.nn.sigmoid(gs_pre) * ys).astype(bf16)


def _front(x2, g1, win_b, cs_b, lng, lnb, ws_b, bs_t, wso_b, tm):
    t, d = x2.shape
    kin = win_b.shape[1]
    const = lambda *shape: pl.BlockSpec(shape, lambda i: (0,) * len(shape))
    return pl.pallas_call(
        _front_kernel,
        grid=(t // tm,),
        in_specs=[
            pl.BlockSpec((tm, d), lambda i: (i, 0)),
            const(1, d),
            const(d, kin),
            const(GROUP, 2 * GROUP),
            const(1, S_WIDTH),
            const(1, S_WIDTH),
            const(N_GROUPS, GROUP, GROUP),
            const(GROUP, N_GROUPS),
            const(S_WIDTH, d),
        ],
        out_specs=[
            pl.BlockSpec((tm, F_WIDTH), lambda i: (i, 0)),
            pl.BlockSpec((tm, F_WIDTH), lambda i: (i, 0)),
            pl.BlockSpec((tm, d), lambda i: (i, 0)),
            pl.BlockSpec((tm, d), lambda i: (i, 0)),
        ],
        out_shape=[
            jax.ShapeDtypeStruct((t, F_WIDTH), bf16),
            jax.ShapeDtypeStruct((t, F_WIDTH), bf16),
            jax.ShapeDtypeStruct((t, d), bf16),
            jax.ShapeDtypeStruct((t, d), bf16),
        ],
        scratch_shapes=[pltpu.VMEM((tm, S_WIDTH), f32)],
        compiler_params=_params(("parallel",)),
        name="front",
    )(x2, g1, win_b, cs_b, lng, lnb, ws_b, bs_t, wso_b)


FOLD_ROWS = 256


def _seqdft_kernel(cp_ref, sm_ref, flip_ref, altrow_ref, j1_ref, a_ref, b_ref, fmd_ref, fmm_ref,
                   ap_ref, bm_ref, edge_ref, *, scale):
    i = pl.program_id(1)
    ts = fmd_ref.shape[0]
    s = a_ref.shape[0]
    half = s // 2
    a_mid = scale * a_ref[half:half + 1, :].astype(f32)

    @pl.when(i == 0)
    def _fold_inputs():
        fold = j1_ref.shape[0]
        first_row = lax.broadcasted_iota(jnp.int32, (fold, 1), 0) == 0
        for j in range(half // fold):
            lo = slice(j * fold, (j + 1) * fold)
            mir = slice(s - (j + 1) * fold, s - j * fold)
            a_m = _dot(j1_ref[...], a_ref[mir, :])
            b_m = _dot(j1_ref[...], b_ref[mir, :])
            if j > 0:
                a_m = jnp.where(first_row, a_ref[mir.stop:mir.stop + 1, :].astype(f32), a_m)
                b_m = jnp.where(first_row, b_ref[mir.stop:mir.stop + 1, :].astype(f32), b_m)
            ap_ref[lo, :] = (a_ref[lo, :].astype(f32) + a_m).astype(bf16)
            bm_ref[lo, :] = (b_ref[lo, :].astype(f32) - b_m).astype(bf16)
        first = lax.broadcasted_iota(jnp.int32, (8, 1), 0) == 0
        edge_ref[...] = _dot(altrow_ref[...], ap_ref[...]) + jnp.where(first, a_mid, 0.0)

    p = _dot(cp_ref[...], ap_ref[...])
    q = _dot(sm_ref[...], bm_ref[...])
    row = lax.broadcasted_iota(jnp.int32, (ts, 1), 0)
    alt = (1 - 2 * (row & 1)).astype(f32) * a_mid
    fmd_ref[...] = (p + q + alt).astype(bf16)
    g = p - q + alt
    flipped = _dot(flip_ref[...], g.astype(bf16))
    fmm_ref[...] = jnp.where(row == 0, edge_ref[0:1, :], flipped).astype(bf16)
    edge_ref[...] = g[0:8, :]


def _seqdft(tables, a, b, nb, s, ts, scale):
    half = s // 2
    nh = half // ts
    cp, sm, flip, altrow, j1 = tables
    const = lambda *shape: pl.BlockSpec(shape, lambda bi, i: (0,) * len(shape))
    table_tile = pl.BlockSpec((ts, half), lambda bi, i: (nh - 1 - i, 0))
    return pl.pallas_call(
        functools.partial(_seqdft_kernel, scale=scale),
        grid=(nb, nh),
        in_specs=[
            table_tile,
            table_tile,
            const(ts, ts),
            const(8, half),
            const(*j1.shape),
            pl.BlockSpec((s, F_WIDTH), lambda bi, i: (bi, 0)),
            pl.BlockSpec((s, F_WIDTH), lambda bi, i: (bi, 0)),
        ],
        out_specs=[
            pl.BlockSpec((ts, F_WIDTH), lambda bi, i: (bi * nh + nh - 1 - i, 0)),
            pl.BlockSpec((ts, F_WIDTH), lambda bi, i: (bi * nh + i, 0)),
        ],
        out_shape=[
            jax.ShapeDtypeStruct((nb * half, F_WIDTH), bf16),
            jax.ShapeDtypeStruct((nb * half, F_WIDTH), bf16),
        ],
        scratch_shapes=[
            pltpu.VMEM((half, F_WIDTH), bf16),
            pltpu.VMEM((half, F_WIDTH), bf16),
            pltpu.VMEM((8, F_WIDTH), f32),
        ],
        compiler_params=_params(("parallel", "arbitrary")),
        name="seqdft",
    )(cp, sm, flip, altrow, j1, a, b)


ROUTER_LANES = 128
POST_ROWS = 256
POST_RING = 3


def _post_kernel(fmd_ref, fmm_ref, sgf_hbm, ysg_hbm, x_hbm, wfo_ref, wout_ref, g2_ref, wr_ref,
                 x1_ref, h2_ref, aff_ref, sgf_buf, ysg_buf, x_buf, sems, *, tiles_per_half, nsteps):
    ne = aff_ref.shape[0]
    tm = x1_ref.shape[0]
    i = pl.program_id(0)
    streams = ((sgf_hbm, sgf_buf), (ysg_hbm, ysg_buf), (x_hbm, x_buf))

    def tile_copies(step):
        rows = pl.ds(pl.multiple_of(step * tm, tm), tm)
        slot = step % POST_RING
        return [pltpu.make_async_copy(hbm.at[rows, :], buf.at[slot], sems.at[k, slot])
                for k, (hbm, buf) in enumerate(streams)]

    @pl.when(i == 0)
    def _():
        for step in range(min(POST_RING - 1, nsteps)):
            for copy in tile_copies(step):
                copy.start()

    @pl.when(i + POST_RING - 1 < nsteps)
    def _():
        for copy in tile_copies(i + POST_RING - 1):
            copy.start()

    for copy in tile_copies(i):
        copy.wait()
    slot = i % POST_RING
    sgf_ref, ysg_ref, x_ref = sgf_buf.at[slot], ysg_buf.at[slot], x_buf.at[slot]
    in_first_half = (i % (2 * tiles_per_half)) < tiles_per_half
    chunk = min(POST_ROWS, tm)
    for r0 in range(0, tm, chunk):
        rows = slice(r0, r0 + chunk)
        fm = jnp.where(in_first_half, fmd_ref[rows, :], fmm_ref[rows, :])
        yf = _dot(fm, wfo_ref[...])
        merged = sgf_ref[rows, :].astype(f32) * yf + ysg_ref[rows, :].astype(f32)
        x1 = x_ref[rows, :] + _dot(merged.astype(bf16), wout_ref[...])
        x1_ref[rows, :] = x1
        ms = jnp.mean(x1 * x1, axis=-1, keepdims=True)
        h2 = x1 * lax.rsqrt(ms + EPS) * g2_ref[...]
        h2_ref[rows, :] = h2
        h_hi = h2.astype(bf16)
        h_lo = (h2 - h_hi.astype(f32)).astype(bf16)
        both = _dot(jnp.concatenate([h_hi, h_lo], axis=1), wr_ref[...])
        logits = both[:, :ROUTER_LANES] + both[:, ROUTER_LANES:]
        logits = logits.T[:ne, :]
        mx = jnp.max(logits, axis=0, keepdims=True)
        ex = jnp.exp(logits - mx)
        aff_ref[:, rows] = ex / jnp.sum(ex, axis=0, keepdims=True)


def _post(fm_first, fm_second, sgf, ysg, x2, wfo_b, wout_b, g2, wr_split, ne, s, tm):
    t, d = x2.shape
    th = s // 2 // tm
    const = lambda *shape: pl.BlockSpec(shape, lambda i: (0,) * len(shape))
    tile = lambda w: pl.BlockSpec((tm, w), lambda i: (i, 0))
    first = pl.BlockSpec((tm, F_WIDTH), lambda i: (i // (2 * th) * th + jnp.minimum(i % (2 * th), th - 1), 0))
    second = pl.BlockSpec((tm, F_WIDTH), lambda i: (i // (2 * th) * th + jnp.maximum(i % (2 * th) - th, 0), 0))
    hbm = pl.BlockSpec(memory_space=pl.ANY)
    return pl.pallas_call(
        functools.partial(_post_kernel, tiles_per_half=th, nsteps=t // tm),
        grid=(t // tm,),
        in_specs=[
            first, second, hbm, hbm, hbm,
            const(F_WIDTH, d),
            const(d, d),
            const(1, d),
            const(2 * d, 2 * ROUTER_LANES),
        ],
        out_specs=[tile(d), tile(d), pl.BlockSpec((ne, tm), lambda i: (0, i))],
        out_shape=[
            jax.ShapeDtypeStruct((t, d), f32),
            jax.ShapeDtypeStruct((t, d), f32),
            jax.ShapeDtypeStruct((ne, t), f32),
        ],
        scratch_shapes=[
            pltpu.VMEM((POST_RING, tm, d), bf16),
            pltpu.VMEM((POST_RING, tm, d), bf16),
            pltpu.VMEM((POST_RING, tm, d), f32),
            pltpu.SemaphoreType.DMA((3, POST_RING)),
        ],
        compiler_params=_params(("arbitrary",)),
        name="post",
    )(fm_first, fm_second, sgf, ysg, x2, wfo_b, wout_b, g2, wr_split)


def _select_kernel(aff_ref, idx_ref, gate_ref, *, cap):
    ne, s = aff_ref.shape
    nbits = s.bit_length() - 1
    aff = aff_ref[...]
    lane = lax.broadcasted_iota(jnp.int32, (ne, s), 1)

    def enough(cand):
        cnt = jnp.sum((aff >= pltpu.bitcast(cand, f32)).astype(jnp.int32), axis=1, keepdims=True)
        return cnt >= cap

    def two_bit_step(i, thr):
        hi = jnp.left_shift(jnp.int32(1), 29 - 2 * i)
        lo = jnp.left_shift(jnp.int32(1), 28 - 2 * i)
        c1, c2, c3 = thr | lo, thr | hi, thr | hi | lo
        return jnp.where(enough(c3), c3, jnp.where(enough(c2), c2, jnp.where(enough(c1), c1, thr)))

    thr = lax.fori_loop(0, 15, two_bit_step, jnp.zeros((ne, 1), jnp.int32))
    gt = aff >= pltpu.bitcast(thr + 1, f32)
    eq = jnp.logical_and(aff >= pltpu.bitcast(thr, f32), jnp.logical_not(gt))
    need = cap - jnp.sum(gt.astype(jnp.int32), axis=1, keepdims=True)

    def cumsum_tokens(v):
        k = 1
        while k < s:
            v = v + jnp.where(lane >= k, pltpu.roll(v, k, axis=1), 0)
            k *= 2
        return v

    eq_i = eq.astype(jnp.int32)
    surplus = jnp.max(jnp.sum(eq_i, axis=1, keepdims=True) - need) > 0
    keep = lax.cond(surplus, lambda: (cumsum_tokens(eq_i) <= need).astype(jnp.int32), lambda: eq_i)
    sel = jnp.logical_or(gt, jnp.logical_and(eq, keep == 1))
    slot = cumsum_tokens(sel.astype(jnp.int32)) - 1
    word = jnp.where(sel, lane | jnp.left_shift(lane - slot, nbits) | (1 << (2 * nbits)), 0)
    gate = aff
    for k in range(nbits):
        step = 1 << k
        inc_word = pltpu.roll(word, s - step, axis=1)
        inc_gate = pltpu.roll(gate, s - step, axis=1)
        take = (jnp.right_shift(inc_word, nbits + k) & 1) == 1
        leave = (jnp.right_shift(word, nbits + k) & 1) == 1
        word = jnp.where(take, inc_word, jnp.where(leave, 0, word))
        gate = jnp.where(take, inc_gate, gate)
    idx_ref[0] = word[:, :cap] & (s - 1)
    gate_ref[0] = gate[:, :cap]


def _select(aff_t, nb, s, cap):
    ne = aff_t.shape[0]
    return pl.pallas_call(
        functools.partial(_select_kernel, cap=cap),
        grid=(nb,),
        in_specs=[pl.BlockSpec((ne, s), lambda bi: (0, bi))],
        out_specs=[
            pl.BlockSpec((1, ne, cap), lambda bi: (bi, 0, 0)),
            pl.BlockSpec((1, ne, cap), lambda bi: (bi, 0, 0)),
        ],
        out_shape=[
            jax.ShapeDtypeStruct((nb, ne, cap), jnp.int32),
            jax.ShapeDtypeStruct((nb, ne, cap), f32),
        ],
        compiler_params=_params(("parallel",)),
        name="select",
    )(aff_t)


GATHER_ROWS = 128
SCATTER_ROWS = 16
STEP_EXPERTS = 4


def _slab_pitch(rows):
    pitch = rows + (-rows) % 4
    return pitch if (pitch // 4) % 2 else pitch + 4


def _gather_kernel(idx_ref, h2_hbm, xe_ref, buf_ref, stage_ref, sem, *, ne, cap, s):
    b = pl.program_id(0)
    e = pl.program_id(1)
    nslab = xe_ref.shape[2] // GROUP
    pitch = buf_ref.shape[1] // nslab
    spitch = stage_ref.shape[0] // nslab
    slot = b % 2
    group = xe_ref.shape[0]

    def slab_copy(batch, into, c):
        rows = pl.ds(pl.multiple_of(batch * s, s), s)
        return pltpu.make_async_copy(h2_hbm.at[rows, c * GROUP:(c + 1) * GROUP],
                                     buf_ref.at[into, c * pitch:c * pitch + s, :], sem.at[into])

    @pl.when(jnp.logical_and(b == 0, e == 0))
    def _():
        for c in range(nslab):
            slab_copy(0, 0, c).start()

    @pl.when(e == 0)
    def _():
        @pl.when(b + 1 < pl.num_programs(0))
        def _():
            for c in range(nslab):
                slab_copy(b + 1, 1 - slot, c).start()

        for c in range(nslab):
            slab_copy(b, slot, c).wait()

    for k in range(group):
        base = (b * ne + e * group + k) * cap

        def body(i, _, k=k, base=base):
            r0 = pl.multiple_of(i * GATHER_ROWS, GATHER_ROWS)
            for j in range(GATHER_ROWS):
                tok = idx_ref[base + r0 + j]
                stage_ref[pl.ds(j, nslab, stride=spitch), :] = buf_ref[slot, pl.ds(tok, nslab, stride=pitch), :]
            for c in range(nslab):
                xe_ref[k, pl.ds(r0, GATHER_ROWS), c * GROUP:(c + 1) * GROUP] = \
                    stage_ref[c * spitch:c * spitch + GATHER_ROWS, :].astype(bf16)
            return 0

        lax.fori_loop(0, cap // GATHER_ROWS, body, 0)


def _gather(idx_flat, h2, nb, s, ne, cap):
    d = h2.shape[1]
    nslab = d // GROUP
    group = _pick(ne, STEP_EXPERTS)
    return pl.pallas_call(
        functools.partial(_gather_kernel, ne=ne, cap=cap, s=s),
        grid_spec=pltpu.PrefetchScalarGridSpec(
            num_scalar_prefetch=1,
            grid=(nb, ne // group),
            in_specs=[pl.BlockSpec(memory_space=pl.ANY)],
            out_specs=pl.BlockSpec((group, cap, d), lambda bi, e, idx: (e, bi, 0)),
            scratch_shapes=[
                pltpu.VMEM((2, nslab * _slab_pitch(s), GROUP), f32),
                pltpu.VMEM((nslab * _slab_pitch(GATHER_ROWS), GROUP), f32),
                pltpu.SemaphoreType.DMA((2,)),
            ],
        ),
        out_shape=jax.ShapeDtypeStruct((ne, nb * cap, d), bf16),
        compiler_params=_params(("arbitrary", "arbitrary")),
        name="gather",
    )(idx_flat, h2)


def _expert_kernel(x_ref, wg_ref, wu_ref, wd_ref, gate_ref, y_ref, acc_ref, *, sub, nf):
    f = pl.program_id(2)
    tmo, d = acc_ref.shape

    def partial_out(rows, w_gate, w_up, w_down):
        xs = x_ref[0, rows, :]
        a = _dot(xs, w_gate)
        g = _dot(xs, w_up)
        hm = (a * jax.nn.sigmoid(a) * g).astype(bf16)
        return _dot(hm, w_down)

    def store_gated(rows, y):
        for c in range(rows.start // GROUP, rows.stop // GROUP):
            chunk = slice(c * GROUP, (c + 1) * GROUP)
            g_col = jnp.broadcast_to(gate_ref[0, :, chunk], (GROUP, GROUP)).T
            y_c = y[chunk.start - rows.start:chunk.stop - rows.start, :]
            y_ref[0, chunk, :] = (y_c * jnp.tile(g_col, (1, d // GROUP))).astype(bf16)

    def run(first, last):
        weights = [w_ref[0].astype(bf16) for w_ref in (wg_ref, wu_ref, wd_ref)]
        for r0 in range(0, tmo, sub):
            rows = slice(r0, r0 + sub)
            y = partial_out(rows, *weights)
            if not first:
                y = acc_ref[rows, :] + y
            if last:
                store_gated(rows, y)
            else:
                acc_ref[rows, :] = y

    if nf == 1:
        run(True, True)
    else:
        pl.when(f == 0)(lambda: run(True, False))
        pl.when(f == nf - 1)(lambda: run(False, True))
        if nf > 2:
            pl.when(jnp.logical_and(f > 0, f < nf - 1))(lambda: run(False, False))


def _experts(xe, wg, wu, wd, gate_row, tmo, tf, sub):
    ne, m, d = xe.shape
    ff = wg.shape[2]
    return pl.pallas_call(
        functools.partial(_expert_kernel, sub=sub, nf=ff // tf),
        grid=(ne, m // tmo, ff // tf),
        in_specs=[
            pl.BlockSpec((1, tmo, d), lambda e, mi, fi: (e, mi, 0)),
            pl.BlockSpec((1, d, tf), lambda e, mi, fi: (e, 0, fi)),
            pl.BlockSpec((1, d, tf), lambda e, mi, fi: (e, 0, fi)),
            pl.BlockSpec((1, tf, d), lambda e, mi, fi: (e, fi, 0)),
            pl.BlockSpec((1, 1, tmo), lambda e, mi, fi: (e, 0, mi)),
        ],
        out_specs=pl.BlockSpec((1, tmo, d), lambda e, mi, fi: (e, mi, 0)),
        out_shape=jax.ShapeDtypeStruct((ne, m, d), bf16),
        scratch_shapes=[pltpu.VMEM((tmo, d), f32)],
        compiler_params=_params(("parallel", "parallel", "arbitrary")),
        name="experts",
    )(xe, wg, wu, wd, gate_row)


NORM_ROWS = 256


def _combine_kernel(idx_ref, ye_ref, x1_hbm, g_ref, out_hbm, acc_ref, ys_ref, stage_ref, sem_x1, sem_out,
                    *, ne, cap, s, do_norm):
    b = pl.program_id(0)
    e = pl.program_id(1)
    d = stage_ref.shape[2]
    nslab = d // GROUP
    pitch = acc_ref.shape[1] // nslab
    ypitch = ys_ref.shape[0] // nslab
    slot = b % 2
    group = ye_ref.shape[0]

    def x1_slab_copy(batch, into, c):
        rows = pl.ds(pl.multiple_of(batch * s, s), s)
        return pltpu.make_async_copy(x1_hbm.at[rows, c * GROUP:(c + 1) * GROUP],
                                     acc_ref.at[into, c * pitch:c * pitch + s, :], sem_x1.at[into])

    @pl.when(jnp.logical_and(b == 0, e == 0))
    def _():
        for c in range(nslab):
            x1_slab_copy(0, 0, c).start()

    @pl.when(e == 0)
    def _():
        @pl.when(b + 1 < pl.num_programs(0))
        def _():
            for c in range(nslab):
                x1_slab_copy(b + 1, 1 - slot, c).start()

        for c in range(nslab):
            x1_slab_copy(b, slot, c).wait()

    def accumulate(acc):
        for k in range(group):
            base = (b * ne + e * group + k) * cap
            for c in range(nslab):
                ys_ref[c * ypitch:c * ypitch + cap, :] = ye_ref[k, :, c * GROUP:(c + 1) * GROUP].astype(f32)

            def body(i, _, base=base):
                r0 = pl.multiple_of(i * SCATTER_ROWS, SCATTER_ROWS)
                toks = [idx_ref[base + r0 + j] for j in range(SCATTER_ROWS)]
                rows = [acc[pl.ds(tok, nslab, stride=pitch), :] for tok in toks]
                for j in range(SCATTER_ROWS):
                    acc[pl.ds(toks[j], nslab, stride=pitch), :] = \
                        rows[j] + ys_ref[pl.ds(r0 + j, nslab, stride=ypitch), :]
                return 0

            lax.fori_loop(0, cap // SCATTER_ROWS, body, 0)

        @pl.when(e == ne // group - 1)
        def _():
            nsteps = s // NORM_ROWS
            batch_row0 = pl.multiple_of(b * s, s)

            def out_copy(step):
                rows = pl.ds(batch_row0 + step * NORM_ROWS, NORM_ROWS)
                return pltpu.make_async_copy(stage_ref.at[step % 2], out_hbm.at[rows, :], sem_out.at[step % 2])

            for step in range(nsteps):
                lo = step * NORM_ROWS
                y = jnp.concatenate([acc[c * pitch + lo:c * pitch + lo + NORM_ROWS, :] for c in range(nslab)],
                                    axis=1)
                if do_norm:
                    ms = jnp.mean(y * y, axis=-1, keepdims=True)
                    y = y * lax.rsqrt(ms + EPS) * g_ref[...]
                if step >= 2:
                    out_copy(step - 2).wait()
                stage_ref[step % 2] = y
                out_copy(step).start()
            for step in range(max(nsteps - 2, 0), nsteps):
                out_copy(step).wait()

    for k in range(2):
        pl.when(slot == k)(functools.partial(accumulate, acc_ref.at[k]))


def _combine(idx_flat, ye, x1, g, nb, s, ne, cap, do_norm):
    d = ye.shape[2]
    nslab = d // GROUP
    group = _pick(ne, STEP_EXPERTS)
    return pl.pallas_call(
        functools.partial(_combine_kernel, ne=ne, cap=cap, s=s, do_norm=do_norm),
        grid_spec=pltpu.PrefetchScalarGridSpec(
            num_scalar_prefetch=1,
            grid=(nb, ne // group),
            in_specs=[
                pl.BlockSpec((group, cap, d), lambda bi, e, idx: (e, bi, 0)),
                pl.BlockSpec(memory_space=pl.ANY),
                pl.BlockSpec((1, d), lambda bi, e, idx: (0, 0)),
            ],
            out_specs=pl.BlockSpec(memory_space=pl.ANY),
            scratch_shapes=[
                pltpu.VMEM((2, nslab * _slab_pitch(s), GROUP), f32),
                pltpu.VMEM((nslab * _slab_pitch(cap), GROUP), f32),
                pltpu.VMEM((2, NORM_ROWS, d), f32),
                pltpu.SemaphoreType.DMA((2,)),
                pltpu.SemaphoreType.DMA((2,)),
            ],
        ),
        out_shape=jax.ShapeDtypeStruct((nb * s, d), f32),
        compiler_params=_params(("arbitrary", "arbitrary")),
        name="combine",
    )(idx_flat, ye, x1, g)


def _channel_dft_table():
    k = jnp.arange(GROUP, dtype=jnp.int32)
    ang = ((k[:, None] * k[None, :]) % GROUP).astype(f32) * (2.0 * math.pi / GROUP)
    return jnp.concatenate([jnp.cos(ang), jnp.sin(ang)], axis=1).astype(bf16)


def _sequence_dft_tables(s, ts, scale):
    half = s // 2
    inner = 64
    k = jnp.arange(half, dtype=jnp.int32)[None, :]
    n1 = jnp.arange(half // inner, dtype=jnp.int32)[:, None] * inner
    n2 = jnp.arange(inner, dtype=jnp.int32)[:, None]
    w = 2.0 * math.pi / s
    a1 = ((n1 * k) % s).astype(f32) * w
    a2 = ((n2 * k) % s).astype(f32) * w
    c1, s1 = jnp.cos(a1)[:, None, :], jnp.sin(a1)[:, None, :]
    c2, s2 = jnp.cos(a2)[None, :, :] * scale, jnp.sin(a2)[None, :, :] * scale
    cp = (c1 * c2 - s1 * s2).reshape(half, half).astype(bf16)
    sm = (-(s1 * c2 + c1 * s2)).reshape(half, half).astype(bf16)
    rows = jnp.arange(ts, dtype=jnp.int32)[:, None]
    cols = jnp.arange(ts, dtype=jnp.int32)[None, :]
    flip = (cols == ts - rows).astype(bf16)
    kk = jnp.arange(half, dtype=jnp.int32)[None, :]
    alt = jnp.where(jnp.arange(8)[:, None] == 0, (1 - 2 * (kk & 1)).astype(f32) * scale, 0.0).astype(bf16)
    fold = min(FOLD_ROWS, half)
    r = jnp.arange(fold, dtype=jnp.int32)
    j1 = ((r[None, :] == fold - r[:, None]) & (r[:, None] >= 1)).astype(bf16)
    return cp, sm, flip, alt, j1


def _pick(n, pref):
    return pref if n % pref == 0 else n


def kernel(x, norm1_g, w_in, sgu_ln_g, sgu_ln_b, w_spatial, b_spatial, w_fourier_out, w_sgu_out, w_out,
           norm2_g, w_router, w_gate_e, w_up_e, w_down_e, final_g):
    nb, s, d = x.shape
    depth = norm1_g.shape[0]
    ne = w_router.shape[2]
    ff = w_gate_e.shape[3]
    cap = CAPACITY_FACTOR * s // ne
    t = nb * s
    assert s % NORM_ROWS == 0 and s & (s - 1) == 0 and cap % GATHER_ROWS == 0 and d % GROUP == 0
    assert (nb * cap) % GROUP == 0

    tm_front = _pick(s, 1024)
    ts_dft = _pick(s // 2, 512)
    tm_post = _pick(s // 2, 1024)
    dft_scale = 1.0 / math.sqrt(s * GROUP)
    cs_tab = _channel_dft_table()
    seq_tables = _sequence_dft_tables(s, ts_dft, dft_scale)
    tmo = _pick(nb * cap, 2048)
    tf = _pick(ff, 512)
    sub = _pick(tmo, 1024)

    x2 = x.reshape(t, d)
    for l in range(depth):
        a, b, ysg, sgf = _front(
            x2, norm1_g[l][None, :], w_in[l].astype(bf16), cs_tab,
            sgu_ln_g[l][None, :], sgu_ln_b[l][None, :], w_spatial[l].astype(bf16),
            b_spatial[l].T, w_sgu_out[l].astype(bf16), tm_front)
        fm_first, fm_second = _seqdft(seq_tables, a, b, nb, s, ts_dft, dft_scale)
        wr = jnp.pad(w_router[l], ((0, 0), (0, ROUTER_LANES - ne)))
        wr_hi = wr.astype(bf16)
        wr_lo = (wr - wr_hi.astype(f32)).astype(bf16)
        wr_split = jnp.concatenate([jnp.concatenate([wr_hi, wr_lo], axis=1),
                                    jnp.concatenate([wr_hi, jnp.zeros_like(wr_hi)], axis=1)], axis=0)
        x1, h2, aff = _post(fm_first, fm_second, sgf, ysg, x2, w_fourier_out[l].astype(bf16),
                            w_out[l].astype(bf16), norm2_g[l][None, :], wr_split, ne, s, tm_post)
        idx, gate = _select(aff, nb, s, cap)
        idx_flat = idx.reshape(-1)
        xe = _gather(idx_flat, h2, nb, s, ne, cap)
        gate_row = jnp.transpose(gate, (1, 0, 2)).reshape(ne, 1, nb * cap)
        ye = _experts(xe, w_gate_e[l], w_up_e[l], w_down_e[l], gate_row, tmo, tf, sub)
        x2 = _combine(idx_flat, ye, x1, final_g[None, :], nb, s, ne, cap, l == depth - 1)
    return x2.reshape(nb, s, d)
```
